```python
import math, functools
import jax, jax.numpy as jnp
from jax import lax
import numpy as np

D_MODEL = 1024
BATCH = 8
SEQ = 2048
DEPTH = 2
DEC_BATCH = 32
DEC_SEQ = 1
PAST_LEN = 16384
PAGE_SIZE = 128

CONV_W = 4
D_RNN = 1280
RG_BLOCK = 128
RG_BLOCKS = D_RNN // RG_BLOCK
RG_C = 8.0
GDN_HEADS = 8
GDN_DK = 128
GDN_DV = 128
GDN_KW = GDN_HEADS * GDN_DK
GDN_VW = GDN_HEADS * GDN_DV
GDN_CONV_C = 2 * GDN_KW + GDN_VW
GDN_CHUNK = 64
MLA_HEADS = 8
Q_LORA = 384
KV_LORA = 256
QK_NOPE = 64
QK_ROPE = 32
V_HEAD = 128
MLA_VW = MLA_HEADS * V_HEAD
MLA_SCALE = (QK_NOPE + QK_ROPE) ** -0.5
ROPE_BASE = 10000.0
Q_BLOCK = 128
D_FF = ((8 * D_MODEL // 3 + 255) // 256) * 256
N_BRANCH = 3
IN_SIZES = (D_RNN, D_RNN, GDN_CONV_C, GDN_VW, GDN_HEADS, GDN_HEADS, Q_LORA, KV_LORA + QK_ROPE, N_BRANCH * D_MODEL)
N_IN = sum(IN_SIZES)
EPS = 1e-6

kernel_name = 'hybrid_rglru_gdn_mla_adaln_step'


def rmsnorm(x, g):
    xf = x.astype(jnp.float32)
    y = xf * lax.rsqrt(jnp.mean(xf * xf, axis=-1, keepdims=True) + EPS)
    return (y * g.astype(jnp.float32)).astype(x.dtype)


def l2norm(x):
    xf = x.astype(jnp.float32)
    return xf * lax.rsqrt(jnp.sum(xf * xf, axis=-1, keepdims=True) + EPS)


def causal_conv(u, buf, w):
    T = u.shape[1]
    up = jnp.concatenate([buf.astype(u.dtype), u], axis=1)
    y = up[:, 0:T] * w[0]
    for j in range(1, CONV_W):
        y = y + up[:, j:j + T] * w[j]
    return y, up[:, T:]


def linear_recurrence(a, b, h0):
    def combine(left, right):
        al, bl = left
        ar, br = right
        return ar * al, ar * bl + br
    a_cum, h = lax.associative_scan(combine, (a, b), axis=1)
    return h + a_cum * h0[:, None, :]


def rglru_branch(u_x, u_y, buf, h0, conv_w, conv_b, wa, ba, wx, bx, lam):
    xc, new_buf = causal_conv(u_x, buf, conv_w)
    xc = xc + conv_b
    B, T, _ = xc.shape
    xb = xc.reshape(B, T, RG_BLOCKS, RG_BLOCK)
    r = jax.nn.sigmoid(jnp.einsum('btnj,njk->btnk', xb, wa).reshape(B, T, D_RNN) + ba)
    i = jax.nn.sigmoid(jnp.einsum('btnj,njk->btnk', xb, wx).reshape(B, T, D_RNN) + bx)
    log_a = -RG_C * r.astype(jnp.float32) * jax.nn.softplus(-lam.astype(jnp.float32))
    a = jnp.exp(log_a)
    b = jnp.sqrt(-jnp.expm1(2.0 * log_a)) * (i * xc).astype(jnp.float32)
    h = linear_recurrence(a, b, h0.astype(jnp.float32))
    out = h.astype(u_x.dtype) * jax.nn.gelu(u_y)
    return out, new_buf, h[:, -1]


def gated_delta_chunked(q, k, v, g, beta, S0):
    B, T, H, _ = q.shape
    C = GDN_CHUNK
    n = -(-T // C)
    pad = n * C - T
    f32 = jnp.float32

    def prep4(t):
        t = jnp.pad(t.astype(f32), ((0, 0), (0, pad), (0, 0), (0, 0)))
        return t.reshape(B, n, C, H, t.shape[-1]).transpose(1, 0, 3, 2, 4)

    def prep3(t):
        t = jnp.pad(t.astype(f32), ((0, 0), (0, pad), (0, 0)))
        return t.reshape(B, n, C, H).transpose(1, 0, 3, 2)

    tri_strict = jnp.tril(jnp.ones((C, C), bool), -1)
    tri_incl = jnp.tril(jnp.ones((C, C), bool))
    eye = jnp.eye(C, dtype=f32)

    def step(S, inp):
        qc, kc, vc, gc, bc = inp
        gcum = jnp.cumsum(gc, axis=-1)
        diff = gcum[..., :, None] - gcum[..., None, :]
        dec_strict = jnp.exp(jnp.where(tri_strict, diff, -jnp.inf))
        dec_incl = jnp.exp(jnp.where(tri_incl, diff, -jnp.inf))
        L = bc[..., :, None] * jnp.einsum('bhid,bhjd->bhij', kc, kc) * dec_strict
        rhs = bc[..., None] * (vc - jnp.exp(gcum)[..., None] * jnp.einsum('bhid,bhde->bhie', kc, S))
        U = lax.linalg.triangular_solve(eye + L, rhs, left_side=True, lower=True)
        qk = jnp.einsum('bhid,bhjd->bhij', qc, kc) * dec_incl
        o = jnp.exp(gcum)[..., None] * jnp.einsum('bhid,bhde->bhie', qc, S) + jnp.einsum('bhij,bhje->bhie', qk, U)
        g_last = gcum[..., -1]
        k_dec = kc * jnp.exp(g_last[..., None] - gcum)[..., None]
        S_new = jnp.exp(g_last)[..., None, None] * S + jnp.einsum('bhjd,bhje->bhde', k_dec, U)
        return S_new, o

    S, o = lax.scan(step, S0.astype(f32), (prep4(q), prep4(k), prep4(v), prep3(g), prep3(beta)))
    o = o.transpose(1, 0, 3, 2, 4).reshape(B, n * C, H, -1)[:, :T]
    return o, S


def gdn_branch(u_qkv, u_z, u_a, u_b, buf, S0, conv_w, A_log, dt_bias, norm_g):
    qkv, new_buf = causal_conv(u_qkv, buf, conv_w)
    qkv = jax.nn.silu(qkv)
    B, T, _ = qkv.shape
    q, k, v = jnp.split(qkv, [GDN_KW, 2 * GDN_KW], axis=-1)
    q = l2norm(q.reshape(B, T, GDN_HEADS, GDN_DK)) * (GDN_DK ** -0.5)
    k = l2norm(k.reshape(B, T, GDN_HEADS, GDN_DK))
    v = v.reshape(B, T, GDN_HEADS, GDN_DV)
    beta = jax.nn.sigmoid(u_b.astype(jnp.float32))
    g = -jnp.exp(A_log.astype(jnp.float32)) * jax.nn.softplus(u_a.astype(jnp.float32) + dt_bias.astype(jnp.float32))
    o, S = gated_delta_chunked(q, k, v, g, beta, S0)
    o = rmsnorm(o.astype(u_z.dtype), norm_g) * jax.nn.silu(u_z.reshape(B, T, GDN_HEADS, GDN_DV))
    return o.reshape(B, T, GDN_VW), new_buf, S


def rope_cos_sin(T, pos0):
    inv = ROPE_BASE ** (-jnp.arange(0, QK_ROPE, 2, dtype=jnp.float32) / QK_ROPE)
    ang = (jnp.arange(T, dtype=jnp.float32) + pos0)[:, None] * inv[None, :]
    return jnp.cos(ang), jnp.sin(ang)


def apply_rope(x, cos, sin):
    x1, x2 = jnp.split(x.astype(jnp.float32), 2, axis=-1)
    return jnp.concatenate([x1 * cos - x2 * sin, x2 * cos + x1 * sin], axis=-1).astype(x.dtype)


def mla_prompt_attention(q_nope, q_rope, c_kv, k_rope, w_ukv):
    B, S, H, _ = q_nope.shape
    kv = jnp.einsum('bsc,cf->bsf', c_kv, w_ukv).reshape(B, S, H, QK_NOPE + V_HEAD)
    k_nope, v = kv[..., :QK_NOPE], kv[..., QK_NOPE:]
    nb = S // Q_BLOCK
    qn = q_nope.reshape(B, nb, Q_BLOCK, H, QK_NOPE).transpose(1, 0, 2, 3, 4)
    qr = q_rope.reshape(B, nb, Q_BLOCK, H, QK_ROPE).transpose(1, 0, 2, 3, 4)
    key_pos = jnp.arange(S)

    def block(args):
        qn_b, qr_b, start = args
        s = (jnp.einsum('bqhd,bkhd->bhqk', qn_b, k_nope)
             + jnp.einsum('bqhr,bkr->bhqk', qr_b, k_rope)).astype(jnp.float32) * MLA_SCALE
        qpos = start + jnp.arange(Q_BLOCK)
        s = jnp.where(key_pos[None, :] <= qpos[:, None], s, -jnp.inf)
        p = jax.nn.softmax(s, axis=-1).astype(v.dtype)
        return jnp.einsum('bhqk,bkhe->bqhe', p, v)

    o = lax.map(block, (qn, qr, jnp.arange(nb) * Q_BLOCK))
    return o.transpose(1, 0, 2, 3, 4).reshape(B, S, H * V_HEAD)


def mla_sample_attention(q_nope, q_rope, c_kv, k_rope, w_ukv, ckv_pool, krope_pool, page_table, layer):
    Bd, T, H, _ = q_nope.shape
    w = w_ukv.reshape(KV_LORA, H, QK_NOPE + V_HEAD)
    w_uk, w_uv = w[..., :QK_NOPE], w[..., QK_NOPE:]
    q_lat = jnp.einsum('bthd,chd->bthc', q_nope, w_uk)
    ckv_past = ckv_pool[layer, page_table].reshape(Bd, -1, KV_LORA)
    kr_past = krope_pool[layer, page_table].reshape(Bd, -1, QK_ROPE)
    P = ckv_past.shape[1]
    s_past = jnp.einsum('bthc,bpc->bthp', q_lat, ckv_past) + jnp.einsum('bthr,bpr->bthp', q_rope, kr_past)
    s_new = jnp.einsum('bthc,bsc->bths', q_lat, c_kv) + jnp.einsum('bthr,bsr->bths', q_rope, k_rope)
    causal = jnp.tril(jnp.ones((T, T), bool))
    s_new = jnp.where(causal[None, :, None, :], s_new, -jnp.inf)
    s = jnp.concatenate([s_past, s_new], axis=-1).astype(jnp.float32) * MLA_SCALE
    p = jax.nn.softmax(s, axis=-1).astype(c_kv.dtype)
    o_lat = jnp.einsum('bthp,bpc->bthc', p[..., :P], ckv_past) + jnp.einsum('bths,bsc->bthc', p[..., P:], c_kv)
    return jnp.einsum('bthc,che->bthe', o_lat, w_uv).reshape(Bd, T, H * V_HEAD)


def hybrid_layer(x, c, pos0, rg_buf, rg_h0, gdn_buf, gdn_S0, attend,
                 w_ada, b_ada, g_norm1, g_norm2, w_in,
                 rg_conv_w, rg_conv_b, rg_wa, rg_ba, rg_wx, rg_bx, rg_lambda,
                 gdn_conv_w, gdn_A_log, gdn_dt_bias, gdn_norm_g,
                 mla_q_norm_g, w_uq, mla_kv_norm_g, w_ukv,
                 w_rg_proj, w_gdn_proj, w_mla_proj, w_o, w_ffn_in, w_ffn_out):
    B, T, _ = x.shape
    mod = jnp.einsum('bd,df->bf', jax.nn.silu(c), w_ada) + b_ada
    sh1, sc1, gt1, sh2, sc2, gt2 = jnp.split(mod[:, None, :], 6, axis=-1)

    h = rmsnorm(x, g_norm1) * (1.0 + sc1) + sh1
    u = jnp.einsum('btd,df->btf', h, w_in)
    offs = np.cumsum(IN_SIZES)[:-1].tolist()
    u_rx, u_ry, u_qkv, u_z, u_a, u_b, u_mq, u_mkv, u_gate = jnp.split(u, offs, axis=-1)

    o_rg, rg_buf_new, rg_h = rglru_branch(u_rx, u_ry, rg_buf, rg_h0, rg_conv_w, rg_conv_b,
                                          rg_wa, rg_ba, rg_wx, rg_bx, rg_lambda)
    o_gdn, gdn_buf_new, gdn_S = gdn_branch(u_qkv, u_z, u_a, u_b, gdn_buf, gdn_S0,
                                           gdn_conv_w, gdn_A_log, gdn_dt_bias, gdn_norm_g)
    cos, sin = rope_cos_sin(T, pos0)
    cq = rmsnorm(u_mq, mla_q_norm_g)
    q = jnp.einsum('btc,cf->btf', cq, w_uq).reshape(B, T, MLA_HEADS, QK_NOPE + QK_ROPE)
    q_nope = q[..., :QK_NOPE]
    q_rope = apply_rope(q[..., QK_NOPE:], cos[:, None, :], sin[:, None, :])
    c_kv = rmsnorm(u_mkv[..., :KV_LORA], mla_kv_norm_g)
    k_rope = apply_rope(u_mkv[..., KV_LORA:], cos, sin)
    o_mla = attend(q_nope, q_rope, c_kv, k_rope, w_ukv)

    ga, gb, gc = jnp.split(u_gate, N_BRANCH, axis=-1)
    m = (jax.nn.sigmoid(ga) * jnp.einsum('btf,fd->btd', o_rg, w_rg_proj)
         + jax.nn.sigmoid(gb) * jnp.einsum('btf,fd->btd', o_gdn, w_gdn_proj)
         + jax.nn.sigmoid(gc) * jnp.einsum('btf,fd->btd', o_mla, w_mla_proj))
    x = x + gt1 * jnp.einsum('btd,de->bte', m, w_o)

    h2 = rmsnorm(x, g_norm2) * (1.0 + sc2) + sh2
    gate, up = jnp.split(jnp.einsum('btd,df->btf', h2, w_ffn_in), 2, axis=-1)
    x = x + gt2 * jnp.einsum('btf,fd->btd', jax.nn.silu(gate) * up, w_ffn_out)
    return x, (c_kv, k_rope, rg_buf_new, rg_h, gdn_buf_new, gdn_S)


def setup_inputs(seed: int = 0) -> dict:
    key = jax.random.key(seed)
    keys = iter(jax.random.split(key, 64))
    f32 = jnp.float32
    L = DEPTH

    def nrm(shape, scale):
        return jax.random.normal(next(keys), shape, f32) * scale

    def gain(shape):
        return 1.0 + nrm(shape, 0.02)

    n_pages = PAST_LEN // PAGE_SIZE
    n_pool = (DEC_BATCH * n_pages * 5) // 4
    perm = jax.random.permutation(next(keys), n_pool)
    page_table = perm[:DEC_BATCH * n_pages].reshape(DEC_BATCH, n_pages).astype(jnp.int32)

    u_lam = jax.random.uniform(next(keys), (L, D_RNN), f32, 0.9, 0.999)
    s_lam = u_lam ** (1.0 / RG_C)
    rg_lambda = jnp.log(s_lam) - jnp.log1p(-s_lam)
    A = jax.random.uniform(next(keys), (L, GDN_HEADS), f32, 1.0, 16.0)
    dt = jnp.exp(jax.random.uniform(next(keys), (L, GDN_HEADS), f32, math.log(1e-3), math.log(1e-1)))
    dt_bias = dt + jnp.log(-jnp.expm1(-dt))

    return {
        'x_prompt': nrm((BATCH, SEQ, D_MODEL), 1.0),
        'x_sample': nrm((DEC_BATCH, DEC_SEQ, D_MODEL), 1.0),
        'cache_ckv': nrm((L, n_pool, PAGE_SIZE, KV_LORA), 1.0),
        'cache_krope': nrm((L, n_pool, PAGE_SIZE, QK_ROPE), 1.0),
        'state_rg_conv': nrm((L, DEC_BATCH, CONV_W - 1, D_RNN), 1.0),
        'state_rg_h': nrm((L, DEC_BATCH, D_RNN), 1.0),
        'state_gdn_conv': nrm((L, DEC_BATCH, CONV_W - 1, GDN_CONV_C), 1.0),
        'state_gdn_S': nrm((L, DEC_BATCH, GDN_HEADS, GDN_DK, GDN_DV), 0.5),
        'page_table': page_table,
        'c_prompt': nrm((BATCH, D_MODEL), 1.0),
        'c_sample': nrm((DEC_BATCH, D_MODEL), 1.0),
        'w_ada': nrm((L, D_MODEL, 6 * D_MODEL), 0.5 * D_MODEL ** -0.5),
        'b_ada': nrm((L, 6 * D_MODEL), 0.02),
        'g_norm1': gain((L, D_MODEL)),
        'g_norm2': gain((L, D_MODEL)),
        'w_in': nrm((L, D_MODEL, N_IN), D_MODEL ** -0.5),
        'rg_conv_w': nrm((L, CONV_W, D_RNN), CONV_W ** -0.5),
        'rg_conv_b': nrm((L, D_RNN), 0.02),
        'rg_wa': nrm((L, RG_BLOCKS, RG_BLOCK, RG_BLOCK), RG_BLOCK ** -0.5),
        'rg_ba': nrm((L, D_RNN), 0.02),
        'rg_wx': nrm((L, RG_BLOCKS, RG_BLOCK, RG_BLOCK), RG_BLOCK ** -0.5),
        'rg_bx': nrm((L, D_RNN), 0.02),
        'rg_lambda': rg_lambda,
        'gdn_conv_w': nrm((L, CONV_W, GDN_CONV_C), CONV_W ** -0.5),
        'gdn_A_log': jnp.log(A),
        'gdn_dt_bias': dt_bias,
        'gdn_norm_g': gain((L, GDN_DV)),
        'mla_q_norm_g': gain((L, Q_LORA)),
        'w_uq': nrm((L, Q_LORA, MLA_HEADS * (QK_NOPE + QK_ROPE)), Q_LORA ** -0.5),
        'mla_kv_norm_g': gain((L, KV_LORA)),
        'w_ukv': nrm((L, KV_LORA, MLA_HEADS * (QK_NOPE + V_HEAD)), KV_LORA ** -0.5),
        'w_rg_proj': nrm((L, D_RNN, D_MODEL), D_RNN ** -0.5),
        'w_gdn_proj': nrm((L, GDN_VW, D_MODEL), GDN_VW ** -0.5),
        'w_mla_proj': nrm((L, MLA_VW, D_MODEL), MLA_VW ** -0.5),
        'w_o': nrm((L, D_MODEL, D_MODEL), D_MODEL ** -0.5),
        'w_ffn_in': nrm((L, D_MODEL, 2 * D_FF), D_MODEL ** -0.5),
        'w_ffn_out': nrm((L, D_FF, D_MODEL), D_FF ** -0.5),
        'g_final': gain((D_MODEL,)),
    }


def reference(x_prompt, x_sample, cache_ckv, cache_krope, state_rg_conv, state_rg_h, state_gdn_conv, state_gdn_S,
              page_table, c_prompt, c_sample,
              w_ada, b_ada, g_norm1, g_norm2, w_in,
              rg_conv_w, rg_conv_b, rg_wa, rg_ba, rg_wx, rg_bx, rg_lambda,
              gdn_conv_w, gdn_A_log, gdn_dt_bias, gdn_norm_g,
              mla_q_norm_g, w_uq, mla_kv_norm_g, w_ukv,
              w_rg_proj, w_gdn_proj, w_mla_proj, w_o, w_ffn_in, w_ffn_out, g_final):
    xp, xs = x_prompt, x_sample
    Bp = xp.shape[0]
    zb_rg = jnp.zeros((Bp, CONV_W - 1, D_RNN), xp.dtype)
    zh_rg = jnp.zeros((Bp, D_RNN), xp.dtype)
    zb_gdn = jnp.zeros((Bp, CONV_W - 1, GDN_CONV_C), xp.dtype)
    zS_gdn = jnp.zeros((Bp, GDN_HEADS, GDN_DK, GDN_DV), jnp.float32)
    outs_p = []
    outs_s = []
    for l in range(DEPTH):
        lp = (w_ada[l], b_ada[l], g_norm1[l], g_norm2[l], w_in[l],
              rg_conv_w[l], rg_conv_b[l], rg_wa[l], rg_ba[l], rg_wx[l], rg_bx[l], rg_lambda[l],
              gdn_conv_w[l], gdn_A_log[l], gdn_dt_bias[l], gdn_norm_g[l],
              mla_q_norm_g[l], w_uq[l], mla_kv_norm_g[l], w_ukv[l],
              w_rg_proj[l], w_gdn_proj[l], w_mla_proj[l], w_o[l], w_ffn_in[l], w_ffn_out[l])
        xp, st_p = hybrid_layer(xp, c_prompt, 0.0, zb_rg, zh_rg, zb_gdn, zS_gdn, mla_prompt_attention, *lp)
        sample_attend = functools.partial(mla_sample_attention, ckv_pool=cache_ckv, krope_pool=cache_krope,
                                          page_table=page_table, layer=l)
        xs, st_s = hybrid_layer(xs, c_sample, float(PAST_LEN), state_rg_conv[l], state_rg_h[l],
                                state_gdn_conv[l], state_gdn_S[l], sample_attend, *lp)
        outs_p.append(st_p)
        outs_s.append(st_s)
    y_prompt = rmsnorm(xp, g_final)
    y_sample = rmsnorm(xs, g_final)
    ckv_p = jnp.stack([o[0] for o in outs_p])
    krope_p = jnp.stack([o[1] for o in outs_p])
    rg_conv_p = jnp.stack([o[2] for o in outs_p])
    rg_h_p = jnp.stack([o[3] for o in outs_p])
    gdn_conv_p = jnp.stack([o[4] for o in outs_p])
    gdn_S_p = jnp.stack([o[5] for o in outs_p])
    ckv_s = jnp.stack([o[0] for o in outs_s])
    krope_s = jnp.stack([o[1] for o in outs_s])
    rg_conv_s = jnp.stack([o[2] for o in outs_s])
    rg_h_s = jnp.stack([o[3] for o in outs_s])
    gdn_conv_s = jnp.stack([o[4] for o in outs_s])
    gdn_S_s = jnp.stack([o[5] for o in outs_s])
    return (y_prompt, y_sample, ckv_p, krope_p, rg_conv_p, rg_h_p, gdn_conv_p, gdn_S_p,
            ckv_s, krope_s, rg_conv_s, rg_h_s, gdn_conv_s, gdn_S_s)
```

```python
import functools
import math

import numpy as np
import jax
import jax.numpy as jnp
from jax import lax
from jax.experimental import pallas as pl
from jax.experimental.pallas import tpu as pltpu

F32 = jnp.float32
BF16 = jnp.bfloat16

D_MODEL = 1024
CONV_W = 4
D_RNN = 1280
RG_BLOCK = 128
RG_BLOCKS = D_RNN // RG_BLOCK
RG_C = 8.0
GDN_HEADS = 8
GDN_DK = 128
GDN_DV = 128
GDN_KW = GDN_HEADS * GDN_DK
GDN_VW = GDN_HEADS * GDN_DV
GDN_CONV_C = 2 * GDN_KW + GDN_VW
GDN_CHUNK = 64
MLA_HEADS = 8
Q_LORA = 384
KV_LORA = 256
QK_NOPE = 64
QK_ROPE = 32
V_HEAD = 128
MLA_SCALE = (QK_NOPE + QK_ROPE) ** -0.5
ROPE_BASE = 10000.0
D_FF = 2816
N_BRANCH = 3
IN_SIZES = (D_RNN, D_RNN, GDN_CONV_C, GDN_VW, GDN_HEADS, GDN_HEADS, Q_LORA, KV_LORA + QK_ROPE, N_BRANCH * D_MODEL)
EPS = 1e-6
PAGE_SIZE = 128

LANES = 128
SUBLANES = 8
VMEM_LIMIT = 56 * 1024 * 1024

U_RX = 0
U_RY = D_RNN
U_QKV = 3072
U_Z = 6144
U_GA = 7168
U_GB = 8192
U_GC = 9216
U_AB = 10240
U_MQ = 10368
U_MKV = 10752
N_U = 11264
W_AB = LANES
W_MKV = 512
ROPE_LANE0 = QK_NOPE


def _cparams(sem):
    return pltpu.CompilerParams(dimension_semantics=sem, vmem_limit_bytes=VMEM_LIMIT)


def _sigmoid(x):
    return 1.0 / (1.0 + jnp.exp(-x))


def _silu(x):
    return x * _sigmoid(x)


def _softplus(x):
    return jnp.maximum(x, 0.0) + jnp.log1p(jnp.exp(-jnp.abs(x)))


def _gelu_tanh(x):
    c = math.sqrt(2.0 / math.pi)
    return 0.5 * x * (1.0 + jnp.tanh(c * (x + 0.044715 * (x * x * x))))


def _rms(x, g):
    return x * lax.rsqrt(jnp.mean(x * x, axis=-1, keepdims=True) + EPS) * g


def _dot(a, b):
    return jnp.dot(a.astype(BF16), b.astype(BF16), preferred_element_type=F32)


def _dot_nt(a, b):
    return lax.dot_general(a.astype(BF16), b.astype(BF16), (((1,), (1,)), ((), ())), preferred_element_type=F32)


def _dot_tn(a, b):
    return lax.dot_general(a.astype(BF16), b.astype(BF16), (((0,), (0,)), ((), ())), preferred_element_type=F32)


def _split2(a):
    hi = a.astype(BF16)
    lo = (a - hi.astype(F32)).astype(BF16)
    return hi, lo


def _dot3(a, b):
    ah, al = _split2(a)
    bh, bl = _split2(b)
    d = functools.partial(jnp.dot, preferred_element_type=F32)
    return d(ah, bh) + (d(ah, bl) + d(al, bh))


def _dot_exact_lhs(a_bf16, b):
    b0 = b.astype(BF16)
    r1 = b - b0.astype(F32)
    b1 = r1.astype(BF16)
    b2 = (r1 - b1.astype(F32)).astype(BF16)
    d = functools.partial(jnp.dot, preferred_element_type=F32)
    return d(a_bf16, b0) + (d(a_bf16, b1) + d(a_bf16, b2))


def _mod_kernel(c_ref, w_ref, b_ref, o_ref):
    c = c_ref[...]
    o_ref[0] = _dot(_silu(c), w_ref[0]) + b_ref[0]


def _modulation(c_all, w_ada, b_ada):
    L, D, N = w_ada.shape
    R = c_all.shape[0]
    bn = 1536
    return pl.pallas_call(
        _mod_kernel,
        grid=(L, N // bn),
        in_specs=[pl.BlockSpec((R, D), lambda l, j: (0, 0)),
                  pl.BlockSpec((1, D, bn), lambda l, j: (l, 0, j)),
                  pl.BlockSpec((1, 1, bn), lambda l, j: (l, 0, j))],
        out_specs=pl.BlockSpec((1, R, bn), lambda l, j: (l, 0, j)),
        out_shape=jax.ShapeDtypeStruct((L, R, N), F32),
        name="adaln_mod",
        compiler_params=_cparams(("parallel", "parallel")),
    )(c_all, w_ada, b_ada.reshape(L, 1, N))


def _mod_spec(mod, bm):
    _, tm, d = mod.shape
    if tm == 1:
        return pl.BlockSpec((1, 1, d), lambda b, i, *_: (b, 0, 0))
    return pl.BlockSpec((1, bm, d), lambda b, i, *_: (b, i, 0))


def _in_kernel(x_ref, sc_ref, sh_ref, g_ref, w_ref, o_ref, hb_ref):
    @pl.when(pl.program_id(2) == 0)
    def _():
        h = _rms(x_ref[0], g_ref[...]) * (1.0 + sc_ref[0]) + sh_ref[0]
        hb_ref[...] = h.astype(BF16)

    o_ref[0] = jnp.dot(hb_ref[...], w_ref[...], preferred_element_type=F32)


def _in_proj(x, sc, sh, g, w_packed, bm, bn):
    B, T, D = x.shape
    N = w_packed.shape[1]
    return pl.pallas_call(
        _in_kernel,
        grid=(B, T // bm, N // bn),
        in_specs=[pl.BlockSpec((1, bm, D), lambda b, i, j: (b, i, 0)),
                  _mod_spec(sc, bm), _mod_spec(sh, bm),
                  pl.BlockSpec((1, D), lambda b, i, j: (0, 0)),
                  pl.BlockSpec((D, bn), lambda b, i, j: (0, j))],
        out_specs=pl.BlockSpec((1, bm, bn), lambda b, i, j: (b, i, j)),
        out_shape=jax.ShapeDtypeStruct((B, T, N), F32),
        scratch_shapes=[pltpu.VMEM((bm, D), BF16)],
        name="in_proj",
        compiler_params=_cparams(("parallel", "parallel", "arbitrary")),
    )(x, sc, sh, g.reshape(1, D), w_packed)


SCAN_PAD = 64


def _rg_kernel(ux_ref, uy_ref, buf_ref, h0_ref, cw_ref, cb_ref, wa_ref, ba_ref, wx_ref, bx_ref, lam_ref,
               o_ref, nbuf_ref, hl_ref, xbuf, abuf, bbuf, hc, *, tt):
    i = pl.program_id(1)
    nt = pl.num_programs(1)

    @pl.when(i == 0)
    def _():
        xbuf[0:SUBLANES, :] = buf_ref[0]
        hc[...] = h0_ref[0]
        abuf[0:SCAN_PAD, :] = jnp.ones((SCAN_PAD, D_RNN), F32)
        bbuf[0:SCAN_PAD, :] = jnp.zeros((SCAN_PAD, D_RNN), F32)

    xbuf[SUBLANES:SUBLANES + tt, :] = ux_ref[0]
    lo = SUBLANES - (CONV_W - 1)
    xc = cb_ref[...] + cw_ref[0:1, :] * xbuf[lo:lo + tt, :]
    for j in range(1, CONV_W):
        xc = xc + cw_ref[j:j + 1, :] * xbuf[lo + j:lo + j + tt, :]

    @pl.when(i == nt - 1)
    def _():
        nbuf_ref[0] = xbuf[tt + lo:tt + SUBLANES, :]

    if tt >= SUBLANES:
        xbuf[0:SUBLANES, :] = xbuf[tt:tt + SUBLANES, :]

    xb = xc.astype(BF16)
    ra = jnp.concatenate([jnp.dot(xb[:, n * RG_BLOCK:(n + 1) * RG_BLOCK], wa_ref[n], preferred_element_type=F32)
                          for n in range(RG_BLOCKS)], axis=1)
    ri = jnp.concatenate([jnp.dot(xb[:, n * RG_BLOCK:(n + 1) * RG_BLOCK], wx_ref[n], preferred_element_type=F32)
                          for n in range(RG_BLOCKS)], axis=1)
    r = _sigmoid(ra + ba_ref[...])
    ig = _sigmoid(ri + bx_ref[...])
    log_a = (-RG_C) * r * _softplus(-lam_ref[...])
    a = jnp.exp(log_a)
    b = jnp.sqrt(-jnp.tanh(log_a) * (a * a + 1.0)) * (ig * xc)

    abuf[SCAN_PAD:SCAN_PAD + tt, :] = a
    bbuf[SCAN_PAD:SCAN_PAD + tt, :] = b
    d = 1
    while d < tt:
        a_s = abuf[SCAN_PAD - d:SCAN_PAD - d + tt, :]
        b_s = bbuf[SCAN_PAD - d:SCAN_PAD - d + tt, :]
        a0 = abuf[SCAN_PAD:SCAN_PAD + tt, :]
        b0 = bbuf[SCAN_PAD:SCAN_PAD + tt, :]
        abuf[SCAN_PAD:SCAN_PAD + tt, :] = a0 * a_s
        bbuf[SCAN_PAD:SCAN_PAD + tt, :] = a0 * b_s + b0
        d *= 2
    h = bbuf[SCAN_PAD:SCAN_PAD + tt, :] + abuf[SCAN_PAD:SCAN_PAD + tt, :] * hc[...]
    hc[...] = h[tt - 1:tt, :]
    o_ref[0] = (h * _gelu_tanh(uy_ref[0])).astype(BF16)

    @pl.when(i == nt - 1)
    def _():
        hl_ref[0] = h[tt - 1:tt, :]


def _rglru(u, buf, h0, cw, cb, wa, ba, wx, bx, lam, tt):
    B, T, _ = u.shape
    C = D_RNN
    buf8 = jnp.pad(buf, ((0, 0), (SUBLANES - (CONV_W - 1), 0), (0, 0)))
    vec = lambda v: v.reshape(1, C)
    const2 = lambda shape: pl.BlockSpec(shape, lambda b, i: (0,) * len(shape))
    return pl.pallas_call(
        functools.partial(_rg_kernel, tt=tt),
        grid=(B, T // tt),
        in_specs=[pl.BlockSpec((1, tt, C), lambda b, i: (b, i, U_RX // C)),
                  pl.BlockSpec((1, tt, C), lambda b, i: (b, i, U_RY // C)),
                  pl.BlockSpec((1, SUBLANES, C), lambda b, i: (b, 0, 0)),
                  pl.BlockSpec((1, 1, C), lambda b, i: (b, 0, 0)),
                  const2((CONV_W, C)), const2((1, C)),
                  const2((RG_BLOCKS, RG_BLOCK, RG_BLOCK)), const2((1, C)),
                  const2((RG_BLOCKS, RG_BLOCK, RG_BLOCK)), const2((1, C)), const2((1, C))],
        out_specs=[pl.BlockSpec((1, tt, C), lambda b, i: (b, i, 0)),
                   pl.BlockSpec((1, CONV_W - 1, C), lambda b, i: (b, 0, 0)),
                   pl.BlockSpec((1, 1, C), lambda b, i: (b, 0, 0))],
        out_shape=[jax.ShapeDtypeStruct((B, T, C), BF16),
                   jax.ShapeDtypeStruct((B, CONV_W - 1, C), F32),
                   jax.ShapeDtypeStruct((B, 1, C), F32)],
        scratch_shapes=[pltpu.VMEM((tt + SUBLANES, C), F32),
                        pltpu.VMEM((SCAN_PAD + tt, C), F32),
                        pltpu.VMEM((SCAN_PAD + tt, C), F32),
                        pltpu.VMEM((1, C), F32)],
        name="rglru",
        compiler_params=_cparams(("parallel", "arbitrary")),
    )(u, u, buf8, h0.reshape(B, 1, C), cw, vec(cb), wa.astype(BF16), vec(ba), wx.astype(BF16), vec(bx), vec(lam))


TRI_BASE = 8


def _tri_inv(Lm, masks):
    eye, base_mask, level_masks = masks
    N = jnp.where(base_mask, -Lm, 0.0)
    P = eye + N
    N2 = _dot3(N, N)
    P = P + _dot3(P, N2)
    N4 = _dot3(N2, N2)
    P = P + _dot3(P, N4)
    for m in level_masks:
        off = jnp.where(m, Lm, 0.0)
        P = P - _dot3(_dot3(P, off), P)
    return P


def _tri_masks(C):
    ii = lax.broadcasted_iota(jnp.int32, (C, C), 0)
    jj = lax.broadcasted_iota(jnp.int32, (C, C), 1)
    same = lambda s: (ii // s) == (jj // s)
    eye = (ii == jj).astype(F32)
    levels = []
    s = TRI_BASE
    while s < C:
        levels.append(jnp.logical_and(same(2 * s), jnp.logical_not(same(s))))
        s *= 2
    return (eye, same(TRI_BASE), levels), ii, jj


def _gdn_kernel(qkv_ref, z_ref, ab_ref, buf_ref, cw_ref, alog_ref, dtb_ref, ng_ref,
                o_ref, nbuf_ref, S_ref, xbuf, act, gs, bs, S_sc, *, tt):
    i = pl.program_id(1)
    nt = pl.num_programs(1)
    C = GDN_CHUNK
    H = GDN_HEADS

    @pl.when(i == 0)
    def _():
        xbuf[0:SUBLANES, :] = buf_ref[0]
        S_sc[...] = jnp.zeros(S_sc.shape, F32)

    xbuf[SUBLANES:SUBLANES + tt, :] = qkv_ref[0]
    lo = SUBLANES - (CONV_W - 1)
    y = cw_ref[0:1, :] * xbuf[lo:lo + tt, :]
    for j in range(1, CONV_W):
        y = y + cw_ref[j:j + 1, :] * xbuf[lo + j:lo + j + tt, :]
    act[...] = _silu(y)

    @pl.when(i == nt - 1)
    def _():
        nbuf_ref[0] = xbuf[tt + lo:tt + SUBLANES, :]

    xbuf[0:SUBLANES, :] = xbuf[tt:tt + SUBLANES, :]

    ab = ab_ref[0]
    gs[...] = -jnp.exp(alog_ref[...]) * _softplus(ab + dtb_ref[...])
    bs[...] = _sigmoid(ab)

    masks, ii, jj = _tri_masks(C)
    incl = ii >= jj
    strict = ii > jj
    tril = incl.astype(BF16)
    ng = ng_ref[...]

    def chunk(c, carry):
        r0 = pl.multiple_of(c * C, C)
        rows = pl.ds(r0, C)
        gcum = _dot_exact_lhs(tril, gs[rows, :])
        gcum_t = gcum.T
        beta = bs[rows, :]
        for h in range(H):
            q = act[rows, h * GDN_DK:(h + 1) * GDN_DK]
            k = act[rows, GDN_KW + h * GDN_DK:GDN_KW + (h + 1) * GDN_DK]
            v = act[rows, 2 * GDN_KW + h * GDN_DV:2 * GDN_KW + (h + 1) * GDN_DV]
            q = q * lax.rsqrt(jnp.sum(q * q, axis=-1, keepdims=True) + EPS) * (GDN_DK ** -0.5)
            k = k * lax.rsqrt(jnp.sum(k * k, axis=-1, keepdims=True) + EPS)
            gc = gcum[:, h:h + 1]
            gr = gcum_t[h:h + 1, :]
            e = jnp.exp(jnp.where(incl, gc - gr, -jnp.inf))
            bt = beta[:, H + h:H + h + 1]
            kb = k.astype(BF16)
            qb = q.astype(BF16)
            kk = _dot_nt(kb, kb)
            qk = _dot_nt(qb, kb) * e
            Lm = jnp.where(strict, bt * kk * e, 0.0)
            t_inv = _tri_inv(Lm, masks)
            S = S_sc[h]
            Sb = S.astype(BF16)
            eg = jnp.exp(gc)
            rhs = bt * (v - eg * jnp.dot(kb, Sb, preferred_element_type=F32))
            U = _dot3(t_inv, rhs)
            Ub = U.astype(BF16)
            o = eg * jnp.dot(qb, Sb, preferred_element_type=F32) + _dot(qk, Ub)
            g_last = gcum[C - 1:C, h:h + 1]
            k_dec = k * jnp.exp(g_last - gc)
            S_sc[h] = jnp.exp(g_last) * S + _dot_tn(k_dec, Ub)
            zz = z_ref[0, rows, h * GDN_DV:(h + 1) * GDN_DV]
            o_ref[0, rows, h * GDN_DV:(h + 1) * GDN_DV] = (_rms(o, ng) * _silu(zz)).astype(BF16)
        return carry

    lax.fori_loop(0, tt // C, chunk, 0)

    @pl.when(i == nt - 1)
    def _():
        S_ref[0] = S_sc[...]


def _gdn_prompt(u, cw, alog, dtb, ng, tt):
    B, T, _ = u.shape
    Cc = GDN_CONV_C
    H = GDN_HEADS
    buf8 = jnp.zeros((B, SUBLANES, Cc), F32)
    lane_vec = lambda v: jnp.pad(v, (0, LANES - v.shape[0])).reshape(1, LANES)
    const2 = lambda shape: pl.BlockSpec(shape, lambda b, i: (0,) * len(shape))
    return pl.pallas_call(
        functools.partial(_gdn_kernel, tt=tt),
        grid=(B, T // tt),
        in_specs=[pl.BlockSpec((1, tt, Cc), lambda b, i: (b, i, U_QKV // Cc)),
                  pl.BlockSpec((1, tt, GDN_VW), lambda b, i: (b, i, U_Z // GDN_VW)),
                  pl.BlockSpec((1, tt, W_AB), lambda b, i: (b, i, U_AB // W_AB)),
                  pl.BlockSpec((1, SUBLANES, Cc), lambda b, i: (b, 0, 0)),
                  const2((CONV_W, Cc)), const2((1, LANES)), const2((1, LANES)), const2((1, GDN_DV))],
        out_specs=[pl.BlockSpec((1, tt, GDN_VW), lambda b, i: (b, i, 0)),
                   pl.BlockSpec((1, CONV_W - 1, Cc), lambda b, i: (b, 0, 0)),
                   pl.BlockSpec((1, H, GDN_DK, GDN_DV), lambda b, i: (b, 0, 0, 0))],
        out_shape=[jax.ShapeDtypeStruct((B, T, GDN_VW), BF16),
                   jax.ShapeDtypeStruct((B, CONV_W - 1, Cc), F32),
                   jax.ShapeDtypeStruct((B, H, GDN_DK, GDN_DV), F32)],
        scratch_shapes=[pltpu.VMEM((tt + SUBLANES, Cc), F32),
                        pltpu.VMEM((tt, Cc), F32),
                        pltpu.VMEM((tt, LANES), F32),
                        pltpu.VMEM((tt, LANES), F32),
                        pltpu.VMEM((H, GDN_DK, GDN_DV), F32)],
        name="gdn_chunked",
        compiler_params=_cparams(("parallel", "arbitrary")),
    )(u, u, u, buf8, cw, lane_vec(alog), lane_vec(dtb), ng.reshape(1, GDN_DV))


def _gdn_step_kernel(qkv_ref, z_ref, ab_ref, buf_ref, S0_ref, cw_ref, alog_ref, dtb_ref, ng_ref,
                     o_ref, nbuf_ref, S_ref, xbuf):
    H = GDN_HEADS
    xbuf[0:SUBLANES, :] = buf_ref[0]
    xbuf[SUBLANES:SUBLANES + 1, :] = qkv_ref[0]
    lo = SUBLANES - (CONV_W - 1)
    y = cw_ref[0:1, :] * xbuf[lo:lo + 1, :]
    for j in range(1, CONV_W):
        y = y + cw_ref[j:j + 1, :] * xbuf[lo + j:lo + j + 1, :]
    y = _silu(y)
    nbuf_ref[0] = xbuf[lo + 1:SUBLANES + 1, :]

    ab = ab_ref[0]
    g_all = -jnp.exp(alog_ref[...]) * _softplus(ab + dtb_ref[...])
    beta_all = _sigmoid(ab)
    ii = lax.broadcasted_iota(jnp.int32, (GDN_DK, GDN_DK), 0)
    jj = lax.broadcasted_iota(jnp.int32, (GDN_DK, GDN_DK), 1)
    eye = ii == jj
    ng = ng_ref[...]
    for h in range(H):
        q = y[:, h * GDN_DK:(h + 1) * GDN_DK]
        k = y[:, GDN_KW + h * GDN_DK:GDN_KW + (h + 1) * GDN_DK]
        v = y[:, 2 * GDN_KW + h * GDN_DV:2 * GDN_KW + (h + 1) * GDN_DV]
        q = q * lax.rsqrt(jnp.sum(q * q, axis=-1, keepdims=True) + EPS) * (GDN_DK ** -0.5)
        k = k * lax.rsqrt(jnp.sum(k * k, axis=-1, keepdims=True) + EPS)
        eg = jnp.exp(g_all[:, h:h + 1])
        bt = beta_all[:, H + h:H + h + 1]
        S = S0_ref[0, h]
        Sb = S.astype(BF16)
        kb = k.astype(BF16)
        qb = q.astype(BF16)
        u = bt * (v - eg * jnp.dot(kb, Sb, preferred_element_type=F32))
        ub = u.astype(BF16)
        qk = jnp.sum(qb.astype(F32) * kb.astype(F32), axis=-1, keepdims=True)
        o = eg * jnp.dot(qb, Sb, preferred_element_type=F32) + qk.astype(BF16).astype(F32) * ub.astype(F32)
        kdiag = jnp.where(eye, jnp.broadcast_to(kb.astype(F32), (GDN_DK, GDN_DK)), 0.0).astype(BF16)
        urows = jnp.broadcast_to(ub, (GDN_DK, GDN_DV))
        S_ref[0, h] = eg * S + jnp.dot(kdiag, urows, preferred_element_type=F32)
        zz = z_ref[0, :, h * GDN_DV:(h + 1) * GDN_DV]
        o_ref[0, :, h * GDN_DV:(h + 1) * GDN_DV] = (_rms(o, ng) * _silu(zz)).astype(BF16)


def _gdn_step(u, buf, S0, cw, alog, dtb, ng):
    B = u.shape[0]
    Cc = GDN_CONV_C
    H = GDN_HEADS
    buf8 = jnp.pad(buf, ((0, 0), (SUBLANES - (CONV_W - 1), 0), (0, 0)))
    lane_vec = lambda v: jnp.pad(v, (0, LANES - v.shape[0])).reshape(1, LANES)
    const1 = lambda shape: pl.BlockSpec(shape, lambda b: (0,) * len(shape))
    return pl.pallas_call(
        _gdn_step_kernel,
        grid=(B,),
        in_specs=[pl.BlockSpec((1, 1, Cc), lambda b: (b, 0, U_QKV // Cc)),
                  pl.BlockSpec((1, 1, GDN_VW), lambda b: (b, 0, U_Z // GDN_VW)),
                  pl.BlockSpec((1, 1, W_AB), lambda b: (b, 0, U_AB // W_AB)),
                  pl.BlockSpec((1, SUBLANES, Cc), lambda b: (b, 0, 0)),
                  pl.BlockSpec((1, H, GDN_DK, GDN_DV), lambda b: (b, 0, 0, 0)),
                  const1((CONV_W, Cc)), const1((1, LANES)), const1((1, LANES)), const1((1, GDN_DV))],
        out_specs=[pl.BlockSpec((1, 1, GDN_VW), lambda b: (b, 0, 0)),
                   pl.BlockSpec((1, CONV_W - 1, Cc), lambda b: (b, 0, 0)),
                   pl.BlockSpec((1, H, GDN_DK, GDN_DV), lambda b: (b, 0, 0, 0))],
        out_shape=[jax.ShapeDtypeStruct((B, 1, GDN_VW), BF16),
                   jax.ShapeDtypeStruct((B, CONV_W - 1, Cc), F32),
                   jax.ShapeDtypeStruct((B, H, GDN_DK, GDN_DV), F32)],
        scratch_shapes=[pltpu.VMEM((2 * SUBLANES, Cc), F32)],
        name="gdn_step",
        compiler_params=_cparams(("parallel",)),
    )(u, u, u, buf8, S0, cw, lane_vec(alog), lane_vec(dtb), ng.reshape(1, GDN_DV))


def _mla_prep_kernel(mq_ref, mkv_ref, cos_ref, sin_ref, gq_ref, gkv_ref, wq_ref, wkv_ref,
                     q_ref, k_ref, v_ref, ckv_ref, kr_ref):
    H = MLA_HEADS
    W = LANES * H
    cos = cos_ref[0]
    sin = sin_ref[0]
    cq = _rms(mq_ref[0], gq_ref[...])
    qa = _dot(cq, wq_ref[...])
    mkv = mkv_ref[0]
    ckv = _rms(mkv[:, :KV_LORA], gkv_ref[...])
    ckv_ref[0] = ckv
    kro = mkv[:, KV_LORA:KV_LORA + LANES] * cos + mkv[:, KV_LORA + LANES:KV_LORA + 2 * LANES] * sin
    kr_ref[0] = kro[:, ROPE_LANE0:ROPE_LANE0 + QK_ROPE]
    kv = _dot(ckv, wkv_ref[...])
    for h in range(H):
        sl = slice(h * LANES, (h + 1) * LANES)
        qh = qa[:, sl] * cos + qa[:, W + h * LANES:W + (h + 1) * LANES] * sin
        q_ref[0, :, sl] = (qh * MLA_SCALE).astype(BF16)
        k_ref[0, :, sl] = (kv[:, sl] + kro).astype(BF16)
    v_ref[0] = kv[:, W:].astype(BF16)


def _mla_prep(u, cos, sin, gq, gkv, wq_p, wkv_p, bm):
    B, T, _ = u.shape
    W = LANES * MLA_HEADS
    const2 = lambda shape: pl.BlockSpec(shape, lambda b, i: (0,) * len(shape))
    return pl.pallas_call(
        _mla_prep_kernel,
        grid=(B, T // bm),
        in_specs=[pl.BlockSpec((1, bm, Q_LORA), lambda b, i: (b, i, U_MQ // Q_LORA)),
                  pl.BlockSpec((1, bm, W_MKV), lambda b, i: (b, i, U_MKV // W_MKV)),
                  pl.BlockSpec((1, bm, LANES), lambda b, i: (0, i, 0)),
                  pl.BlockSpec((1, bm, LANES), lambda b, i: (0, i, 0)),
                  const2((1, Q_LORA)), const2((1, KV_LORA)),
                  const2((Q_LORA, 2 * W)), const2((KV_LORA, 2 * W))],
        out_specs=[pl.BlockSpec((1, bm, W), lambda b, i: (b, i, 0)),
                   pl.BlockSpec((1, bm, W), lambda b, i: (b, i, 0)),
                   pl.BlockSpec((1, bm, W), lambda b, i: (b, i, 0)),
                   pl.BlockSpec((1, bm, KV_LORA), lambda b, i: (b, i, 0)),
                   pl.BlockSpec((1, bm, QK_ROPE), lambda b, i: (b, i, 0))],
        out_shape=[jax.ShapeDtypeStruct((B, T, W), BF16),
                   jax.ShapeDtypeStruct((B, T, W), BF16),
                   jax.ShapeDtypeStruct((B, T, W), BF16),
                   jax.ShapeDtypeStruct((B, T, KV_LORA), F32),
                   jax.ShapeDtypeStruct((B, T, QK_ROPE), F32)],
        name="mla_prep",
        compiler_params=_cparams(("parallel", "parallel")),
    )(u, u, cos, sin, gq.reshape(1, Q_LORA), gkv.reshape(1, KV_LORA), wq_p, wkv_p)


NEG_BIG = -1e30


def _flash_kernel(q_ref, k_ref, v_ref, o_ref, *, tq):
    qi = pl.program_id(2)
    q = q_ref[0]
    row = qi * tq + lax.broadcasted_iota(jnp.int32, (tq, tq), 0)
    col = lax.broadcasted_iota(jnp.int32, (tq, tq), 1)

    def body(j, carry):
        m, l, acc = carry
        ks = pl.ds(pl.multiple_of(j * tq, tq), tq)
        s = lax.dot_general(q, k_ref[0, ks, :], (((1,), (1,)), ((), ())), preferred_element_type=F32)
        s = jnp.where(j * tq + col <= row, s, NEG_BIG)
        m_new = jnp.maximum(m, jnp.max(s, axis=-1, keepdims=True))
        alpha = jnp.exp(m - m_new)
        p = jnp.exp(s - m_new)
        l = alpha * l + jnp.sum(p, axis=-1, keepdims=True)
        acc = alpha * acc + jnp.dot(p.astype(BF16), v_ref[0, ks, :], preferred_element_type=F32)
        return m_new, l, acc

    init = (jnp.full((tq, 1), NEG_BIG, F32), jnp.zeros((tq, 1), F32), jnp.zeros((tq, LANES), F32))
    m, l, acc = lax.fori_loop(0, qi + 1, body, init)
    o_ref[0] = (acc / l).astype(BF16)


def _flash(q, k, v, tq):
    B, T, W = q.shape
    H = W // LANES
    return pl.pallas_call(
        functools.partial(_flash_kernel, tq=tq),
        grid=(B, H, T // tq),
        in_specs=[pl.BlockSpec((1, tq, LANES), lambda b, h, i: (b, i, h)),
                  pl.BlockSpec((1, T, LANES), lambda b, h, i: (b, 0, h)),
                  pl.BlockSpec((1, T, LANES), lambda b, h, i: (b, 0, h))],
        out_specs=pl.BlockSpec((1, tq, LANES), lambda b, h, i: (b, i, h)),
        out_shape=jax.ShapeDtypeStruct((B, T, W), BF16),
        name="mla_flash",
        compiler_params=_cparams(("parallel", "parallel", "arbitrary")),
    )(q, k, v)


PAGES_PER_STEP = 8


def _sattn_kernel(pt_ref, q_ref, cn_ref, krn_ref, wuk_ref, wuv_ref, *refs):
    n = PAGES_PER_STEP
    ck_refs = refs[:n]
    kr_refs = refs[n:2 * n]
    o_ref = refs[2 * n]
    qlat_sc, m_sc, l_sc, acc_sc = refs[2 * n + 1:]
    H = MLA_HEADS
    p_idx = pl.program_id(1)
    n_steps = pl.num_programs(1)
    q = q_ref[0]
    q_rope = q[:, ROPE_LANE0:ROPE_LANE0 + QK_ROPE]

    @pl.when(p_idx == 0)
    def _():
        rr = lax.broadcasted_iota(jnp.int32, (H, H * LANES), 0)
        cc = lax.broadcasted_iota(jnp.int32, (H, H * LANES), 1)
        q_bd = jnp.where(cc // LANES == rr, jnp.concatenate([q] * H, axis=1), jnp.zeros((), BF16))
        qlat_sc[...] = jnp.dot(q_bd, wuk_ref[...], preferred_element_type=F32).astype(BF16)
        m_sc[...] = jnp.full(m_sc.shape, NEG_BIG, F32)
        l_sc[...] = jnp.zeros(l_sc.shape, F32)
        acc_sc[...] = jnp.zeros(acc_sc.shape, F32)

    qlat = qlat_sc[...]
    ck = jnp.concatenate([r[0, 0] for r in ck_refs], axis=0).astype(BF16)
    kr = jnp.concatenate([r[0, 0] for r in kr_refs], axis=0).astype(BF16)
    s = _dot_nt(qlat, ck) + _dot_nt(q_rope, kr)
    m_new = jnp.maximum(m_sc[...], jnp.max(s, axis=-1, keepdims=True))
    alpha = jnp.exp(m_sc[...] - m_new)
    p = jnp.exp(s - m_new)
    l_sc[...] = alpha * l_sc[...] + jnp.sum(p, axis=-1, keepdims=True)
    acc_sc[...] = alpha * acc_sc[...] + jnp.dot(p.astype(BF16), ck, preferred_element_type=F32)
    m_sc[...] = m_new

    @pl.when(p_idx == n_steps - 1)
    def _():
        cn = cn_ref[0].astype(BF16).astype(F32)
        krn = krn_ref[0].astype(BF16).astype(F32)
        s_n = (jnp.sum(qlat.astype(F32) * cn, axis=-1, keepdims=True)
               + jnp.sum(q_rope.astype(F32) * krn, axis=-1, keepdims=True))
        m_f = jnp.maximum(m_sc[...], s_n)
        a_f = jnp.exp(m_sc[...] - m_f)
        p_n = jnp.exp(s_n - m_f)
        l_f = a_f * l_sc[...] + p_n
        o_lat = (a_f * acc_sc[...] + p_n.astype(BF16).astype(F32) * cn) / l_f
        o_all = jnp.dot(o_lat.astype(BF16), wuv_ref[...], preferred_element_type=F32)
        rr = lax.broadcasted_iota(jnp.int32, o_all.shape, 0)
        cc = lax.broadcasted_iota(jnp.int32, o_all.shape, 1)
        o_ref[0] = jnp.sum(jnp.where(cc // LANES == rr, o_all, 0.0), axis=0, keepdims=True).astype(BF16)


def _sample_attention(page_table, q, ckv_new, kr_new, wuk_s, wuv_c, cache_ckv, cache_krope, layer):
    Bd = q.shape[0]
    H = MLA_HEADS
    n = PAGES_PER_STEP
    n_pages = page_table.shape[1]
    q3 = q.reshape(Bd, H, LANES)

    def page_spec(width, k):
        return pl.BlockSpec((1, 1, PAGE_SIZE, width), lambda b, p, pt: (layer, pt[b, p * n + k], 0, 0))

    const = lambda shape: pl.BlockSpec(shape, lambda b, p, pt: (0,) * len(shape))
    grid_spec = pltpu.PrefetchScalarGridSpec(
        num_scalar_prefetch=1,
        grid=(Bd, n_pages // n),
        in_specs=[pl.BlockSpec((1, H, LANES), lambda b, p, pt: (b, 0, 0)),
                  pl.BlockSpec((1, 1, KV_LORA), lambda b, p, pt: (b, 0, 0)),
                  pl.BlockSpec((1, 1, QK_ROPE), lambda b, p, pt: (b, 0, 0)),
                  const((H * LANES, KV_LORA)), const((KV_LORA, H * V_HEAD))]
                 + [page_spec(KV_LORA, k) for k in range(n)]
                 + [page_spec(QK_ROPE, k) for k in range(n)],
        out_specs=pl.BlockSpec((1, 1, H * V_HEAD), lambda b, p, pt: (b, 0, 0)),
        scratch_shapes=[pltpu.VMEM((H, KV_LORA), BF16),
                        pltpu.VMEM((H, 1), F32), pltpu.VMEM((H, 1), F32),
                        pltpu.VMEM((H, KV_LORA), F32)],
    )
    return pl.pallas_call(
        _sattn_kernel,
        grid_spec=grid_spec,
        out_shape=jax.ShapeDtypeStruct((Bd, 1, H * V_HEAD), BF16),
        name="mla_decode",
        compiler_params=_cparams(("parallel", "arbitrary")),
    )(page_table, q3, ckv_new, kr_new, wuk_s, wuv_c, *([cache_ckv] * n), *([cache_krope] * n))


def _merge_kernel(x_ref, org_ref, ogdn_ref, omla_ref, ga_ref, gb_ref, gc_ref, gt_ref,
                  wrg_ref, wgdn_ref, wmla_ref, wo_ref, o_ref):
    d = functools.partial(jnp.dot, preferred_element_type=F32)
    m = (_sigmoid(ga_ref[0]) * d(org_ref[0], wrg_ref[...])
         + _sigmoid(gb_ref[0]) * d(ogdn_ref[0], wgdn_ref[...])
         + _sigmoid(gc_ref[0]) * d(omla_ref[0], wmla_ref[...]))
    o_ref[0] = x_ref[0] + gt_ref[0] * d(m.astype(BF16), wo_ref[...])


def _merge(x, o_rg, o_gdn, o_mla, u, gt, wrg, wgdn, wmla, wo, bm):
    B, T, D = x.shape
    tok = lambda w, col: pl.BlockSpec((1, bm, w), lambda b, i: (b, i, col))
    const2 = lambda shape: pl.BlockSpec(shape, lambda b, i: (0,) * len(shape))
    return pl.pallas_call(
        _merge_kernel,
        grid=(B, T // bm),
        in_specs=[tok(D, 0), tok(D_RNN, 0), tok(GDN_VW, 0), tok(D, 0),
                  tok(D, U_GA // D), tok(D, U_GB // D), tok(D, U_GC // D),
                  _mod_spec(gt, bm),
                  const2((D_RNN, D)), const2((GDN_VW, D)), const2((D, D)), const2((D, D))],
        out_specs=tok(D, 0),
        out_shape=jax.ShapeDtypeStruct((B, T, D), F32),
        name="branch_merge",
        compiler_params=_cparams(("parallel", "parallel")),
    )(x, o_rg, o_gdn, o_mla, u, u, u, gt, wrg, wgdn, wmla, wo)


def _ffn_kernel(x_ref, sc_ref, sh_ref, gt_ref, g_ref, gf_ref, wg_ref, wu_ref, wd_ref, o_ref, hb_ref, acc_ref, *, final):
    j = pl.program_id(2)

    @pl.when(j == 0)
    def _():
        h = _rms(x_ref[0], g_ref[...]) * (1.0 + sc_ref[0]) + sh_ref[0]
        hb_ref[...] = h.astype(BF16)
        acc_ref[...] = jnp.zeros(acc_ref.shape, F32)

    hb = hb_ref[...]
    gate = jnp.dot(hb, wg_ref[...], preferred_element_type=F32)
    up = jnp.dot(hb, wu_ref[...], preferred_element_type=F32)
    acc_ref[...] += jnp.dot((_silu(gate) * up).astype(BF16), wd_ref[...], preferred_element_type=F32)

    @pl.when(j == pl.num_programs(2) - 1)
    def _():
        y = x_ref[0] + gt_ref[0] * acc_ref[...]
        o_ref[0] = _rms(y, gf_ref[...]) if final else y


def _ffn(x, sc, sh, gt, g, g_final, w_in_b, w_out_b, bm, bf, final):
    B, T, D = x.shape
    nf = D_FF // bf
    return pl.pallas_call(
        functools.partial(_ffn_kernel, final=final),
        grid=(B, T // bm, nf),
        in_specs=[pl.BlockSpec((1, bm, D), lambda b, i, j: (b, i, 0)),
                  _mod_spec(sc, bm), _mod_spec(sh, bm), _mod_spec(gt, bm),
                  pl.BlockSpec((1, D), lambda b, i, j: (0, 0)),
                  pl.BlockSpec((1, D), lambda b, i, j: (0, 0)),
                  pl.BlockSpec((D, bf), lambda b, i, j: (0, j)),
                  pl.BlockSpec((D, bf), lambda b, i, j: (0, nf + j)),
                  pl.BlockSpec((bf, D), lambda b, i, j: (j, 0))],
        out_specs=pl.BlockSpec((1, bm, D), lambda b, i, j: (b, i, 0)),
        out_shape=jax.ShapeDtypeStruct((B, T, D), F32),
        scratch_shapes=[pltpu.VMEM((bm, D), BF16), pltpu.VMEM((bm, D), F32)],
        name="swiglu",
        compiler_params=_cparams(("parallel", "parallel", "arbitrary")),
    )(x, sc, sh, gt, g.reshape(1, D), g_final.reshape(1, D), w_in_b, w_in_b, w_out_b)


def _rot_half(w):
    half = QK_ROPE // 2
    return jnp.concatenate([-w[..., half:], w[..., :half]], axis=-1)


def _pack_w_in(w):
    D = w.shape[0]
    offs = np.cumsum((0,) + IN_SIZES)
    rx, ry, qkv, z, a, b, mq, mkv, gate = [w[:, offs[i]:offs[i + 1]] for i in range(len(IN_SIZES))]
    zeros = lambda n: jnp.zeros((D, n), w.dtype)
    kr = mkv[:, KV_LORA:]
    tail = LANES - ROPE_LANE0 - QK_ROPE
    cols = [rx, ry, zeros(U_QKV - 2 * D_RNN), qkv, z, gate,
            a, b, zeros(W_AB - 2 * GDN_HEADS), mq,
            mkv[:, :KV_LORA], zeros(ROPE_LANE0), kr, zeros(tail), zeros(ROPE_LANE0), _rot_half(kr), zeros(tail)]
    out = jnp.concatenate(cols, axis=1).astype(BF16)
    assert out.shape[1] == N_U
    return out


def _pack_w_uq(w):
    H = MLA_HEADS
    w = w.reshape(Q_LORA, H, QK_NOPE + QK_ROPE)
    nope, rope = w[..., :QK_NOPE], w[..., QK_NOPE:]
    tail = jnp.zeros((Q_LORA, H, LANES - QK_NOPE - QK_ROPE), w.dtype)
    a = jnp.concatenate([nope, rope, tail], axis=-1).reshape(Q_LORA, H * LANES)
    b = jnp.concatenate([jnp.zeros_like(nope), _rot_half(rope), tail], axis=-1).reshape(Q_LORA, H * LANES)
    return jnp.concatenate([a, b], axis=1).astype(BF16)


def _pack_w_ukv(w):
    H = MLA_HEADS
    w3 = w.reshape(KV_LORA, H, QK_NOPE + V_HEAD)
    w_uk, w_uv = w3[..., :QK_NOPE], w3[..., QK_NOPE:]
    kpad = jnp.concatenate([w_uk, jnp.zeros((KV_LORA, H, LANES - QK_NOPE), w.dtype)], axis=-1)
    wkv_p = jnp.concatenate([kpad.reshape(KV_LORA, H * LANES), w_uv.reshape(KV_LORA, H * V_HEAD)], axis=1)
    wuk_s = jnp.transpose(kpad, (1, 2, 0)).reshape(H * LANES, KV_LORA)
    wuv_c = w_uv.reshape(KV_LORA, H * V_HEAD)
    return wkv_p.astype(BF16), wuk_s.astype(BF16), wuv_c.astype(BF16)


def _rope_tables(T, pos0):
    inv = ROPE_BASE ** (-jnp.arange(0, QK_ROPE, 2, dtype=F32) / QK_ROPE)
    ang = (jnp.arange(T, dtype=F32) + pos0)[:, None] * inv[None, :]
    cos, sin = jnp.cos(ang), jnp.sin(ang)
    tail = jnp.zeros((T, LANES - ROPE_LANE0 - QK_ROPE), F32)
    cos_t = jnp.concatenate([jnp.ones((T, ROPE_LANE0), F32), cos, cos, tail], axis=1)
    sin_t = jnp.concatenate([jnp.zeros((T, ROPE_LANE0), F32), sin, sin, tail], axis=1)
    return cos_t[None], sin_t[None]


def _split_mod(mod, per_token):
    R = mod.shape[0]
    parts = jnp.split(mod, 6, axis=-1)
    return [p.reshape(1, R, D_MODEL) if per_token else p.reshape(R, 1, D_MODEL) for p in parts]


def kernel(x_prompt, x_sample, cache_ckv, cache_krope, state_rg_conv, state_rg_h, state_gdn_conv, state_gdn_S,
           page_table, c_prompt, c_sample, w_ada, b_ada, g_norm1, g_norm2, w_in, rg_conv_w, rg_conv_b, rg_wa,
           rg_ba, rg_wx, rg_bx, rg_lambda, gdn_conv_w, gdn_A_log, gdn_dt_bias, gdn_norm_g, mla_q_norm_g, w_uq,
           mla_kv_norm_g, w_ukv, w_rg_proj, w_gdn_proj, w_mla_proj, w_o, w_ffn_in, w_ffn_out, g_final):
    L = w_in.shape[0]
    Bp, T, D = x_prompt.shape
    Bd = x_sample.shape[0]
    past_len = page_table.shape[1] * PAGE_SIZE

    mod = _modulation(jnp.concatenate([c_prompt, c_sample], axis=0), w_ada, b_ada)
    cos_p, sin_p = _rope_tables(T, 0.0)
    cos_s, sin_s = _rope_tables(1, float(past_len))
    cos_s = jnp.broadcast_to(cos_s, (1, Bd, LANES))
    sin_s = jnp.broadcast_to(sin_s, (1, Bd, LANES))

    xp = x_prompt
    xs = x_sample.reshape(1, Bd, D)
    outs_p, outs_s = [], []
    for l in range(L):
        w_in_p = _pack_w_in(w_in[l])
        wq_p = _pack_w_uq(w_uq[l])
        wkv_p, wuk_s, wuv_c = _pack_w_ukv(w_ukv[l])
        wrg, wgdn, wmla, wo = (w_rg_proj[l].astype(BF16), w_gdn_proj[l].astype(BF16),
                               w_mla_proj[l].astype(BF16), w_o[l].astype(BF16))
        wfi, wfo = w_ffn_in[l].astype(BF16), w_ffn_out[l].astype(BF16)
        final = l == L - 1
        rg_w = (rg_conv_w[l], rg_conv_b[l], rg_wa[l], rg_ba[l], rg_wx[l], rg_bx[l], rg_lambda[l])
        gdn_w = (gdn_conv_w[l], gdn_A_log[l], gdn_dt_bias[l], gdn_norm_g[l])

        sh1, sc1, gt1, sh2, sc2, gt2 = _split_mod(mod[l, :Bp], per_token=False)
        u = _in_proj(xp, sc1, sh1, g_norm1[l], w_in_p, bm=min(1024, T), bn=1024)
        o_rg, rg_buf, rg_h = _rglru(u, jnp.zeros((Bp, CONV_W - 1, D_RNN), F32), jnp.zeros((Bp, D_RNN), F32),
                                    *rg_w, tt=128)
        o_gdn, gdn_buf, gdn_S = _gdn_prompt(u, *gdn_w, tt=256)
        q, k, v, ckv, kr = _mla_prep(u, cos_p, sin_p, mla_q_norm_g[l], mla_kv_norm_g[l], wq_p, wkv_p, bm=512)
        o_mla = _flash(q, k, v, tq=256)
        x1 = _merge(xp, o_rg, o_gdn, o_mla, u, gt1, wrg, wgdn, wmla, wo, bm=256)
        xp = _ffn(x1, sc2, sh2, gt2, g_norm2[l], g_final, wfi, wfo, bm=512, bf=1408, final=final)
        outs_p.append((ckv, kr, rg_buf, rg_h.reshape(Bp, D_RNN), gdn_buf, gdn_S))

        sh1, sc1, gt1, sh2, sc2, gt2 = _split_mod(mod[l, Bp:], per_token=True)
        us = _in_proj(xs, sc1, sh1, g_norm1[l], w_in_p, bm=Bd, bn=1024)
        us_seq = us.reshape(Bd, 1, N_U)
        o_rg, rg_buf, rg_h = _rglru(us_seq, state_rg_conv[l], state_rg_h[l], *rg_w, tt=1)
        o_gdn, gdn_buf, gdn_S = _gdn_step(us_seq, state_gdn_conv[l], state_gdn_S[l], *gdn_w)
        q, k, v, ckv, kr = _mla_prep(us, cos_s, sin_s, mla_q_norm_g[l], mla_kv_norm_g[l], wq_p, wkv_p, bm=Bd)
        o_mla = _sample_attention(page_table, q.reshape(Bd, MLA_HEADS * LANES), ckv.reshape(Bd, 1, KV_LORA),
                                  kr.reshape(Bd, 1, QK_ROPE), wuk_s, wuv_c, cache_ckv, cache_krope, l)
        x1 = _merge(xs, o_rg.reshape(1, Bd, D_RNN), o_gdn.reshape(1, Bd, GDN_VW), o_mla.reshape(1, Bd, D),
                    us, gt1, wrg, wgdn, wmla, wo, bm=Bd)
        xs = _ffn(x1, sc2, sh2, gt2, g_norm2[l], g_final, wfi, wfo, bm=Bd, bf=1408, final=final)
        outs_s.append((ckv.reshape(Bd, 1, KV_LORA), kr.reshape(Bd, 1, QK_ROPE), rg_buf, rg_h.reshape(Bd, D_RNN),
                       gdn_buf, gdn_S))

    stack = lambda outs, i: jnp.stack([o[i] for o in outs])
    return (xp, xs.reshape(Bd, 1, D),
            stack(outs_p, 0), stack(outs_p, 1), stack(outs_p, 2), stack(outs_p, 3), stack(outs_p, 4), stack(outs_p, 5),
            stack(outs_s, 0), stack(outs_s, 1), stack(outs_s, 2), stack(outs_s, 3), stack(outs_s, 4), stack(outs_s, 5))
```

```python
import functools
import math

import numpy as np
import jax
import jax.numpy as jnp
from jax import lax
from jax.experimental import pallas as pl
from jax.experimental.pallas import tpu as pltpu

F32 = jnp.float32
BF16 = jnp.bfloat16

D_MODEL = 1024
CONV_W = 4
D_RNN = 1280
RG_BLOCK = 128
RG_BLOCKS = D_RNN // RG_BLOCK
RG_C = 8.0
GDN_HEADS = 8
GDN_DK = 128
GDN_DV = 128
GDN_KW = GDN_HEADS * GDN_DK
GDN_VW = GDN_HEADS * GDN_DV
GDN_CONV_C = 2 * GDN_KW + GDN_VW
GDN_CHUNK = 64
MLA_HEADS = 8
Q_LORA = 384
KV_LORA = 256
QK_NOPE = 64
QK_ROPE = 32
V_HEAD = 128
MLA_SCALE = (QK_NOPE + QK_ROPE) ** -0.5
ROPE_BASE = 10000.0
D_FF = 2816
N_BRANCH = 3
IN_SIZES = (D_RNN, D_RNN, GDN_CONV_C, GDN_VW, GDN_HEADS, GDN_HEADS, Q_LORA, KV_LORA + QK_ROPE, N_BRANCH * D_MODEL)
EPS = 1e-6
PAGE_SIZE = 128

LANES = 128
SUBLANES = 8
VMEM_LIMIT = 56 * 1024 * 1024

U_RX = 0
U_RY = D_RNN
U_QKV = 3072
U_Z = 6144
U_GA = 7168
U_GB = 8192
U_GC = 9216
U_AB = 10240
U_MQ = 10368
U_MKV = 10752
N_U = 11264
W_AB = LANES
W_MKV = 512
ROPE_LANE0 = QK_NOPE


def _cparams(sem):
    return pltpu.CompilerParams(dimension_semantics=sem, vmem_limit_bytes=VMEM_LIMIT)


def _sigmoid(x):
    return 1.0 / (1.0 + jnp.exp(-x))


def _silu(x):
    return x * _sigmoid(x)


def _softplus(x):
    return jnp.maximum(x, 0.0) + jnp.log1p(jnp.exp(-jnp.abs(x)))


def _gelu_tanh(x):
    c = math.sqrt(2.0 / math.pi)
    return 0.5 * x * (1.0 + jnp.tanh(c * (x + 0.044715 * (x * x * x))))


def _rms(x, g):
    return x * lax.rsqrt(jnp.mean(x * x, axis=-1, keepdims=True) + EPS) * g


def _dot(a, b):
    return jnp.dot(a.astype(BF16), b.astype(BF16), preferred_element_type=F32)


def _dot_nt(a, b):
    return lax.dot_general(a.astype(BF16), b.astype(BF16), (((1,), (1,)), ((), ())), preferred_element_type=F32)


def _dot_tn(a, b):
    return lax.dot_general(a.astype(BF16), b.astype(BF16), (((0,), (0,)), ((), ())), preferred_element_type=F32)


def _split2(a):
    hi = a.astype(BF16)
    lo = (a - hi.astype(F32)).astype(BF16)
    return hi, lo


def _dot3(a, b):
    ah, al = _split2(a)
    bh, bl = _split2(b)
    d = functools.partial(jnp.dot, preferred_element_type=F32)
    return d(ah, bh) + (d(ah, bl) + d(al, bh))


def _dot_exact_lhs(a_bf16, b):
    b0 = b.astype(BF16)
    r1 = b - b0.astype(F32)
    b1 = r1.astype(BF16)
    b2 = (r1 - b1.astype(F32)).astype(BF16)
    d = functools.partial(jnp.dot, preferred_element_type=F32)
    return d(a_bf16, b0) + (d(a_bf16, b1) + d(a_bf16, b2))


def _mod_kernel(c_ref, w_ref, b_ref, o_ref):
    c = c_ref[...]
    o_ref[0] = _dot(_silu(c), w_ref[0]) + b_ref[0]


def _modulation(c_all, w_ada, b_ada):
    L, D, N = w_ada.shape
    R = c_all.shape[0]
    bn = 1536
    return pl.pallas_call(
        _mod_kernel,
        grid=(L, N // bn),
        in_specs=[pl.BlockSpec((R, D), lambda l, j: (0, 0)),
                  pl.BlockSpec((1, D, bn), lambda l, j: (l, 0, j)),
                  pl.BlockSpec((1, 1, bn), lambda l, j: (l, 0, j))],
        out_specs=pl.BlockSpec((1, R, bn), lambda l, j: (l, 0, j)),
        out_shape=jax.ShapeDtypeStruct((L, R, N), F32),
        name="adaln_mod",
        compiler_params=_cparams(("parallel", "parallel")),
    )(c_all, w_ada, b_ada.reshape(L, 1, N))


def _mod_spec(mod, bm):
    _, tm, d = mod.shape
    if tm == 1:
        return pl.BlockSpec((1, 1, d), lambda b, i, *_: (b, 0, 0))
    return pl.BlockSpec((1, bm, d), lambda b, i, *_: (b, i, 0))


def _in_kernel(x_ref, sc_ref, sh_ref, g_ref, w_ref, o_ref, hb_ref):
    @pl.when(pl.program_id(2) == 0)
    def _():
        h = _rms(x_ref[0], g_ref[...]) * (1.0 + sc_ref[0]) + sh_ref[0]
        hb_ref[...] = h.astype(BF16)

    o_ref[0] = jnp.dot(hb_ref[...], w_ref[...], preferred_element_type=F32)


def _in_proj(x, sc, sh, g, w_packed, bm, bn):
    B, T, D = x.shape
    N = w_packed.shape[1]
    return pl.pallas_call(
        _in_kernel,
        grid=(B, T // bm, N // bn),
        in_specs=[pl.BlockSpec((1, bm, D), lambda b, i, j: (b, i, 0)),
                  _mod_spec(sc, bm), _mod_spec(sh, bm),
                  pl.BlockSpec((1, D), lambda b, i, j: (0, 0)),
                  pl.BlockSpec((D, bn), lambda b, i, j: (0, j))],
        out_specs=pl.BlockSpec((1, bm, bn), lambda b, i, j: (b, i, j)),
        out_shape=jax.ShapeDtypeStruct((B, T, N), F32),
        scratch_shapes=[pltpu.VMEM((bm, D), BF16)],
        name="in_proj",
        compiler_params=_cparams(("parallel", "parallel", "arbitrary")),
    )(x, sc, sh, g.reshape(1, D), w_packed)


SCAN_PAD = 64


def _rg_kernel(ux_ref, uy_ref, buf_ref, h0_ref, cw_ref, cb_ref, wa_ref, ba_ref, wx_ref, bx_ref, lam_ref,
               o_ref, nbuf_ref, hl_ref, xbuf, abuf, bbuf, hc, *, tt):
    i = pl.program_id(1)
    nt = pl.num_programs(1)

    @pl.when(i == 0)
    def _():
        xbuf[0:SUBLANES, :] = buf_ref[0]
        hc[...] = h0_ref[0]
        abuf[0:SCAN_PAD, :] = jnp.ones((SCAN_PAD, D_RNN), F32)
        bbuf[0:SCAN_PAD, :] = jnp.zeros((SCAN_PAD, D_RNN), F32)

    xbuf[SUBLANES:SUBLANES + tt, :] = ux_ref[0]
    lo = SUBLANES - (CONV_W - 1)
    xc = cb_ref[...] + cw_ref[0:1, :] * xbuf[lo:lo + tt, :]
    for j in range(1, CONV_W):
        xc = xc + cw_ref[j:j + 1, :] * xbuf[lo + j:lo + j + tt, :]

    @pl.when(i == nt - 1)
    def _():
        nbuf_ref[0] = xbuf[tt + lo:tt + SUBLANES, :]

    if tt >= SUBLANES:
        xbuf[0:SUBLANES, :] = xbuf[tt:tt + SUBLANES, :]

    xb = xc.astype(BF16)
    ra = jnp.concatenate([jnp.dot(xb[:, n * RG_BLOCK:(n + 1) * RG_BLOCK], wa_ref[n], preferred_element_type=F32)
                          for n in range(RG_BLOCKS)], axis=1)
    ri = jnp.concatenate([jnp.dot(xb[:, n * RG_BLOCK:(n + 1) * RG_BLOCK], wx_ref[n], preferred_element_type=F32)
                          for n in range(RG_BLOCKS)], axis=1)
    r = _sigmoid(ra + ba_ref[...])
    ig = _sigmoid(ri + bx_ref[...])
    log_a = (-RG_C) * r * _softplus(-lam_ref[...])
    a = jnp.exp(log_a)
    b = jnp.sqrt(-jnp.tanh(log_a) * (a * a + 1.0)) * (ig * xc)

    abuf[SCAN_PAD:SCAN_PAD + tt, :] = a
    bbuf[SCAN_PAD:SCAN_PAD + tt, :] = b
    d = 1
    while d < tt:
        a_s = abuf[SCAN_PAD - d:SCAN_PAD - d + tt, :]
        b_s = bbuf[SCAN_PAD - d:SCAN_PAD - d + tt, :]
        a0 = abuf[SCAN_PAD:SCAN_PAD + tt, :]
        b0 = bbuf[SCAN_PAD:SCAN_PAD + tt, :]
        abuf[SCAN_PAD:SCAN_PAD + tt, :] = a0 * a_s
        bbuf[SCAN_PAD:SCAN_PAD + tt, :] = a0 * b_s + b0
        d *= 2
    h = bbuf[SCAN_PAD:SCAN_PAD + tt, :] + abuf[SCAN_PAD:SCAN_PAD + tt, :] * hc[...]
    hc[...] = h[tt - 1:tt, :]
    o_ref[0] = (h * _gelu_tanh(uy_ref[0])).astype(BF16)

    @pl.when(i == nt - 1)
    def _():
        hl_ref[0] = h[tt - 1:tt, :]


def _rglru(u, buf, h0, cw, cb, wa, ba, wx, bx, lam, tt):
    B, T, _ = u.shape
    C = D_RNN
    buf8 = jnp.pad(buf, ((0, 0), (SUBLANES - (CONV_W - 1), 0), (0, 0)))
    vec = lambda v: v.reshape(1, C)
    const2 = lambda shape: pl.BlockSpec(shape, lambda b, i: (0,) * len(shape))
    return pl.pallas_call(
        functools.partial(_rg_kernel, tt=tt),
        grid=(B, T // tt),
        in_specs=[pl.BlockSpec((1, tt, C), lambda b, i: (b, i, U_RX // C)),
                  pl.BlockSpec((1, tt, C), lambda b, i: (b, i, U_RY // C)),
                  pl.BlockSpec((1, SUBLANES, C), lambda b, i: (b, 0, 0)),
                  pl.BlockSpec((1, 1, C), lambda b, i: (b, 0, 0)),
                  const2((CONV_W, C)), const2((1, C)),
                  const2((RG_BLOCKS, RG_BLOCK, RG_BLOCK)), const2((1, C)),
                  const2((RG_BLOCKS, RG_BLOCK, RG_BLOCK)), const2((1, C)), const2((1, C))],
        out_specs=[pl.BlockSpec((1, tt, C), lambda b, i: (b, i, 0)),
                   pl.BlockSpec((1, CONV_W - 1, C), lambda b, i: (b, 0, 0)),
                   pl.BlockSpec((1, 1, C), lambda b, i: (b, 0, 0))],
        out_shape=[jax.ShapeDtypeStruct((B, T, C), BF16),
                   jax.ShapeDtypeStruct((B, CONV_W - 1, C), F32),
                   jax.ShapeDtypeStruct((B, 1, C), F32)],
        scratch_shapes=[pltpu.VMEM((tt + SUBLANES, C), F32),
                        pltpu.VMEM((SCAN_PAD + tt, C), F32),
                        pltpu.VMEM((SCAN_PAD + tt, C), F32),
                        pltpu.VMEM((1, C), F32)],
        name="rglru",
        compiler_params=_cparams(("parallel", "arbitrary")),
    )(u, u, buf8, h0.reshape(B, 1, C), cw, vec(cb), wa.astype(BF16), vec(ba), wx.astype(BF16), vec(bx), vec(lam))


TRI_BASE = 8


def _cat_dot3(x, y, bd_b, n):
    C = x.shape[0]
    xh, xl = _split2(x)
    yh, yl = _split2(y)
    d = functools.partial(jnp.dot, preferred_element_type=F32)
    r = d(jnp.concatenate([xh, xl], axis=0), jnp.concatenate([yh] * n, axis=0) * bd_b)
    return r[:C] + (r[C:] + d(xh, jnp.concatenate([yl] * n, axis=0) * bd_b))


def _tri_inv_cat(Lc, masks, bd_b, n):
    eye, base_mask, level_masks = masks
    mm = functools.partial(_cat_dot3, bd_b=bd_b, n=n)
    N = jnp.where(base_mask, -Lc, 0.0)
    P = eye + N
    N2 = mm(N, N)
    P = P + mm(P, N2)
    N4 = mm(N2, N2)
    P = P + mm(P, N4)
    for m in level_masks:
        off = jnp.where(m, Lc, 0.0)
        P = P - mm(mm(P, off), P)
    return P


def _cat_masks(C, n):
    rr = lax.broadcasted_iota(jnp.int32, (C, n * C), 0)
    jj = lax.broadcasted_iota(jnp.int32, (C, n * C), 1) % C
    same = lambda s: (rr // s) == (jj // s)
    eye = (rr == jj).astype(F32)
    levels = []
    s = TRI_BASE
    while s < C:
        levels.append(jnp.logical_and(same(2 * s), jnp.logical_not(same(s))))
        s *= 2
    return eye, same(TRI_BASE), levels


def _gdn_kernel(qkv_ref, z_ref, ab_ref, buf_ref, cw_ref, alog_ref, dtb_ref, ng_ref,
                o_ref, nbuf_ref, S_ref, xbuf, act, S_sc, wq_sc, uv_sc, qk_sc, kd_sc, u_sc, gl_sc, *, tt):
    i = pl.program_id(1)
    nt = pl.num_programs(1)
    C = GDN_CHUNK
    H = GDN_HEADS
    n = tt // C

    @pl.when(i == 0)
    def _():
        xbuf[0:SUBLANES, :] = buf_ref[0]
        S_sc[...] = jnp.zeros(S_sc.shape, F32)
        u_sc[...] = jnp.zeros(u_sc.shape, BF16)

    xbuf[SUBLANES:SUBLANES + tt, :] = qkv_ref[0]
    lo = SUBLANES - (CONV_W - 1)
    y = cw_ref[0:1, :] * xbuf[lo:lo + tt, :]
    for j in range(1, CONV_W):
        y = y + cw_ref[j:j + 1, :] * xbuf[lo + j:lo + j + tt, :]
    act[...] = _silu(y)

    @pl.when(i == nt - 1)
    def _():
        nbuf_ref[0] = xbuf[tt + lo:tt + SUBLANES, :]

    xbuf[0:SUBLANES, :] = xbuf[tt:tt + SUBLANES, :]

    ab = ab_ref[0]
    g = -jnp.exp(alog_ref[...]) * _softplus(ab + dtb_ref[...])
    beta = _sigmoid(ab)

    ri = lax.broadcasted_iota(jnp.int32, (tt, tt), 0)
    ci = lax.broadcasted_iota(jnp.int32, (tt, tt), 1)
    bd = (ri // C) == (ci // C)
    bd_incl = jnp.logical_and(bd, ri >= ci)
    bd_strict = jnp.logical_and(bd, ri > ci)
    bd_b = bd.astype(BF16)
    last_b = jnp.logical_and(bd, ci % C == C - 1).astype(BF16)
    gcum = _dot_exact_lhs(bd_incl.astype(BF16), g)
    glast = _dot_exact_lhs(last_b, gcum)
    gl_sc[...] = jnp.exp(glast)
    gcum_t = gcum.T
    masks = _cat_masks(C, n)
    ng = ng_ref[...]

    for h in range(H):
        q = act[:, h * GDN_DK:(h + 1) * GDN_DK]
        k = act[:, GDN_KW + h * GDN_DK:GDN_KW + (h + 1) * GDN_DK]
        v = act[:, 2 * GDN_KW + h * GDN_DV:2 * GDN_KW + (h + 1) * GDN_DV]
        q = q * lax.rsqrt(jnp.sum(q * q, axis=-1, keepdims=True) + EPS) * (GDN_DK ** -0.5)
        k = k * lax.rsqrt(jnp.sum(k * k, axis=-1, keepdims=True) + EPS)
        gc = gcum[:, h:h + 1]
        gr = gcum_t[h:h + 1, :]
        e = jnp.exp(jnp.where(bd_incl, gc - gr, -jnp.inf))
        bt = beta[:, H + h:H + h + 1]
        kb = k.astype(BF16)
        qb = q.astype(BF16)
        Lf = jnp.where(bd_strict, bt * _dot_nt(kb, kb) * e, 0.0)
        qk_sc[h] = (_dot_nt(qb, kb) * e).astype(BF16)
        Lc = Lf[0:C]
        for c in range(1, n):
            Lc = Lc + Lf[c * C:(c + 1) * C]
        t_cat = _tri_inv_cat(Lc, masks, bd_b, n)
        t_bd = jnp.where(bd, jnp.concatenate([t_cat] * n, axis=0), 0.0)
        eg = jnp.exp(gc)
        uw = _dot3(t_bd, jnp.concatenate([bt * v, (bt * eg) * k], axis=1))
        uv_sc[h] = uw[:, :GDN_DV]
        w = uw[:, GDN_DV:].astype(BF16)
        qe = (eg * q).astype(BF16)
        for c in range(n):
            wq_sc[h, c, 0:C, :] = w[c * C:(c + 1) * C]
            wq_sc[h, c, C:2 * C, :] = qe[c * C:(c + 1) * C]
        kd_sc[h] = (k * jnp.exp(glast[:, h:h + 1] - gc)).astype(BF16)

    def chunk(c, carry):
        r0 = pl.multiple_of(c * C, C)
        rows = pl.ds(r0, C)
        for h in range(H):
            S = S_sc[h]
            wq = jnp.dot(wq_sc[h, c], S.astype(BF16), preferred_element_type=F32)
            Ub = (uv_sc[h, rows, :] - wq[:C]).astype(BF16)
            u_sc[h, rows, :] = Ub
            o = wq[C:] + jnp.dot(qk_sc[h, rows, :], u_sc[h], preferred_element_type=F32)
            S_sc[h] = gl_sc[pl.ds(r0, 1), h:h + 1] * S + _dot_tn(kd_sc[h, rows, :], Ub)
            zz = z_ref[0, rows, h * GDN_DV:(h + 1) * GDN_DV]
            o_ref[0, rows, h * GDN_DV:(h + 1) * GDN_DV] = (_rms(o, ng) * _silu(zz)).astype(BF16)
        return carry

    lax.fori_loop(0, n, chunk, 0)

    @pl.when(i == nt - 1)
    def _():
        S_ref[0] = S_sc[...]


def _gdn_prompt(u, cw, alog, dtb, ng, tt):
    B, T, _ = u.shape
    Cc = GDN_CONV_C
    H = GDN_HEADS
    buf8 = jnp.zeros((B, SUBLANES, Cc), F32)
    lane_vec = lambda v: jnp.pad(v, (0, LANES - v.shape[0])).reshape(1, LANES)
    const2 = lambda shape: pl.BlockSpec(shape, lambda b, i: (0,) * len(shape))
    return pl.pallas_call(
        functools.partial(_gdn_kernel, tt=tt),
        grid=(B, T // tt),
        in_specs=[pl.BlockSpec((1, tt, Cc), lambda b, i: (b, i, U_QKV // Cc)),
                  pl.BlockSpec((1, tt, GDN_VW), lambda b, i: (b, i, U_Z // GDN_VW)),
                  pl.BlockSpec((1, tt, W_AB), lambda b, i: (b, i, U_AB // W_AB)),
                  pl.BlockSpec((1, SUBLANES, Cc), lambda b, i: (b, 0, 0)),
                  const2((CONV_W, Cc)), const2((1, LANES)), const2((1, LANES)), const2((1, GDN_DV))],
        out_specs=[pl.BlockSpec((1, tt, GDN_VW), lambda b, i: (b, i, 0)),
                   pl.BlockSpec((1, CONV_W - 1, Cc), lambda b, i: (b, 0, 0)),
                   pl.BlockSpec((1, H, GDN_DK, GDN_DV), lambda b, i: (b, 0, 0, 0))],
        out_shape=[jax.ShapeDtypeStruct((B, T, GDN_VW), BF16),
                   jax.ShapeDtypeStruct((B, CONV_W - 1, Cc), F32),
                   jax.ShapeDtypeStruct((B, H, GDN_DK, GDN_DV), F32)],
        scratch_shapes=[pltpu.VMEM((tt + SUBLANES, Cc), F32),
                        pltpu.VMEM((tt, Cc), F32),
                        pltpu.VMEM((H, GDN_DK, GDN_DV), F32),
                        pltpu.VMEM((H, tt // GDN_CHUNK, 2 * GDN_CHUNK, GDN_DK), BF16),
                        pltpu.VMEM((H, tt, GDN_DV), F32),
                        pltpu.VMEM((H, tt, tt), BF16),
                        pltpu.VMEM((H, tt, GDN_DK), BF16),
                        pltpu.VMEM((H, tt, GDN_DV), BF16),
                        pltpu.VMEM((tt, LANES), F32)],
        name="gdn_chunked",
        compiler_params=_cparams(("parallel", "arbitrary")),
    )(u, u, u, buf8, cw, lane_vec(alog), lane_vec(dtb), ng.reshape(1, GDN_DV))


def _gdn_step_kernel(qkv_ref, z_ref, ab_ref, buf_ref, S0_ref, cw_ref, alog_ref, dtb_ref, ng_ref,
                     o_ref, nbuf_ref, S_ref, xbuf):
    H = GDN_HEADS
    xbuf[0:SUBLANES, :] = buf_ref[0]
    xbuf[SUBLANES:SUBLANES + 1, :] = qkv_ref[0]
    lo = SUBLANES - (CONV_W - 1)
    y = cw_ref[0:1, :] * xbuf[lo:lo + 1, :]
    for j in range(1, CONV_W):
        y = y + cw_ref[j:j + 1, :] * xbuf[lo + j:lo + j + 1, :]
    y = _silu(y)
    nbuf_ref[0] = xbuf[lo + 1:SUBLANES + 1, :]

    ab = ab_ref[0]
    g_all = -jnp.exp(alog_ref[...]) * _softplus(ab + dtb_ref[...])
    beta_all = _sigmoid(ab)
    ii = lax.broadcasted_iota(jnp.int32, (GDN_DK, GDN_DK), 0)
    jj = lax.broadcasted_iota(jnp.int32, (GDN_DK, GDN_DK), 1)
    eye = ii == jj
    ng = ng_ref[...]
    for h in range(H):
        q = y[:, h * GDN_DK:(h + 1) * GDN_DK]
        k = y[:, GDN_KW + h * GDN_DK:GDN_KW + (h + 1) * GDN_DK]
        v = y[:, 2 * GDN_KW + h * GDN_DV:2 * GDN_KW + (h + 1) * GDN_DV]
        q = q * lax.rsqrt(jnp.sum(q * q, axis=-1, keepdims=True) + EPS) * (GDN_DK ** -0.5)
        k = k * lax.rsqrt(jnp.sum(k * k, axis=-1, keepdims=True) + EPS)
        eg = jnp.exp(g_all[:, h:h + 1])
        bt = beta_all[:, H + h:H + h + 1]
        S = S0_ref[0, h]
        Sb = S.astype(BF16)
        kb = k.astype(BF16)
        qb = q.astype(BF16)
        u = bt * (v - eg * jnp.dot(kb, Sb, preferred_element_type=F32))
        ub = u.astype(BF16)
        qk = jnp.sum(qb.astype(F32) * kb.astype(F32), axis=-1, keepdims=True)
        o = eg * jnp.dot(qb, Sb, preferred_element_type=F32) + qk.astype(BF16).astype(F32) * ub.astype(F32)
        kdiag = jnp.where(eye, jnp.broadcast_to(kb.astype(F32), (GDN_DK, GDN_DK)), 0.0).astype(BF16)
        urows = jnp.broadcast_to(ub, (GDN_DK, GDN_DV))
        S_ref[0, h] = eg * S + jnp.dot(kdiag, urows, preferred_element_type=F32)
        zz = z_ref[0, :, h * GDN_DV:(h + 1) * GDN_DV]
        o_ref[0, :, h * GDN_DV:(h + 1) * GDN_DV] = (_rms(o, ng) * _silu(zz)).astype(BF16)


def _gdn_step(u, buf, S0, cw, alog, dtb, ng):
    B = u.shape[0]
    Cc = GDN_CONV_C
    H = GDN_HEADS
    buf8 = jnp.pad(buf, ((0, 0), (SUBLANES - (CONV_W - 1), 0), (0, 0)))
    lane_vec = lambda v: jnp.pad(v, (0, LANES - v.shape[0])).reshape(1, LANES)
    const1 = lambda shape: pl.BlockSpec(shape, lambda b: (0,) * len(shape))
    return pl.pallas_call(
        _gdn_step_kernel,
        grid=(B,),
        in_specs=[pl.BlockSpec((1, 1, Cc), lambda b: (b, 0, U_QKV // Cc)),
                  pl.BlockSpec((1, 1, GDN_VW), lambda b: (b, 0, U_Z // GDN_VW)),
                  pl.BlockSpec((1, 1, W_AB), lambda b: (b, 0, U_AB // W_AB)),
                  pl.BlockSpec((1, SUBLANES, Cc), lambda b: (b, 0, 0)),
                  pl.BlockSpec((1, H, GDN_DK, GDN_DV), lambda b: (b, 0, 0, 0)),
                  const1((CONV_W, Cc)), const1((1, LANES)), const1((1, LANES)), const1((1, GDN_DV))],
        out_specs=[pl.BlockSpec((1, 1, GDN_VW), lambda b: (b, 0, 0)),
                   pl.BlockSpec((1, CONV_W - 1, Cc), lambda b: (b, 0, 0)),
                   pl.BlockSpec((1, H, GDN_DK, GDN_DV), lambda b: (b, 0, 0, 0))],
        out_shape=[jax.ShapeDtypeStruct((B, 1, GDN_VW), BF16),
                   jax.ShapeDtypeStruct((B, CONV_W - 1, Cc), F32),
                   jax.ShapeDtypeStruct((B, H, GDN_DK, GDN_DV), F32)],
        scratch_shapes=[pltpu.VMEM((2 * SUBLANES, Cc), F32)],
        name="gdn_step",
        compiler_params=_cparams(("parallel",)),
    )(u, u, u, buf8, S0, cw, lane_vec(alog), lane_vec(dtb), ng.reshape(1, GDN_DV))


def _mla_prep_kernel(mq_ref, mkv_ref, cos_ref, sin_ref, gq_ref, gkv_ref, wq_ref, wkv_ref,
                     q_ref, k_ref, v_ref, ckv_ref, kr_ref):
    H = MLA_HEADS
    W = LANES * H
    cos = cos_ref[0]
    sin = sin_ref[0]
    cq = _rms(mq_ref[0], gq_ref[...])
    qa = _dot(cq, wq_ref[...])
    mkv = mkv_ref[0]
    ckv = _rms(mkv[:, :KV_LORA], gkv_ref[...])
    ckv_ref[0] = ckv
    kro = mkv[:, KV_LORA:KV_LORA + LANES] * cos + mkv[:, KV_LORA + LANES:KV_LORA + 2 * LANES] * sin
    kr_ref[0] = kro[:, ROPE_LANE0:ROPE_LANE0 + QK_ROPE]
    kv = _dot(ckv, wkv_ref[...])
    for h in range(H):
        sl = slice(h * LANES, (h + 1) * LANES)
        qh = qa[:, sl] * cos + qa[:, W + h * LANES:W + (h + 1) * LANES] * sin
        q_ref[0, :, sl] = (qh * MLA_SCALE).astype(BF16)
        k_ref[0, :, sl] = (kv[:, sl] + kro).astype(BF16)
    v_ref[0] = kv[:, W:].astype(BF16)


def _mla_prep(u, cos, sin, gq, gkv, wq_p, wkv_p, bm):
    B, T, _ = u.shape
    W = LANES * MLA_HEADS
    const2 = lambda shape: pl.BlockSpec(shape, lambda b, i: (0,) * len(shape))
    return pl.pallas_call(
        _mla_prep_kernel,
        grid=(B, T // bm),
        in_specs=[pl.BlockSpec((1, bm, Q_LORA), lambda b, i: (b, i, U_MQ // Q_LORA)),
                  pl.BlockSpec((1, bm, W_MKV), lambda b, i: (b, i, U_MKV // W_MKV)),
                  pl.BlockSpec((1, bm, LANES), lambda b, i: (0, i, 0)),
                  pl.BlockSpec((1, bm, LANES), lambda b, i: (0, i, 0)),
                  const2((1, Q_LORA)), const2((1, KV_LORA)),
                  const2((Q_LORA, 2 * W)), const2((KV_LORA, 2 * W))],
        out_specs=[pl.BlockSpec((1, bm, W), lambda b, i: (b, i, 0)),
                   pl.BlockSpec((1, bm, W), lambda b, i: (b, i, 0)),
                   pl.BlockSpec((1, bm, W), lambda b, i: (b, i, 0)),
                   pl.BlockSpec((1, bm, KV_LORA), lambda b, i: (b, i, 0)),
                   pl.BlockSpec((1, bm, QK_ROPE), lambda b, i: (b, i, 0))],
        out_shape=[jax.ShapeDtypeStruct((B, T, W), BF16),
                   jax.ShapeDtypeStruct((B, T, W), BF16),
                   jax.ShapeDtypeStruct((B, T, W), BF16),
                   jax.ShapeDtypeStruct((B, T, KV_LORA), F32),
                   jax.ShapeDtypeStruct((B, T, QK_ROPE), F32)],
        name="mla_prep",
        compiler_params=_cparams(("parallel", "parallel")),
    )(u, u, cos, sin, gq.reshape(1, Q_LORA), gkv.reshape(1, KV_LORA), wq_p, wkv_p)


NEG_BIG = -1e30


FLASH_HEADS = 4


def _flash_kernel(q_ref, k_ref, v_ref, o_ref, *, tq):
    qi = pl.program_id(2)
    hp = FLASH_HEADS
    heads = [slice(h * LANES, (h + 1) * LANES) for h in range(hp)]
    qs = [q_ref[0, :, sl] for sl in heads]
    causal = (lax.broadcasted_iota(jnp.int32, (tq, tq), 1) <= lax.broadcasted_iota(jnp.int32, (tq, tq), 0))

    def step(j, carry, diagonal):
        ks = pl.ds(pl.multiple_of(j * tq, tq), tq)
        out = []
        for h, sl in enumerate(heads):
            m, l, acc = carry[h]
            s = lax.dot_general(qs[h], k_ref[0, ks, sl], (((1,), (1,)), ((), ())), preferred_element_type=F32)
            if diagonal:
                s = jnp.where(causal, s, NEG_BIG)
            m_new = jnp.maximum(m, jnp.max(s, axis=-1, keepdims=True))
            alpha = jnp.exp(m - m_new)
            p = jnp.exp(s - m_new)
            l = alpha * l + jnp.sum(p, axis=-1, keepdims=True)
            acc = alpha * acc + jnp.dot(p.astype(BF16), v_ref[0, ks, sl], preferred_element_type=F32)
            out.append((m_new, l, acc))
        return tuple(out)

    init = tuple((jnp.full((tq, 1), NEG_BIG, F32), jnp.zeros((tq, 1), F32), jnp.zeros((tq, LANES), F32))
                 for _ in heads)
    carry = lax.fori_loop(0, qi, lambda j, c: step(j, c, False), init)
    carry = step(qi, carry, True)
    for h, sl in enumerate(heads):
        _, l, acc = carry[h]
        o_ref[0, :, sl] = (acc / l).astype(BF16)


def _flash(q, k, v, tq):
    B, T, W = q.shape
    wb = FLASH_HEADS * LANES
    H = W // wb
    return pl.pallas_call(
        functools.partial(_flash_kernel, tq=tq),
        grid=(B, H, T // tq),
        in_specs=[pl.BlockSpec((1, tq, wb), lambda b, h, i: (b, i, h)),
                  pl.BlockSpec((1, T, wb), lambda b, h, i: (b, 0, h)),
                  pl.BlockSpec((1, T, wb), lambda b, h, i: (b, 0, h))],
        out_specs=pl.BlockSpec((1, tq, wb), lambda b, h, i: (b, i, h)),
        out_shape=jax.ShapeDtypeStruct((B, T, W), BF16),
        name="mla_flash",
        compiler_params=_cparams(("parallel", "parallel", "arbitrary")),
    )(q, k, v)


PAGES_PER_STEP = 16


def _sattn_kernel(pt_ref, q_ref, cn_ref, krn_ref, wuk_ref, wuv_ref, *refs):
    n = PAGES_PER_STEP
    ck_refs = refs[:n]
    kr_refs = refs[n:2 * n]
    o_ref = refs[2 * n]
    qlat_sc, m_sc, l_sc, acc_sc = refs[2 * n + 1:]
    H = MLA_HEADS
    p_idx = pl.program_id(1)
    n_steps = pl.num_programs(1)
    q = q_ref[0]
    q_rope = q[:, ROPE_LANE0:ROPE_LANE0 + QK_ROPE]

    @pl.when(p_idx == 0)
    def _():
        rr = lax.broadcasted_iota(jnp.int32, (H, H * LANES), 0)
        cc = lax.broadcasted_iota(jnp.int32, (H, H * LANES), 1)
        q_bd = jnp.where(cc // LANES == rr, jnp.concatenate([q] * H, axis=1), jnp.zeros((), BF16))
        qlat_sc[...] = jnp.dot(q_bd, wuk_ref[...], preferred_element_type=F32).astype(BF16)
        m_sc[...] = jnp.full(m_sc.shape, NEG_BIG, F32)
        l_sc[...] = jnp.zeros(l_sc.shape, F32)
        acc_sc[...] = jnp.zeros(acc_sc.shape, F32)

    qlat = qlat_sc[...]
    ck = jnp.concatenate([r[0, 0] for r in ck_refs], axis=0).astype(BF16)
    kr_t = jnp.concatenate([r[0, 0] for r in kr_refs], axis=1).astype(BF16)
    s = _dot_nt(qlat, ck) + jnp.dot(q_rope, kr_t, preferred_element_type=F32)
    m_new = jnp.maximum(m_sc[...], jnp.max(s, axis=-1, keepdims=True))
    alpha = jnp.exp(m_sc[...] - m_new)
    p = jnp.exp(s - m_new)
    l_sc[...] = alpha * l_sc[...] + jnp.sum(p, axis=-1, keepdims=True)
    acc_sc[...] = alpha * acc_sc[...] + jnp.dot(p.astype(BF16), ck, preferred_element_type=F32)
    m_sc[...] = m_new

    @pl.when(p_idx == n_steps - 1)
    def _():
        cn = cn_ref[0].astype(BF16).astype(F32)
        krn = krn_ref[0].astype(BF16).astype(F32)
        s_n = (jnp.sum(qlat.astype(F32) * cn, axis=-1, keepdims=True)
               + jnp.sum(q_rope.astype(F32) * krn, axis=-1, keepdims=True))
        m_f = jnp.maximum(m_sc[...], s_n)
        a_f = jnp.exp(m_sc[...] - m_f)
        p_n = jnp.exp(s_n - m_f)
        l_f = a_f * l_sc[...] + p_n
        o_lat = (a_f * acc_sc[...] + p_n.astype(BF16).astype(F32) * cn) / l_f
        o_all = jnp.dot(o_lat.astype(BF16), wuv_ref[...], preferred_element_type=F32)
        rr = lax.broadcasted_iota(jnp.int32, o_all.shape, 0)
        cc = lax.broadcasted_iota(jnp.int32, o_all.shape, 1)
        o_ref[0] = jnp.sum(jnp.where(cc // LANES == rr, o_all, 0.0), axis=0, keepdims=True).astype(BF16)


def _sample_attention(page_table, q, ckv_new, kr_new, wuk_s, wuv_c, cache_ckv, cache_krope_t, layer):
    Bd = q.shape[0]
    H = MLA_HEADS
    n = PAGES_PER_STEP
    n_pages = page_table.shape[1]
    q3 = q.reshape(Bd, H, LANES)

    def page_spec(shape, k):
        return pl.BlockSpec((1, 1) + shape, lambda b, p, pt: (layer, pt[b, p * n + k], 0, 0))

    const = lambda shape: pl.BlockSpec(shape, lambda b, p, pt: (0,) * len(shape))
    grid_spec = pltpu.PrefetchScalarGridSpec(
        num_scalar_prefetch=1,
        grid=(Bd, n_pages // n),
        in_specs=[pl.BlockSpec((1, H, LANES), lambda b, p, pt: (b, 0, 0)),
                  pl.BlockSpec((1, 1, KV_LORA), lambda b, p, pt: (b, 0, 0)),
                  pl.BlockSpec((1, 1, QK_ROPE), lambda b, p, pt: (b, 0, 0)),
                  const((H * LANES, KV_LORA)), const((KV_LORA, H * V_HEAD))]
                 + [page_spec((PAGE_SIZE, KV_LORA), k) for k in range(n)]
                 + [page_spec((QK_ROPE, PAGE_SIZE), k) for k in range(n)],
        out_specs=pl.BlockSpec((1, 1, H * V_HEAD), lambda b, p, pt: (b, 0, 0)),
        scratch_shapes=[pltpu.VMEM((H, KV_LORA), BF16),
                        pltpu.VMEM((H, 1), F32), pltpu.VMEM((H, 1), F32),
                        pltpu.VMEM((H, KV_LORA), F32)],
    )
    return pl.pallas_call(
        _sattn_kernel,
        grid_spec=grid_spec,
        out_shape=jax.ShapeDtypeStruct((Bd, 1, H * V_HEAD), BF16),
        name="mla_decode",
        compiler_params=_cparams(("parallel", "arbitrary")),
    )(page_table, q3, ckv_new, kr_new, wuk_s, wuv_c, *([cache_ckv] * n), *([cache_krope_t] * n))


def _merge_kernel(x_ref, org_ref, ogdn_ref, omla_ref, ga_ref, gb_ref, gc_ref, gt_ref,
                  wrg_ref, wgdn_ref, wmla_ref, wo_ref, o_ref):
    d = functools.partial(jnp.dot, preferred_element_type=F32)
    m = (_sigmoid(ga_ref[0]) * d(org_ref[0], wrg_ref[...])
         + _sigmoid(gb_ref[0]) * d(ogdn_ref[0], wgdn_ref[...])
         + _sigmoid(gc_ref[0]) * d(omla_ref[0], wmla_ref[...]))
    o_ref[0] = x_ref[0] + gt_ref[0] * d(m.astype(BF16), wo_ref[...])


def _merge(x, o_rg, o_gdn, o_mla, u, gt, wrg, wgdn, wmla, wo, bm):
    B, T, D = x.shape
    tok = lambda w, col: pl.BlockSpec((1, bm, w), lambda b, i: (b, i, col))
    const2 = lambda shape: pl.BlockSpec(shape, lambda b, i: (0,) * len(shape))
    return pl.pallas_call(
        _merge_kernel,
        grid=(B, T // bm),
        in_specs=[tok(D, 0), tok(D_RNN, 0), tok(GDN_VW, 0), tok(D, 0),
                  tok(D, U_GA // D), tok(D, U_GB // D), tok(D, U_GC // D),
                  _mod_spec(gt, bm),
                  const2((D_RNN, D)), const2((GDN_VW, D)), const2((D, D)), const2((D, D))],
        out_specs=tok(D, 0),
        out_shape=jax.ShapeDtypeStruct((B, T, D), F32),
        name="branch_merge",
        compiler_params=_cparams(("parallel", "parallel")),
    )(x, o_rg, o_gdn, o_mla, u, u, u, gt, wrg, wgdn, wmla, wo)


def _ffn_kernel(x_ref, sc_ref, sh_ref, gt_ref, g_ref, gf_ref, wg_ref, wu_ref, wd_ref, o_ref, hb_ref, acc_ref, *, final):
    j = pl.program_id(2)

    @pl.when(j == 0)
    def _():
        h = _rms(x_ref[0], g_ref[...]) * (1.0 + sc_ref[0]) + sh_ref[0]
        hb_ref[...] = h.astype(BF16)
        acc_ref[...] = jnp.zeros(acc_ref.shape, F32)

    hb = hb_ref[...]
    gate = jnp.dot(hb, wg_ref[...], preferred_element_type=F32)
    up = jnp.dot(hb, wu_ref[...], preferred_element_type=F32)
    acc_ref[...] += jnp.dot((_silu(gate) * up).astype(BF16), wd_ref[...], preferred_element_type=F32)

    @pl.when(j == pl.num_programs(2) - 1)
    def _():
        y = x_ref[0] + gt_ref[0] * acc_ref[...]
        o_ref[0] = _rms(y, gf_ref[...]) if final else y


def _ffn(x, sc, sh, gt, g, g_final, w_in_b, w_out_b, bm, bf, final):
    B, T, D = x.shape
    nf = D_FF // bf
    return pl.pallas_call(
        functools.partial(_ffn_kernel, final=final),
        grid=(B, T // bm, nf),
        in_specs=[pl.BlockSpec((1, bm, D), lambda b, i, j: (b, i, 0)),
                  _mod_spec(sc, bm), _mod_spec(sh, bm), _mod_spec(gt, bm),
                  pl.BlockSpec((1, D), lambda b, i, j: (0, 0)),
                  pl.BlockSpec((1, D), lambda b, i, j: (0, 0)),
                  pl.BlockSpec((D, bf), lambda b, i, j: (0, j)),
                  pl.BlockSpec((D, bf), lambda b, i, j: (0, nf + j)),
                  pl.BlockSpec((bf, D), lambda b, i, j: (j, 0))],
        out_specs=pl.BlockSpec((1, bm, D), lambda b, i, j: (b, i, 0)),
        out_shape=jax.ShapeDtypeStruct((B, T, D), F32),
        scratch_shapes=[pltpu.VMEM((bm, D), BF16), pltpu.VMEM((bm, D), F32)],
        name="swiglu",
        compiler_params=_cparams(("parallel", "parallel", "arbitrary")),
    )(x, sc, sh, gt, g.reshape(1, D), g_final.reshape(1, D), w_in_b, w_in_b, w_out_b)


def _rot_half(w):
    half = QK_ROPE // 2
    return jnp.concatenate([-w[..., half:], w[..., :half]], axis=-1)


def _pack_w_in(w):
    D = w.shape[0]
    offs = np.cumsum((0,) + IN_SIZES)
    rx, ry, qkv, z, a, b, mq, mkv, gate = [w[:, offs[i]:offs[i + 1]] for i in range(len(IN_SIZES))]
    zeros = lambda n: jnp.zeros((D, n), w.dtype)
    kr = mkv[:, KV_LORA:]
    tail = LANES - ROPE_LANE0 - QK_ROPE
    cols = [rx, ry, zeros(U_QKV - 2 * D_RNN), qkv, z, gate,
            a, b, zeros(W_AB - 2 * GDN_HEADS), mq,
            mkv[:, :KV_LORA], zeros(ROPE_LANE0), kr, zeros(tail), zeros(ROPE_LANE0), _rot_half(kr), zeros(tail)]
    out = jnp.concatenate(cols, axis=1).astype(BF16)
    assert out.shape[1] == N_U
    return out


def _pack_w_uq(w):
    H = MLA_HEADS
    w = w.reshape(Q_LORA, H, QK_NOPE + QK_ROPE)
    nope, rope = w[..., :QK_NOPE], w[..., QK_NOPE:]
    tail = jnp.zeros((Q_LORA, H, LANES - QK_NOPE - QK_ROPE), w.dtype)
    a = jnp.concatenate([nope, rope, tail], axis=-1).reshape(Q_LORA, H * LANES)
    b = jnp.concatenate([jnp.zeros_like(nope), _rot_half(rope), tail], axis=-1).reshape(Q_LORA, H * LANES)
    return jnp.concatenate([a, b], axis=1).astype(BF16)


def _pack_w_ukv(w):
    H = MLA_HEADS
    w3 = w.reshape(KV_LORA, H, QK_NOPE + V_HEAD)
    w_uk, w_uv = w3[..., :QK_NOPE], w3[..., QK_NOPE:]
    kpad = jnp.concatenate([w_uk, jnp.zeros((KV_LORA, H, LANES - QK_NOPE), w.dtype)], axis=-1)
    wkv_p = jnp.concatenate([kpad.reshape(KV_LORA, H * LANES), w_uv.reshape(KV_LORA, H * V_HEAD)], axis=1)
    wuk_s = jnp.transpose(kpad, (1, 2, 0)).reshape(H * LANES, KV_LORA)
    wuv_c = w_uv.reshape(KV_LORA, H * V_HEAD)
    return wkv_p.astype(BF16), wuk_s.astype(BF16), wuv_c.astype(BF16)


def _rope_tables(T, pos0):
    inv = ROPE_BASE ** (-jnp.arange(0, QK_ROPE, 2, dtype=F32) / QK_ROPE)
    ang = (jnp.arange(T, dtype=F32) + pos0)[:, None] * inv[None, :]
    cos, sin = jnp.cos(ang), jnp.sin(ang)
    tail = jnp.zeros((T, LANES - ROPE_LANE0 - QK_ROPE), F32)
    cos_t = jnp.concatenate([jnp.ones((T, ROPE_LANE0), F32), cos, cos, tail], axis=1)
    sin_t = jnp.concatenate([jnp.zeros((T, ROPE_LANE0), F32), sin, sin, tail], axis=1)
    return cos_t[None], sin_t[None]


def _split_mod(mod, per_token):
    R = mod.shape[0]
    parts = jnp.split(mod, 6, axis=-1)
    return [p.reshape(1, R, D_MODEL) if per_token else p.reshape(R, 1, D_MODEL) for p in parts]


def kernel(x_prompt, x_sample, cache_ckv, cache_krope, state_rg_conv, state_rg_h, state_gdn_conv, state_gdn_S,
           page_table, c_prompt, c_sample, w_ada, b_ada, g_norm1, g_norm2, w_in, rg_conv_w, rg_conv_b, rg_wa,
           rg_ba, rg_wx, rg_bx, rg_lambda, gdn_conv_w, gdn_A_log, gdn_dt_bias, gdn_norm_g, mla_q_norm_g, w_uq,
           mla_kv_norm_g, w_ukv, w_rg_proj, w_gdn_proj, w_mla_proj, w_o, w_ffn_in, w_ffn_out, g_final):
    L = w_in.shape[0]
    Bp, T, D = x_prompt.shape
    Bd = x_sample.shape[0]
    past_len = page_table.shape[1] * PAGE_SIZE

    mod = _modulation(jnp.concatenate([c_prompt, c_sample], axis=0), w_ada, b_ada)
    cos_p, sin_p = _rope_tables(T, 0.0)
    cos_s, sin_s = _rope_tables(1, float(past_len))
    cos_s = jnp.broadcast_to(cos_s, (1, Bd, LANES))
    sin_s = jnp.broadcast_to(sin_s, (1, Bd, LANES))

    cache_krope_t = jnp.swapaxes(cache_krope, 2, 3)
    xp = x_prompt
    xs = x_sample.reshape(1, Bd, D)
    outs_p, outs_s = [], []
    for l in range(L):
        w_in_p = _pack_w_in(w_in[l])
        wq_p = _pack_w_uq(w_uq[l])
        wkv_p, wuk_s, wuv_c = _pack_w_ukv(w_ukv[l])
        wrg, wgdn, wmla, wo = (w_rg_proj[l].astype(BF16), w_gdn_proj[l].astype(BF16),
                               w_mla_proj[l].astype(BF16), w_o[l].astype(BF16))
        wfi, wfo = w_ffn_in[l].astype(BF16), w_ffn_out[l].astype(BF16)
        final = l == L - 1
        rg_w = (rg_conv_w[l], rg_conv_b[l], rg_wa[l], rg_ba[l], rg_wx[l], rg_bx[l], rg_lambda[l])
        gdn_w = (gdn_conv_w[l], gdn_A_log[l], gdn_dt_bias[l], gdn_norm_g[l])

        sh1, sc1, gt1, sh2, sc2, gt2 = _split_mod(mod[l, :Bp], per_token=False)
        u = _in_proj(xp, sc1, sh1, g_norm1[l], w_in_p, bm=min(1024, T), bn=1024)
        o_rg, rg_buf, rg_h = _rglru(u, jnp.zeros((Bp, CONV_W - 1, D_RNN), F32), jnp.zeros((Bp, D_RNN), F32),
                                    *rg_w, tt=128)
        o_gdn, gdn_buf, gdn_S = _gdn_prompt(u, *gdn_w, tt=256)
        q, k, v, ckv, kr = _mla_prep(u, cos_p, sin_p, mla_q_norm_g[l], mla_kv_norm_g[l], wq_p, wkv_p, bm=512)
        o_mla = _flash(q, k, v, tq=256)
        x1 = _merge(xp, o_rg, o_gdn, o_mla, u, gt1, wrg, wgdn, wmla, wo, bm=256)
        xp = _ffn(x1, sc2, sh2, gt2, g_norm2[l], g_final, wfi, wfo, bm=512, bf=1408, final=final)
        outs_p.append((ckv, kr, rg_buf, rg_h.reshape(Bp, D_RNN), gdn_buf, gdn_S))

        sh1, sc1, gt1, sh2, sc2, gt2 = _split_mod(mod[l, Bp:], per_token=True)
        us = _in_proj(xs, sc1, sh1, g_norm1[l], w_in_p, bm=Bd, bn=1024)
        us_seq = us.reshape(Bd, 1, N_U)
        o_rg, rg_buf, rg_h = _rglru(us_seq, state_rg_conv[l], state_rg_h[l], *rg_w, tt=1)
        o_gdn, gdn_buf, gdn_S = _gdn_step(us_seq, state_gdn_conv[l], state_gdn_S[l], *gdn_w)
        q, k, v, ckv, kr = _mla_prep(us, cos_s, sin_s, mla_q_norm_g[l], mla_kv_norm_g[l], wq_p, wkv_p, bm=Bd)
        o_mla = _sample_attention(page_table, q.reshape(Bd, MLA_HEADS * LANES), ckv.reshape(Bd, 1, KV_LORA),
                                  kr.reshape(Bd, 1, QK_ROPE), wuk_s, wuv_c, cache_ckv, cache_krope_t, l)
        x1 = _merge(xs, o_rg.reshape(1, Bd, D_RNN), o_gdn.reshape(1, Bd, GDN_VW), o_mla.reshape(1, Bd, D),
                    us, gt1, wrg, wgdn, wmla, wo, bm=Bd)
        xs = _ffn(x1, sc2, sh2, gt2, g_norm2[l], g_final, wfi, wfo, bm=Bd, bf=1408, final=final)
        outs_s.append((ckv.reshape(Bd, 1, KV_LORA), kr.reshape(Bd, 1, QK_ROPE), rg_buf, rg_h.reshape(Bd, D_RNN),
                       gdn_buf, gdn_S))

    stack = lambda outs, i: jnp.stack([o[i] for o in outs])
    return (xp, xs.reshape(Bd, 1, D),
            stack(outs_p, 0), stack(outs_p, 1), stack(outs_p, 2), stack(outs_p, 3), stack(outs_p, 4), stack(outs_p, 5),
            stack(outs_s, 0), stack(outs_s, 1), stack(outs_s, 2), stack(outs_s, 3), stack(outs_s, 4), stack(outs_s, 5))
```

```python
import functools
import math

import numpy as np
import jax
import jax.numpy as jnp
from jax import lax
from jax.experimental import pallas as pl
from jax.experimental.pallas import tpu as pltpu

F32 = jnp.float32
BF16 = jnp.bfloat16

D_MODEL = 1024
CONV_W = 4
D_RNN = 1280
RG_BLOCK = 128
RG_BLOCKS = D_RNN // RG_BLOCK
RG_C = 8.0
GDN_HEADS = 8
GDN_DK = 128
GDN_DV = 128
GDN_KW = GDN_HEADS * GDN_DK
GDN_VW = GDN_HEADS * GDN_DV
GDN_CONV_C = 2 * GDN_KW + GDN_VW
GDN_CHUNK = 64
MLA_HEADS = 8
Q_LORA = 384
KV_LORA = 256
QK_NOPE = 64
QK_ROPE = 32
V_HEAD = 128
MLA_SCALE = (QK_NOPE + QK_ROPE) ** -0.5
ROPE_BASE = 10000.0
D_FF = 2816
N_BRANCH = 3
IN_SIZES = (D_RNN, D_RNN, GDN_CONV_C, GDN_VW, GDN_HEADS, GDN_HEADS, Q_LORA, KV_LORA + QK_ROPE, N_BRANCH * D_MODEL)
EPS = 1e-6
PAGE_SIZE = 128

LANES = 128
SUBLANES = 8
VMEM_LIMIT = 56 * 1024 * 1024

U_RX = 0
U_RY = D_RNN
U_QKV = 3072
U_Z = 6144
U_GA = 7168
U_GB = 8192
U_GC = 9216
U_AB = 10240
U_MQ = 10368
U_MKV = 10752
N_U = 11264
W_AB = LANES
W_MKV = 512
ROPE_LANE0 = QK_NOPE


def _cparams(sem):
    return pltpu.CompilerParams(dimension_semantics=sem, vmem_limit_bytes=VMEM_LIMIT)


def _sigmoid(x):
    return 1.0 / (1.0 + jnp.exp(-x))


def _silu(x):
    return x * _sigmoid(x)


def _softplus(x):
    return jnp.maximum(x, 0.0) + jnp.log1p(jnp.exp(-jnp.abs(x)))


def _gelu_tanh(x):
    c = math.sqrt(2.0 / math.pi)
    return 0.5 * x * (1.0 + jnp.tanh(c * (x + 0.044715 * (x * x * x))))


def _rms(x, g):
    return x * lax.rsqrt(jnp.mean(x * x, axis=-1, keepdims=True) + EPS) * g


def _dot(a, b):
    return jnp.dot(a.astype(BF16), b.astype(BF16), preferred_element_type=F32)


def _dot_nt(a, b):
    return lax.dot_general(a.astype(BF16), b.astype(BF16), (((1,), (1,)), ((), ())), preferred_element_type=F32)


def _dot_tn(a, b):
    return lax.dot_general(a.astype(BF16), b.astype(BF16), (((0,), (0,)), ((), ())), preferred_element_type=F32)


def _split2(a):
    hi = a.astype(BF16)
    lo = (a - hi.astype(F32)).astype(BF16)
    return hi, lo


def _dot3(a, b):
    ah, al = _split2(a)
    bh, bl = _split2(b)
    d = functools.partial(jnp.dot, preferred_element_type=F32)
    return d(ah, bh) + (d(ah, bl) + d(al, bh))


def _dot_exact_lhs(a_bf16, b):
    b0 = b.astype(BF16)
    r1 = b - b0.astype(F32)
    b1 = r1.astype(BF16)
    b2 = (r1 - b1.astype(F32)).astype(BF16)
    d = functools.partial(jnp.dot, preferred_element_type=F32)
    return d(a_bf16, b0) + (d(a_bf16, b1) + d(a_bf16, b2))


def _mod_kernel(c_ref, w_ref, b_ref, o_ref):
    c = c_ref[...]
    o_ref[0] = _dot(_silu(c), w_ref[0]) + b_ref[0]


def _modulation(c_all, w_ada, b_ada):
    L, D, N = w_ada.shape
    R = c_all.shape[0]
    bn = 1536
    return pl.pallas_call(
        _mod_kernel,
        grid=(L, N // bn),
        in_specs=[pl.BlockSpec((R, D), lambda l, j: (0, 0)),
                  pl.BlockSpec((1, D, bn), lambda l, j: (l, 0, j)),
                  pl.BlockSpec((1, 1, bn), lambda l, j: (l, 0, j))],
        out_specs=pl.BlockSpec((1, R, bn), lambda l, j: (l, 0, j)),
        out_shape=jax.ShapeDtypeStruct((L, R, N), F32),
        name="adaln_mod",
        compiler_params=_cparams(("parallel", "parallel")),
    )(c_all, w_ada, b_ada.reshape(L, 1, N))


def _mod_spec(mod, bm):
    _, tm, d = mod.shape
    if tm == 1:
        return pl.BlockSpec((1, 1, d), lambda b, i, *_: (b, 0, 0))
    return pl.BlockSpec((1, bm, d), lambda b, i, *_: (b, i, 0))


def _in_kernel(x_ref, sc_ref, sh_ref, g_ref, w_ref, o_ref, hb_ref):
    @pl.when(pl.program_id(2) == 0)
    def _():
        h = _rms(x_ref[0], g_ref[...]) * (1.0 + sc_ref[0]) + sh_ref[0]
        hb_ref[...] = h.astype(BF16)

    o_ref[0] = lax.dot_general(hb_ref[...], w_ref[...], (((1,), (1,)), ((), ())), preferred_element_type=F32)


def _in_proj(x, sc, sh, g, w_packed, bm, bn):
    B, T, D = x.shape
    N = w_packed.shape[0]
    return pl.pallas_call(
        _in_kernel,
        grid=(B, T // bm, N // bn),
        in_specs=[pl.BlockSpec((1, bm, D), lambda b, i, j: (b, i, 0)),
                  _mod_spec(sc, bm), _mod_spec(sh, bm),
                  pl.BlockSpec((1, D), lambda b, i, j: (0, 0)),
                  pl.BlockSpec((bn, D), lambda b, i, j: (j, 0))],
        out_specs=pl.BlockSpec((1, bm, bn), lambda b, i, j: (b, i, j)),
        out_shape=jax.ShapeDtypeStruct((B, T, N), F32),
        scratch_shapes=[pltpu.VMEM((bm, D), BF16)],
        name="in_proj",
        compiler_params=_cparams(("parallel", "parallel", "arbitrary")),
    )(x, sc, sh, g.reshape(1, D), w_packed)


SCAN_PAD = 64


def _rg_kernel(ux_ref, uy_ref, buf_ref, h0_ref, cw_ref, cb_ref, wa_ref, ba_ref, wx_ref, bx_ref, lam_ref,
               o_ref, nbuf_ref, hl_ref, xbuf, abuf, bbuf, hc, *, tt):
    i = pl.program_id(1)
    nt = pl.num_programs(1)

    @pl.when(i == 0)
    def _():
        xbuf[0:SUBLANES, :] = buf_ref[0]
        hc[...] = h0_ref[0]
        abuf[0:SCAN_PAD, :] = jnp.ones((SCAN_PAD, D_RNN), F32)
        bbuf[0:SCAN_PAD, :] = jnp.zeros((SCAN_PAD, D_RNN), F32)

    xbuf[SUBLANES:SUBLANES + tt, :] = ux_ref[0]
    lo = SUBLANES - (CONV_W - 1)
    if tt >= SUBLANES:
        xs = xbuf[...]
        xc = cb_ref[...] + cw_ref[CONV_W - 1:CONV_W, :] * xs[SUBLANES:, :]
        for j in range(CONV_W - 1):
            xc = xc + cw_ref[j:j + 1, :] * pltpu.roll(xs, CONV_W - 1 - j, axis=0)[SUBLANES:, :]
    else:
        xc = cb_ref[...] + cw_ref[0:1, :] * xbuf[lo:lo + tt, :]
        for j in range(1, CONV_W):
            xc = xc + cw_ref[j:j + 1, :] * xbuf[lo + j:lo + j + tt, :]

    @pl.when(i == nt - 1)
    def _():
        nbuf_ref[0] = xbuf[tt + lo:tt + SUBLANES, :]

    if tt >= SUBLANES:
        xbuf[0:SUBLANES, :] = xbuf[tt:tt + SUBLANES, :]

    xb = xc.astype(BF16)
    ra = jnp.concatenate([jnp.dot(xb[:, n * RG_BLOCK:(n + 1) * RG_BLOCK], wa_ref[n], preferred_element_type=F32)
                          for n in range(RG_BLOCKS)], axis=1)
    ri = jnp.concatenate([jnp.dot(xb[:, n * RG_BLOCK:(n + 1) * RG_BLOCK], wx_ref[n], preferred_element_type=F32)
                          for n in range(RG_BLOCKS)], axis=1)
    r = _sigmoid(ra + ba_ref[...])
    ig = _sigmoid(ri + bx_ref[...])
    log_a = (-RG_C) * r * _softplus(-lam_ref[...])
    a = jnp.exp(log_a)
    b = jnp.sqrt(-jnp.tanh(log_a) * (a * a + 1.0)) * (ig * xc)

    abuf[SCAN_PAD:SCAN_PAD + tt, :] = a
    bbuf[SCAN_PAD:SCAN_PAD + tt, :] = b
    d = 1
    while d < tt:
        a_s = abuf[SCAN_PAD - d:SCAN_PAD - d + tt, :]
        b_s = bbuf[SCAN_PAD - d:SCAN_PAD - d + tt, :]
        a0 = abuf[SCAN_PAD:SCAN_PAD + tt, :]
        b0 = bbuf[SCAN_PAD:SCAN_PAD + tt, :]
        abuf[SCAN_PAD:SCAN_PAD + tt, :] = a0 * a_s
        bbuf[SCAN_PAD:SCAN_PAD + tt, :] = a0 * b_s + b0
        d *= 2
    h = bbuf[SCAN_PAD:SCAN_PAD + tt, :] + abuf[SCAN_PAD:SCAN_PAD + tt, :] * hc[...]
    hc[...] = h[tt - 1:tt, :]
    o_ref[0] = (h * _gelu_tanh(uy_ref[0])).astype(BF16)

    @pl.when(i == nt - 1)
    def _():
        hl_ref[0] = h[tt - 1:tt, :]


def _rglru(u, buf, h0, cw, cb, wa, ba, wx, bx, lam, tt):
    B, T, _ = u.shape
    C = D_RNN
    buf8 = jnp.pad(buf, ((0, 0), (SUBLANES - (CONV_W - 1), 0), (0, 0)))
    vec = lambda v: v.reshape(1, C)
    const2 = lambda shape: pl.BlockSpec(shape, lambda b, i: (0,) * len(shape))
    return pl.pallas_call(
        functools.partial(_rg_kernel, tt=tt),
        grid=(B, T // tt),
        in_specs=[pl.BlockSpec((1, tt, C), lambda b, i: (b, i, U_RX // C)),
                  pl.BlockSpec((1, tt, C), lambda b, i: (b, i, U_RY // C)),
                  pl.BlockSpec((1, SUBLANES, C), lambda b, i: (b, 0, 0)),
                  pl.BlockSpec((1, 1, C), lambda b, i: (b, 0, 0)),
                  const2((CONV_W, C)), const2((1, C)),
                  const2((RG_BLOCKS, RG_BLOCK, RG_BLOCK)), const2((1, C)),
                  const2((RG_BLOCKS, RG_BLOCK, RG_BLOCK)), const2((1, C)), const2((1, C))],
        out_specs=[pl.BlockSpec((1, tt, C), lambda b, i: (b, i, 0)),
                   pl.BlockSpec((1, CONV_W - 1, C), lambda b, i: (b, 0, 0)),
                   pl.BlockSpec((1, 1, C), lambda b, i: (b, 0, 0))],
        out_shape=[jax.ShapeDtypeStruct((B, T, C), BF16),
                   jax.ShapeDtypeStruct((B, CONV_W - 1, C), F32),
                   jax.ShapeDtypeStruct((B, 1, C), F32)],
        scratch_shapes=[pltpu.VMEM((tt + SUBLANES, C), F32),
                        pltpu.VMEM((SCAN_PAD + tt, C), F32),
                        pltpu.VMEM((SCAN_PAD + tt, C), F32),
                        pltpu.VMEM((1, C), F32)],
        name="rglru",
        compiler_params=_cparams(("parallel", "arbitrary")),
    )(u, u, buf8, h0.reshape(B, 1, C), cw, vec(cb), wa.astype(BF16), vec(ba), wx.astype(BF16), vec(bx), vec(lam))


TRI_BASE = 8
INV_GROUP = 8
NEWTON_STEPS = 2


def _cat_dot3(x, y, bd_b, n):
    C = x.shape[0]
    xh, xl = _split2(x)
    yh, yl = _split2(y)
    d = functools.partial(jnp.dot, preferred_element_type=F32)
    r = d(jnp.concatenate([xh, xl], axis=0), jnp.concatenate([yh] * n, axis=0) * bd_b)
    return r[:C] + (r[C:] + d(xh, jnp.concatenate([yl] * n, axis=0) * bd_b))


def _cat_dot1(x, y, bd_b, n):
    yb = jnp.concatenate([y.astype(BF16)] * n, axis=0) * bd_b
    return jnp.dot(x.astype(BF16), yb, preferred_element_type=F32)


def _tri_inv_cat(Lcs, masks, bd_b, n):
    eye, base_mask, level_masks = masks
    mm = functools.partial(_cat_dot1, bd_b=bd_b, n=n)
    mm3 = functools.partial(_cat_dot3, bd_b=bd_b, n=n)
    N = [jnp.where(base_mask, -Lc, 0.0) for Lc in Lcs]
    P = [eye + x for x in N]
    N2 = [mm(x, x) for x in N]
    P = [p + mm(p, x) for p, x in zip(P, N2)]
    N4 = [mm(x, x) for x in N2]
    P = [p + mm(p, x) for p, x in zip(P, N4)]
    for m in level_masks:
        PO = [mm(p, jnp.where(m, Lc, 0.0)) for p, Lc in zip(P, Lcs)]
        P = [p - mm(po, p) for p, po in zip(P, PO)]
    for _ in range(NEWTON_STEPS):
        R = [eye - p - mm3(Lc, p) for p, Lc in zip(P, Lcs)]
        P = [p + mm(p, r) for p, r in zip(P, R)]
    return P


def _cat_masks(C, n):
    rr = lax.broadcasted_iota(jnp.int32, (C, n * C), 0)
    jj = lax.broadcasted_iota(jnp.int32, (C, n * C), 1) % C
    same = lambda s: (rr // s) == (jj // s)
    eye = (rr == jj).astype(F32)
    levels = []
    s = TRI_BASE
    while s < C:
        levels.append(jnp.logical_and(same(2 * s), jnp.logical_not(same(s))))
        s *= 2
    return eye, same(TRI_BASE), levels


def _gdn_kernel(qkv_ref, z_ref, ab_ref, buf_ref, cw_ref, alog_ref, dtb_ref, ng_ref,
                o_ref, nbuf_ref, S_ref, xbuf, act, S_sc, wq_sc, uv_sc, qk_sc, kd_sc, gl_sc, lc_sc, rhs_sc, *, tt):
    i = pl.program_id(1)
    nt = pl.num_programs(1)
    C = GDN_CHUNK
    H = GDN_HEADS
    n = tt // C

    @pl.when(i == 0)
    def _():
        xbuf[0:SUBLANES, :] = buf_ref[0]
        S_sc[...] = jnp.zeros(S_sc.shape, F32)

    xbuf[SUBLANES:SUBLANES + tt, :] = qkv_ref[0]
    lo = SUBLANES - (CONV_W - 1)
    xs = xbuf[...]
    y = cw_ref[CONV_W - 1:CONV_W, :] * xs[SUBLANES:, :]
    for j in range(CONV_W - 1):
        y = y + cw_ref[j:j + 1, :] * pltpu.roll(xs, CONV_W - 1 - j, axis=0)[SUBLANES:, :]
    act[...] = _silu(y)

    @pl.when(i == nt - 1)
    def _():
        nbuf_ref[0] = xbuf[tt + lo:tt + SUBLANES, :]

    xbuf[0:SUBLANES, :] = xbuf[tt:tt + SUBLANES, :]

    ab = ab_ref[0]
    g = -jnp.exp(alog_ref[...]) * _softplus(ab + dtb_ref[...])
    beta = _sigmoid(ab)

    ri = lax.broadcasted_iota(jnp.int32, (tt, tt), 0)
    ci = lax.broadcasted_iota(jnp.int32, (tt, tt), 1)
    bd = (ri // C) == (ci // C)
    bd_incl = jnp.logical_and(bd, ri >= ci)
    bd_strict = jnp.logical_and(bd, ri > ci)
    bd_b = bd.astype(BF16)
    last_b = jnp.logical_and(bd, ci % C == C - 1).astype(BF16)
    gcum = _dot_exact_lhs(bd_incl.astype(BF16), g)
    glast = _dot_exact_lhs(last_b, gcum)
    gl_sc[...] = jnp.exp(glast)
    gcum_t = gcum.T
    masks = _cat_masks(C, n)
    ng = ng_ref[...]

    for h in range(H):
        q = act[:, h * GDN_DK:(h + 1) * GDN_DK]
        k = act[:, GDN_KW + h * GDN_DK:GDN_KW + (h + 1) * GDN_DK]
        v = act[:, 2 * GDN_KW + h * GDN_DV:2 * GDN_KW + (h + 1) * GDN_DV]
        q = q * lax.rsqrt(jnp.sum(q * q, axis=-1, keepdims=True) + EPS) * (GDN_DK ** -0.5)
        k = k * lax.rsqrt(jnp.sum(k * k, axis=-1, keepdims=True) + EPS)
        gc = gcum[:, h:h + 1]
        gr = gcum_t[h:h + 1, :]
        e = jnp.exp(jnp.where(bd_incl, gc - gr, -jnp.inf))
        bt = beta[:, H + h:H + h + 1]
        kb = k.astype(BF16)
        qb = q.astype(BF16)
        Lf = jnp.where(bd_strict, bt * _dot_nt(kb, kb) * e, 0.0)
        qkd = (_dot_nt(qb, kb) * e).astype(BF16)
        for c in range(n):
            qk_sc[h, c] = qkd[c * C:(c + 1) * C, c * C:(c + 1) * C]
        Lc = Lf[0:C]
        for c in range(1, n):
            Lc = Lc + Lf[c * C:(c + 1) * C]
        lc_sc[h] = Lc
        eg = jnp.exp(gc)
        rhs_sc[h, :, 0:GDN_DV] = bt * v
        rhs_sc[h, :, GDN_DV:] = (bt * eg) * k
        qe = (eg * q).astype(BF16)
        for c in range(n):
            wq_sc[h, c, C:2 * C, :] = qe[c * C:(c + 1) * C]
        kd_sc[h] = (k * jnp.exp(glast[:, h:h + 1] - gc)).astype(BF16)

    for h0 in range(0, H, INV_GROUP):
        hs = range(h0, h0 + INV_GROUP)
        t_cats = _tri_inv_cat([lc_sc[h] for h in hs], masks, bd_b, n)
        for h, t_cat in zip(hs, t_cats):
            t_bd = jnp.where(bd, jnp.concatenate([t_cat] * n, axis=0), 0.0)
            uw = _dot3(t_bd, rhs_sc[h])
            uv_sc[h] = uw[:, :GDN_DV]
            w = uw[:, GDN_DV:].astype(BF16)
            for c in range(n):
                wq_sc[h, c, 0:C, :] = w[c * C:(c + 1) * C]

    def chunk(c, carry):
        r0 = pl.multiple_of(c * C, C)
        rows = pl.ds(r0, C)
        gl = gl_sc[pl.ds(r0, 1), :]
        S = [S_sc[h] for h in range(H)]
        wq = [jnp.dot(wq_sc[h, c], S[h].astype(BF16), preferred_element_type=F32) for h in range(H)]
        Ub = [(uv_sc[h, rows, :] - wq[h][:C]).astype(BF16) for h in range(H)]
        o = [wq[h][C:] + jnp.dot(qk_sc[h, c], Ub[h], preferred_element_type=F32) for h in range(H)]
        for h in range(H):
            S_sc[h] = gl[:, h:h + 1] * S[h] + _dot_tn(kd_sc[h, rows, :], Ub[h])
        for h in range(H):
            zz = z_ref[0, rows, h * GDN_DV:(h + 1) * GDN_DV]
            o_ref[0, rows, h * GDN_DV:(h + 1) * GDN_DV] = (_rms(o[h], ng) * _silu(zz)).astype(BF16)
        return carry

    lax.fori_loop(0, n, chunk, 0)

    @pl.when(i == nt - 1)
    def _():
        S_ref[0] = S_sc[...]


def _gdn_prompt(u, cw, alog, dtb, ng, tt):
    B, T, _ = u.shape
    Cc = GDN_CONV_C
    H = GDN_HEADS
    buf8 = jnp.zeros((B, SUBLANES, Cc), F32)
    lane_vec = lambda v: jnp.pad(v, (0, LANES - v.shape[0])).reshape(1, LANES)
    const2 = lambda shape: pl.BlockSpec(shape, lambda b, i: (0,) * len(shape))
    return pl.pallas_call(
        functools.partial(_gdn_kernel, tt=tt),
        grid=(B, T // tt),
        in_specs=[pl.BlockSpec((1, tt, Cc), lambda b, i: (b, i, U_QKV // Cc)),
                  pl.BlockSpec((1, tt, GDN_VW), lambda b, i: (b, i, U_Z // GDN_VW)),
                  pl.BlockSpec((1, tt, W_AB), lambda b, i: (b, i, U_AB // W_AB)),
                  pl.BlockSpec((1, SUBLANES, Cc), lambda b, i: (b, 0, 0)),
                  const2((CONV_W, Cc)), const2((1, LANES)), const2((1, LANES)), const2((1, GDN_DV))],
        out_specs=[pl.BlockSpec((1, tt, GDN_VW), lambda b, i: (b, i, 0)),
                   pl.BlockSpec((1, CONV_W - 1, Cc), lambda b, i: (b, 0, 0)),
                   pl.BlockSpec((1, H, GDN_DK, GDN_DV), lambda b, i: (b, 0, 0, 0))],
        out_shape=[jax.ShapeDtypeStruct((B, T, GDN_VW), BF16),
                   jax.ShapeDtypeStruct((B, CONV_W - 1, Cc), F32),
                   jax.ShapeDtypeStruct((B, H, GDN_DK, GDN_DV), F32)],
        scratch_shapes=[pltpu.VMEM((tt + SUBLANES, Cc), F32),
                        pltpu.VMEM((tt, Cc), F32),
                        pltpu.VMEM((H, GDN_DK, GDN_DV), F32),
                        pltpu.VMEM((H, tt // GDN_CHUNK, 2 * GDN_CHUNK, GDN_DK), BF16),
                        pltpu.VMEM((H, tt, GDN_DV), F32),
                        pltpu.VMEM((H, tt // GDN_CHUNK, GDN_CHUNK, GDN_CHUNK), BF16),
                        pltpu.VMEM((H, tt, GDN_DK), BF16),
                        pltpu.VMEM((tt, LANES), F32),
                        pltpu.VMEM((H, GDN_CHUNK, tt), F32),
                        pltpu.VMEM((H, tt, GDN_DV + GDN_DK), F32)],
        name="gdn_chunked",
        compiler_params=_cparams(("parallel", "arbitrary")),
    )(u, u, u, buf8, cw, lane_vec(alog), lane_vec(dtb), ng.reshape(1, GDN_DV))


def _gdn_step_kernel(qkv_ref, z_ref, ab_ref, buf_ref, S0_ref, cw_ref, alog_ref, dtb_ref, ng_ref,
                     o_ref, nbuf_ref, S_ref, xbuf):
    H = GDN_HEADS
    xbuf[0:SUBLANES, :] = buf_ref[0]
    xbuf[SUBLANES:SUBLANES + 1, :] = qkv_ref[0]
    lo = SUBLANES - (CONV_W - 1)
    y = cw_ref[0:1, :] * xbuf[lo:lo + 1, :]
    for j in range(1, CONV_W):
        y = y + cw_ref[j:j + 1, :] * xbuf[lo + j:lo + j + 1, :]
    y = _silu(y)
    nbuf_ref[0] = xbuf[lo + 1:SUBLANES + 1, :]

    ab = ab_ref[0]
    g_all = -jnp.exp(alog_ref[...]) * _softplus(ab + dtb_ref[...])
    beta_all = _sigmoid(ab)
    ii = lax.broadcasted_iota(jnp.int32, (GDN_DK, GDN_DK), 0)
    jj = lax.broadcasted_iota(jnp.int32, (GDN_DK, GDN_DK), 1)
    eye = ii == jj
    ng = ng_ref[...]
    for h in range(H):
        q = y[:, h * GDN_DK:(h + 1) * GDN_DK]
        k = y[:, GDN_KW + h * GDN_DK:GDN_KW + (h + 1) * GDN_DK]
        v = y[:, 2 * GDN_KW + h * GDN_DV:2 * GDN_KW + (h + 1) * GDN_DV]
        q = q * lax.rsqrt(jnp.sum(q * q, axis=-1, keepdims=True) + EPS) * (GDN_DK ** -0.5)
        k = k * lax.rsqrt(jnp.sum(k * k, axis=-1, keepdims=True) + EPS)
        eg = jnp.exp(g_all[:, h:h + 1])
        bt = beta_all[:, H + h:H + h + 1]
        S = S0_ref[0, h]
        Sb = S.astype(BF16)
        kb = k.astype(BF16)
        qb = q.astype(BF16)
        u = bt * (v - eg * jnp.dot(kb, Sb, preferred_element_type=F32))
        ub = u.astype(BF16)
        qk = jnp.sum(qb.astype(F32) * kb.astype(F32), axis=-1, keepdims=True)
        o = eg * jnp.dot(qb, Sb, preferred_element_type=F32) + qk.astype(BF16).astype(F32) * ub.astype(F32)
        kdiag = jnp.where(eye, jnp.broadcast_to(kb.astype(F32), (GDN_DK, GDN_DK)), 0.0).astype(BF16)
        urows = jnp.broadcast_to(ub, (GDN_DK, GDN_DV))
        S_ref[0, h] = eg * S + jnp.dot(kdiag, urows, preferred_element_type=F32)
        zz = z_ref[0, :, h * GDN_DV:(h + 1) * GDN_DV]
        o_ref[0, :, h * GDN_DV:(h + 1) * GDN_DV] = (_rms(o, ng) * _silu(zz)).astype(BF16)


def _gdn_step(u, buf, S0, cw, alog, dtb, ng):
    B = u.shape[0]
    Cc = GDN_CONV_C
    H = GDN_HEADS
    buf8 = jnp.pad(buf, ((0, 0), (SUBLANES - (CONV_W - 1), 0), (0, 0)))
    lane_vec = lambda v: jnp.pad(v, (0, LANES - v.shape[0])).reshape(1, LANES)
    const1 = lambda shape: pl.BlockSpec(shape, lambda b: (0,) * len(shape))
    return pl.pallas_call(
        _gdn_step_kernel,
        grid=(B,),
        in_specs=[pl.BlockSpec((1, 1, Cc), lambda b: (b, 0, U_QKV // Cc)),
                  pl.BlockSpec((1, 1, GDN_VW), lambda b: (b, 0, U_Z // GDN_VW)),
                  pl.BlockSpec((1, 1, W_AB), lambda b: (b, 0, U_AB // W_AB)),
                  pl.BlockSpec((1, SUBLANES, Cc), lambda b: (b, 0, 0)),
                  pl.BlockSpec((1, H, GDN_DK, GDN_DV), lambda b: (b, 0, 0, 0)),
                  const1((CONV_W, Cc)), const1((1, LANES)), const1((1, LANES)), const1((1, GDN_DV))],
        out_specs=[pl.BlockSpec((1, 1, GDN_VW), lambda b: (b, 0, 0)),
                   pl.BlockSpec((1, CONV_W - 1, Cc), lambda b: (b, 0, 0)),
                   pl.BlockSpec((1, H, GDN_DK, GDN_DV), lambda b: (b, 0, 0, 0))],
        out_shape=[jax.ShapeDtypeStruct((B, 1, GDN_VW), BF16),
                   jax.ShapeDtypeStruct((B, CONV_W - 1, Cc), F32),
                   jax.ShapeDtypeStruct((B, H, GDN_DK, GDN_DV), F32)],
        scratch_shapes=[pltpu.VMEM((2 * SUBLANES, Cc), F32)],
        name="gdn_step",
        compiler_params=_cparams(("parallel",)),
    )(u, u, u, buf8, S0, cw, lane_vec(alog), lane_vec(dtb), ng.reshape(1, GDN_DV))


def _mla_prep_kernel(mq_ref, mkv_ref, cos_ref, sin_ref, gq_ref, gkv_ref, wq_ref, wkv_ref,
                     q_ref, k_ref, v_ref, ckv_ref, kr_ref):
    H = MLA_HEADS
    W = LANES * H
    cos = cos_ref[0]
    sin = sin_ref[0]
    cq = _rms(mq_ref[0], gq_ref[...])
    qa = _dot(cq, wq_ref[...])
    mkv = mkv_ref[0]
    ckv = _rms(mkv[:, :KV_LORA], gkv_ref[...])
    ckv_ref[0] = ckv
    kro = mkv[:, KV_LORA:KV_LORA + LANES] * cos + mkv[:, KV_LORA + LANES:KV_LORA + 2 * LANES] * sin
    kr_ref[0] = kro[:, ROPE_LANE0:ROPE_LANE0 + QK_ROPE]
    kv = _dot(ckv, wkv_ref[...])
    for h in range(H):
        sl = slice(h * LANES, (h + 1) * LANES)
        qh = qa[:, sl] * cos + qa[:, W + h * LANES:W + (h + 1) * LANES] * sin
        q_ref[0, :, sl] = (qh * MLA_SCALE).astype(BF16)
        k_ref[0, :, sl] = (kv[:, sl] + kro).astype(BF16)
    v_ref[0] = kv[:, W:].astype(BF16)


def _mla_prep(u, cos, sin, gq, gkv, wq_p, wkv_p, bm):
    B, T, _ = u.shape
    W = LANES * MLA_HEADS
    const2 = lambda shape: pl.BlockSpec(shape, lambda b, i: (0,) * len(shape))
    return pl.pallas_call(
        _mla_prep_kernel,
        grid=(B, T // bm),
        in_specs=[pl.BlockSpec((1, bm, Q_LORA), lambda b, i: (b, i, U_MQ // Q_LORA)),
                  pl.BlockSpec((1, bm, W_MKV), lambda b, i: (b, i, U_MKV // W_MKV)),
                  pl.BlockSpec((1, bm, LANES), lambda b, i: (0, i, 0)),
                  pl.BlockSpec((1, bm, LANES), lambda b, i: (0, i, 0)),
                  const2((1, Q_LORA)), const2((1, KV_LORA)),
                  const2((Q_LORA, 2 * W)), const2((KV_LORA, 2 * W))],
        out_specs=[pl.BlockSpec((1, bm, W), lambda b, i: (b, i, 0)),
                   pl.BlockSpec((1, bm, W), lambda b, i: (b, i, 0)),
                   pl.BlockSpec((1, bm, W), lambda b, i: (b, i, 0)),
                   pl.BlockSpec((1, bm, KV_LORA), lambda b, i: (b, i, 0)),
                   pl.BlockSpec((1, bm, QK_ROPE), lambda b, i: (b, i, 0))],
        out_shape=[jax.ShapeDtypeStruct((B, T, W), BF16),
                   jax.ShapeDtypeStruct((B, T, W), BF16),
                   jax.ShapeDtypeStruct((B, T, W), BF16),
                   jax.ShapeDtypeStruct((B, T, KV_LORA), F32),
                   jax.ShapeDtypeStruct((B, T, QK_ROPE), F32)],
        name="mla_prep",
        compiler_params=_cparams(("parallel", "parallel")),
    )(u, u, cos, sin, gq.reshape(1, Q_LORA), gkv.reshape(1, KV_LORA), wq_p, wkv_p)


NEG_BIG = -1e30


FLASH_HEADS = 4
FLASH_KV_FACTOR = 2


def _flash_kernel(q_ref, k_ref, v_ref, o_ref, *, tq):
    qi = pl.program_id(2)
    hp = FLASH_HEADS
    heads = [slice(h * LANES, (h + 1) * LANES) for h in range(hp)]
    qs = [q_ref[0, :, sl] for sl in heads]
    tk = FLASH_KV_FACTOR * tq
    n_full = (qi * tq) // tk

    def step(j, carry, diagonal):
        ks = pl.ds(pl.multiple_of(j * tk, tk), tk)
        if diagonal:
            row = qi * tq + lax.broadcasted_iota(jnp.int32, (tq, tk), 0)
            col = j * tk + lax.broadcasted_iota(jnp.int32, (tq, tk), 1)
            causal = col <= row
        out = []
        for h, sl in enumerate(heads):
            m, l, acc = carry[h]
            s = lax.dot_general(qs[h], k_ref[0, ks, sl], (((1,), (1,)), ((), ())), preferred_element_type=F32)
            if diagonal:
                s = jnp.where(causal, s, NEG_BIG)
            m_new = jnp.maximum(m, jnp.max(s, axis=-1, keepdims=True))
            alpha = jnp.exp(m - m_new)
            p = jnp.exp(s - m_new)
            l = alpha * l + jnp.sum(p, axis=-1, keepdims=True)
            acc = alpha * acc + jnp.dot(p.astype(BF16), v_ref[0, ks, sl], preferred_element_type=F32)
            out.append((m_new, l, acc))
        return tuple(out)

    init = tuple((jnp.full((tq, 1), NEG_BIG, F32), jnp.zeros((tq, 1), F32), jnp.zeros((tq, LANES), F32))
                 for _ in heads)
    carry = lax.fori_loop(0, n_full, lambda j, c: step(j, c, False), init)
    carry = step(n_full, carry, True)
    for h, sl in enumerate(heads):
        _, l, acc = carry[h]
        o_ref[0, :, sl] = (acc / l).astype(BF16)


def _flash(q, k, v, tq):
    B, T, W = q.shape
    wb = FLASH_HEADS * LANES
    H = W // wb
    return pl.pallas_call(
        functools.partial(_flash_kernel, tq=tq),
        grid=(B, H, T // tq),
        in_specs=[pl.BlockSpec((1, tq, wb), lambda b, h, i: (b, i, h)),
                  pl.BlockSpec((1, T, wb), lambda b, h, i: (b, 0, h)),
                  pl.BlockSpec((1, T, wb), lambda b, h, i: (b, 0, h))],
        out_specs=pl.BlockSpec((1, tq, wb), lambda b, h, i: (b, i, h)),
        out_shape=jax.ShapeDtypeStruct((B, T, W), BF16),
        name="mla_flash",
        compiler_params=_cparams(("parallel", "parallel", "arbitrary")),
    )(q, k, v)


PAGES_PER_STEP = 16


def _sattn_kernel(pt_ref, q_ref, cn_ref, krn_ref, wuk_ref, wuv_ref, *refs):
    n = PAGES_PER_STEP
    ck_refs = refs[:n]
    kr_refs = refs[n:2 * n]
    o_ref = refs[2 * n]
    qlat_sc, m_sc, l_sc, acc_sc = refs[2 * n + 1:]
    H = MLA_HEADS
    p_idx = pl.program_id(1)
    n_steps = pl.num_programs(1)
    q = q_ref[0]
    q_rope = q[:, ROPE_LANE0:ROPE_LANE0 + QK_ROPE]

    @pl.when(p_idx == 0)
    def _():
        rr = lax.broadcasted_iota(jnp.int32, (H, H * LANES), 0)
        cc = lax.broadcasted_iota(jnp.int32, (H, H * LANES), 1)
        q_bd = jnp.where(cc // LANES == rr, jnp.concatenate([q] * H, axis=1), jnp.zeros((), BF16))
        qlat_sc[...] = jnp.dot(q_bd, wuk_ref[...], preferred_element_type=F32).astype(BF16)
        m_sc[...] = jnp.full(m_sc.shape, NEG_BIG, F32)
        l_sc[...] = jnp.zeros(l_sc.shape, F32)
        acc_sc[...] = jnp.zeros(acc_sc.shape, F32)

    qlat = qlat_sc[...]
    ck = jnp.concatenate([r[0, 0] for r in ck_refs], axis=0).astype(BF16)
    kr_t = jnp.concatenate([r[0, 0] for r in kr_refs], axis=1).astype(BF16)
    s = _dot_nt(qlat, ck) + jnp.dot(q_rope, kr_t, preferred_element_type=F32)
    m_new = jnp.maximum(m_sc[...], jnp.max(s, axis=-1, keepdims=True))
    alpha = jnp.exp(m_sc[...] - m_new)
    p = jnp.exp(s - m_new)
    l_sc[...] = alpha * l_sc[...] + jnp.sum(p, axis=-1, keepdims=True)
    acc_sc[...] = alpha * acc_sc[...] + jnp.dot(p.astype(BF16), ck, preferred_element_type=F32)
    m_sc[...] = m_new

    @pl.when(p_idx == n_steps - 1)
    def _():
        cn = cn_ref[0].astype(BF16).astype(F32)
        krn = krn_ref[0].astype(BF16).astype(F32)
        s_n = (jnp.sum(qlat.astype(F32) * cn, axis=-1, keepdims=True)
               + jnp.sum(q_rope.astype(F32) * krn, axis=-1, keepdims=True))
        m_f = jnp.maximum(m_sc[...], s_n)
        a_f = jnp.exp(m_sc[...] - m_f)
        p_n = jnp.exp(s_n - m_f)
        l_f = a_f * l_sc[...] + p_n
        o_lat = (a_f * acc_sc[...] + p_n.astype(BF16).astype(F32) * cn) / l_f
        o_all = jnp.dot(o_lat.astype(BF16), wuv_ref[...], preferred_element_type=F32)
        rr = lax.broadcasted_iota(jnp.int32, o_all.shape, 0)
        cc = lax.broadcasted_iota(jnp.int32, o_all.shape, 1)
        o_ref[0] = jnp.sum(jnp.where(cc // LANES == rr, o_all, 0.0), axis=0, keepdims=True).astype(BF16)


def _sample_attention(page_table, q, ckv_new, kr_new, wuk_s, wuv_c, cache_ckv, cache_krope_t, layer):
    Bd = q.shape[0]
    H = MLA_HEADS
    n = PAGES_PER_STEP
    n_pages = page_table.shape[1]
    q3 = q.reshape(Bd, H, LANES)

    def page_spec(shape, k):
        return pl.BlockSpec((1, 1) + shape, lambda b, p, pt: (layer, pt[b, p * n + k], 0, 0))

    const = lambda shape: pl.BlockSpec(shape, lambda b, p, pt: (0,) * len(shape))
    grid_spec = pltpu.PrefetchScalarGridSpec(
        num_scalar_prefetch=1,
        grid=(Bd, n_pages // n),
        in_specs=[pl.BlockSpec((1, H, LANES), lambda b, p, pt: (b, 0, 0)),
                  pl.BlockSpec((1, 1, KV_LORA), lambda b, p, pt: (b, 0, 0)),
                  pl.BlockSpec((1, 1, QK_ROPE), lambda b, p, pt: (b, 0, 0)),
                  const((H * LANES, KV_LORA)), const((KV_LORA, H * V_HEAD))]
                 + [page_spec((PAGE_SIZE, KV_LORA), k) for k in range(n)]
                 + [page_spec((QK_ROPE, PAGE_SIZE), k) for k in range(n)],
        out_specs=pl.BlockSpec((1, 1, H * V_HEAD), lambda b, p, pt: (b, 0, 0)),
        scratch_shapes=[pltpu.VMEM((H, KV_LORA), BF16),
                        pltpu.VMEM((H, 1), F32), pltpu.VMEM((H, 1), F32),
                        pltpu.VMEM((H, KV_LORA), F32)],
    )
    return pl.pallas_call(
        _sattn_kernel,
        grid_spec=grid_spec,
        out_shape=jax.ShapeDtypeStruct((Bd, 1, H * V_HEAD), BF16),
        name="mla_decode",
        compiler_params=_cparams(("parallel", "arbitrary")),
    )(page_table, q3, ckv_new, kr_new, wuk_s, wuv_c, *([cache_ckv] * n), *([cache_krope_t] * n))


def _merge_kernel(x_ref, org_ref, ogdn_ref, omla_ref, ga_ref, gb_ref, gc_ref, gt_ref,
                  wrg_ref, wgdn_ref, wmla_ref, wo_ref, o_ref):
    d = functools.partial(jnp.dot, preferred_element_type=F32)
    m = (_sigmoid(ga_ref[0]) * d(org_ref[0], wrg_ref[...])
         + _sigmoid(gb_ref[0]) * d(ogdn_ref[0], wgdn_ref[...])
         + _sigmoid(gc_ref[0]) * d(omla_ref[0], wmla_ref[...]))
    o_ref[0] = x_ref[0] + gt_ref[0] * d(m.astype(BF16), wo_ref[...])


def _merge(x, o_rg, o_gdn, o_mla, u, gt, wrg, wgdn, wmla, wo, bm):
    B, T, D = x.shape
    tok = lambda w, col: pl.BlockSpec((1, bm, w), lambda b, i: (b, i, col))
    const2 = lambda shape: pl.BlockSpec(shape, lambda b, i: (0,) * len(shape))
    return pl.pallas_call(
        _merge_kernel,
        grid=(B, T // bm),
        in_specs=[tok(D, 0), tok(D_RNN, 0), tok(GDN_VW, 0), tok(D, 0),
                  tok(D, U_GA // D), tok(D, U_GB // D), tok(D, U_GC // D),
                  _mod_spec(gt, bm),
                  const2((D_RNN, D)), const2((GDN_VW, D)), const2((D, D)), const2((D, D))],
        out_specs=tok(D, 0),
        out_shape=jax.ShapeDtypeStruct((B, T, D), F32),
        name="branch_merge",
        compiler_params=_cparams(("parallel", "parallel")),
    )(x, o_rg, o_gdn, o_mla, u, u, u, gt, wrg, wgdn, wmla, wo)


def _ffn_kernel(x_ref, sc_ref, sh_ref, gt_ref, g_ref, gf_ref, wg_ref, wu_ref, wd_ref, o_ref, hb_ref, acc_ref, *, final):
    j = pl.program_id(2)

    @pl.when(j == 0)
    def _():
        h = _rms(x_ref[0], g_ref[...]) * (1.0 + sc_ref[0]) + sh_ref[0]
        hb_ref[...] = h.astype(BF16)
        acc_ref[...] = jnp.zeros(acc_ref.shape, F32)

    hb = hb_ref[...]
    gate = jnp.dot(hb, wg_ref[...], preferred_element_type=F32)
    up = jnp.dot(hb, wu_ref[...], preferred_element_type=F32)
    acc_ref[...] += jnp.dot((_silu(gate) * up).astype(BF16), wd_ref[...], preferred_element_type=F32)

    @pl.when(j == pl.num_programs(2) - 1)
    def _():
        y = x_ref[0] + gt_ref[0] * acc_ref[...]
        o_ref[0] = _rms(y, gf_ref[...]) if final else y


def _ffn(x, sc, sh, gt, g, g_final, w_in_b, w_out_b, bm, bf, final):
    B, T, D = x.shape
    nf = D_FF // bf
    return pl.pallas_call(
        functools.partial(_ffn_kernel, final=final),
        grid=(B, T // bm, nf),
        in_specs=[pl.BlockSpec((1, bm, D), lambda b, i, j: (b, i, 0)),
                  _mod_spec(sc, bm), _mod_spec(sh, bm), _mod_spec(gt, bm),
                  pl.BlockSpec((1, D), lambda b, i, j: (0, 0)),
                  pl.BlockSpec((1, D), lambda b, i, j: (0, 0)),
                  pl.BlockSpec((D, bf), lambda b, i, j: (0, j)),
                  pl.BlockSpec((D, bf), lambda b, i, j: (0, nf + j)),
                  pl.BlockSpec((bf, D), lambda b, i, j: (j, 0))],
        out_specs=pl.BlockSpec((1, bm, D), lambda b, i, j: (b, i, 0)),
        out_shape=jax.ShapeDtypeStruct((B, T, D), F32),
        scratch_shapes=[pltpu.VMEM((bm, D), BF16), pltpu.VMEM((bm, D), F32)],
        name="swiglu",
        compiler_params=_cparams(("parallel", "parallel", "arbitrary")),
    )(x, sc, sh, gt, g.reshape(1, D), g_final.reshape(1, D), w_in_b, w_in_b, w_out_b)


def _rot_half(w):
    half = QK_ROPE // 2
    return jnp.concatenate([-w[..., half:], w[..., :half]], axis=-1)


def _pack_w_in(w):
    D = w.shape[0]
    wt = w.T
    offs = np.cumsum((0,) + IN_SIZES)
    rx, ry, qkv, z, a, b, mq, mkv, gate = [wt[offs[i]:offs[i + 1]] for i in range(len(IN_SIZES))]
    zeros = lambda n: jnp.zeros((n, D), w.dtype)
    kr = mkv[KV_LORA:]
    half = QK_ROPE // 2
    kr_rot = jnp.concatenate([-kr[half:], kr[:half]], axis=0)
    tail = LANES - ROPE_LANE0 - QK_ROPE
    rows = [rx, ry, zeros(U_QKV - 2 * D_RNN), qkv, z, gate,
            a, b, zeros(W_AB - 2 * GDN_HEADS), mq,
            mkv[:KV_LORA], zeros(ROPE_LANE0), kr, zeros(tail), zeros(ROPE_LANE0), kr_rot, zeros(tail)]
    out = jnp.concatenate(rows, axis=0).astype(BF16)
    assert out.shape[0] == N_U
    return out


def _pack_w_uq(w):
    H = MLA_HEADS
    w = w.reshape(Q_LORA, H, QK_NOPE + QK_ROPE)
    nope, rope = w[..., :QK_NOPE], w[..., QK_NOPE:]
    tail = jnp.zeros((Q_LORA, H, LANES - QK_NOPE - QK_ROPE), w.dtype)
    a = jnp.concatenate([nope, rope, tail], axis=-1).reshape(Q_LORA, H * LANES)
    b = jnp.concatenate([jnp.zeros_like(nope), _rot_half(rope), tail], axis=-1).reshape(Q_LORA, H * LANES)
    return jnp.concatenate([a, b], axis=1).astype(BF16)


def _pack_w_ukv(w):
    H = MLA_HEADS
    w3 = w.reshape(KV_LORA, H, QK_NOPE + V_HEAD)
    w_uk, w_uv = w3[..., :QK_NOPE], w3[..., QK_NOPE:]
    kpad = jnp.concatenate([w_uk, jnp.zeros((KV_LORA, H, LANES - QK_NOPE), w.dtype)], axis=-1)
    wkv_p = jnp.concatenate([kpad.reshape(KV_LORA, H * LANES), w_uv.reshape(KV_LORA, H * V_HEAD)], axis=1)
    wuk_s = jnp.transpose(kpad, (1, 2, 0)).reshape(H * LANES, KV_LORA)
    wuv_c = w_uv.reshape(KV_LORA, H * V_HEAD)
    return wkv_p.astype(BF16), wuk_s.astype(BF16), wuv_c.astype(BF16)


def _rope_tables(T, pos0):
    inv = ROPE_BASE ** (-jnp.arange(0, QK_ROPE, 2, dtype=F32) / QK_ROPE)
    ang = (jnp.arange(T, dtype=F32) + pos0)[:, None] * inv[None, :]
    cos, sin = jnp.cos(ang), jnp.sin(ang)
    tail = jnp.zeros((T, LANES - ROPE_LANE0 - QK_ROPE), F32)
    cos_t = jnp.concatenate([jnp.ones((T, ROPE_LANE0), F32), cos, cos, tail], axis=1)
    sin_t = jnp.concatenate([jnp.zeros((T, ROPE_LANE0), F32), sin, sin, tail], axis=1)
    return cos_t[None], sin_t[None]


def _split_mod(mod, per_token):
    R = mod.shape[0]
    parts = jnp.split(mod, 6, axis=-1)
    return [p.reshape(1, R, D_MODEL) if per_token else p.reshape(R, 1, D_MODEL) for p in parts]


def kernel(x_prompt, x_sample, cache_ckv, cache_krope, state_rg_conv, state_rg_h, state_gdn_conv, state_gdn_S,
           page_table, c_prompt, c_sample, w_ada, b_ada, g_norm1, g_norm2, w_in, rg_conv_w, rg_conv_b, rg_wa,
           rg_ba, rg_wx, rg_bx, rg_lambda, gdn_conv_w, gdn_A_log, gdn_dt_bias, gdn_norm_g, mla_q_norm_g, w_uq,
           mla_kv_norm_g, w_ukv, w_rg_proj, w_gdn_proj, w_mla_proj, w_o, w_ffn_in, w_ffn_out, g_final):
    L = w_in.shape[0]
    Bp, T, D = x_prompt.shape
    Bd = x_sample.shape[0]
    past_len = page_table.shape[1] * PAGE_SIZE

    mod = _modulation(jnp.concatenate([c_prompt, c_sample], axis=0), w_ada, b_ada)
    cos_p, sin_p = _rope_tables(T, 0.0)
    cos_s, sin_s = _rope_tables(1, float(past_len))
    cos_s = jnp.broadcast_to(cos_s, (1, Bd, LANES))
    sin_s = jnp.broadcast_to(sin_s, (1, Bd, LANES))

    cache_krope_t = jnp.swapaxes(cache_krope, 2, 3)
    xp = x_prompt
    xs = x_sample.reshape(1, Bd, D)
    outs_p, outs_s = [], []
    for l in range(L):
        w_in_p = _pack_w_in(w_in[l])
        wq_p = _pack_w_uq(w_uq[l])
        wkv_p, wuk_s, wuv_c = _pack_w_ukv(w_ukv[l])
        wrg, wgdn, wmla, wo = (w_rg_proj[l].astype(BF16), w_gdn_proj[l].astype(BF16),
                               w_mla_proj[l].astype(BF16), w_o[l].astype(BF16))
        wfi, wfo = w_ffn_in[l].astype(BF16), w_ffn_out[l].astype(BF16)
        final = l == L - 1
        rg_w = (rg_conv_w[l], rg_conv_b[l], rg_wa[l], rg_ba[l], rg_wx[l], rg_bx[l], rg_lambda[l])
        gdn_w = (gdn_conv_w[l], gdn_A_log[l], gdn_dt_bias[l], gdn_norm_g[l])

        sh1, sc1, gt1, sh2, sc2, gt2 = _split_mod(mod[l, :Bp], per_token=False)
        u = _in_proj(xp, sc1, sh1, g_norm1[l], w_in_p, bm=min(1024, T), bn=1024)
        o_rg, rg_buf, rg_h = _rglru(u, jnp.zeros((Bp, CONV_W - 1, D_RNN), F32), jnp.zeros((Bp, D_RNN), F32),
                                    *rg_w, tt=128)
        o_gdn, gdn_buf, gdn_S = _gdn_prompt(u, *gdn_w, tt=256)
        q, k, v, ckv, kr = _mla_prep(u, cos_p, sin_p, mla_q_norm_g[l], mla_kv_norm_g[l], wq_p, wkv_p, bm=512)
        o_mla = _flash(q, k, v, tq=256)
        x1 = _merge(xp, o_rg, o_gdn, o_mla, u, gt1, wrg, wgdn, wmla, wo, bm=256)
        xp = _ffn(x1, sc2, sh2, gt2, g_norm2[l], g_final, wfi, wfo, bm=512, bf=1408, final=final)
        outs_p.append((ckv, kr, rg_buf, rg_h.reshape(Bp, D_RNN), gdn_buf, gdn_S))

        sh1, sc1, gt1, sh2, sc2, gt2 = _split_mod(mod[l, Bp:], per_token=True)
        us = _in_proj(xs, sc1, sh1, g_norm1[l], w_in_p, bm=Bd, bn=1024)
        us_seq = us.reshape(Bd, 1, N_U)
        o_rg, rg_buf, rg_h = _rglru(us_seq, state_rg_conv[l], state_rg_h[l], *rg_w, tt=1)
        o_gdn, gdn_buf, gdn_S = _gdn_step(us_seq, state_gdn_conv[l], state_gdn_S[l], *gdn_w)
        q, k, v, ckv, kr = _mla_prep(us, cos_s, sin_s, mla_q_norm_g[l], mla_kv_norm_g[l], wq_p, wkv_p, bm=Bd)
        o_mla = _sample_attention(page_table, q.reshape(Bd, MLA_HEADS * LANES), ckv.reshape(Bd, 1, KV_LORA),
                                  kr.reshape(Bd, 1, QK_ROPE), wuk_s, wuv_c, cache_ckv, cache_krope_t, l)
        x1 = _merge(xs, o_rg.reshape(1, Bd, D_RNN), o_gdn.reshape(1, Bd, GDN_VW), o_mla.reshape(1, Bd, D),
                    us, gt1, wrg, wgdn, wmla, wo, bm=Bd)
        xs = _ffn(x1, sc2, sh2, gt2, g_norm2[l], g_final, wfi, wfo, bm=Bd, bf=1408, final=final)
        outs_s.append((ckv.reshape(Bd, 1, KV_LORA), kr.reshape(Bd, 1, QK_ROPE), rg_buf, rg_h.reshape(Bd, D_RNN),
                       gdn_buf, gdn_S))

    stack = lambda outs, i: jnp.stack([o[i] for o in outs])
    return (xp, xs.reshape(Bd, 1, D),
            stack(outs_p, 0), stack(outs_p, 1), stack(outs_p, 2), stack(outs_p, 3), stack(outs_p, 4), stack(outs_p, 5),
            stack(outs_s, 0), stack(outs_s, 1), stack(outs_s, 2), stack(outs_s, 3), stack(outs_s, 4), stack(outs_s, 5))
```

```python
import functools
import math

import numpy as np
import jax
import jax.numpy as jnp
from jax import lax
from jax.experimental import pallas as pl
from jax.experimental.pallas import tpu as pltpu

F32 = jnp.float32
BF16 = jnp.bfloat16

D_MODEL = 1024
CONV_W = 4
D_RNN = 1280
RG_BLOCK = 128
RG_BLOCKS = D_RNN // RG_BLOCK
RG_C = 8.0
GDN_HEADS = 8
GDN_DK = 128
GDN_DV = 128
GDN_KW = GDN_HEADS * GDN_DK
GDN_VW = GDN_HEADS * GDN_DV
GDN_CONV_C = 2 * GDN_KW + GDN_VW
GDN_CHUNK = 64
MLA_HEADS = 8
Q_LORA = 384
KV_LORA = 256
QK_NOPE = 64
QK_ROPE = 32
V_HEAD = 128
MLA_SCALE = (QK_NOPE + QK_ROPE) ** -0.5
ROPE_BASE = 10000.0
D_FF = 2816
N_BRANCH = 3
IN_SIZES = (D_RNN, D_RNN, GDN_CONV_C, GDN_VW, GDN_HEADS, GDN_HEADS, Q_LORA, KV_LORA + QK_ROPE, N_BRANCH * D_MODEL)
EPS = 1e-6
PAGE_SIZE = 128

LANES = 128
SUBLANES = 8
VMEM_LIMIT = 56 * 1024 * 1024

U_RX = 0
U_RY = D_RNN
U_QKV = 3072
U_Z = 6144
U_GA = 7168
U_GB = 8192
U_GC = 9216
U_AB = 10240
U_MQ = 10368
U_MKV = 10752
N_U = 11264
W_AB = LANES
W_MKV = 512
ROPE_LANE0 = QK_NOPE


def _cparams(sem):
    return pltpu.CompilerParams(dimension_semantics=sem, vmem_limit_bytes=VMEM_LIMIT)


def _sigmoid(x):
    return 1.0 / (1.0 + jnp.exp(-x))


def _silu(x):
    return x * _sigmoid(x)


def _softplus(x):
    return jnp.maximum(x, 0.0) + jnp.log1p(jnp.exp(-jnp.abs(x)))


def _gelu_tanh(x):
    c = math.sqrt(2.0 / math.pi)
    return 0.5 * x * (1.0 + jnp.tanh(c * (x + 0.044715 * (x * x * x))))


def _rms(x, g):
    return x * lax.rsqrt(jnp.mean(x * x, axis=-1, keepdims=True) + EPS) * g


def _dot(a, b):
    return jnp.dot(a.astype(BF16), b.astype(BF16), preferred_element_type=F32)


def _dot_nt(a, b):
    return lax.dot_general(a.astype(BF16), b.astype(BF16), (((1,), (1,)), ((), ())), preferred_element_type=F32)


def _dot_tn(a, b):
    return lax.dot_general(a.astype(BF16), b.astype(BF16), (((0,), (0,)), ((), ())), preferred_element_type=F32)


def _split2(a):
    hi = a.astype(BF16)
    lo = (a - hi.astype(F32)).astype(BF16)
    return hi, lo


def _dot3(a, b):
    ah, al = _split2(a)
    bh, bl = _split2(b)
    d = functools.partial(jnp.dot, preferred_element_type=F32)
    return d(ah, bh) + (d(ah, bl) + d(al, bh))


def _dot_exact_lhs(a_bf16, b):
    b0 = b.astype(BF16)
    r1 = b - b0.astype(F32)
    b1 = r1.astype(BF16)
    b2 = (r1 - b1.astype(F32)).astype(BF16)
    d = functools.partial(jnp.dot, preferred_element_type=F32)
    return d(a_bf16, b0) + (d(a_bf16, b1) + d(a_bf16, b2))


def _mod_kernel(c_ref, w_ref, b_ref, o_ref):
    c = c_ref[...]
    o_ref[0] = _dot(_silu(c), w_ref[0]) + b_ref[0]


def _modulation(c_all, w_ada, b_ada):
    L, D, N = w_ada.shape
    R = c_all.shape[0]
    bn = 1536
    return pl.pallas_call(
        _mod_kernel,
        grid=(L, N // bn),
        in_specs=[pl.BlockSpec((R, D), lambda l, j: (0, 0)),
                  pl.BlockSpec((1, D, bn), lambda l, j: (l, 0, j)),
                  pl.BlockSpec((1, 1, bn), lambda l, j: (l, 0, j))],
        out_specs=pl.BlockSpec((1, R, bn), lambda l, j: (l, 0, j)),
        out_shape=jax.ShapeDtypeStruct((L, R, N), F32),
        name="adaln_mod",
        compiler_params=_cparams(("parallel", "parallel")),
    )(c_all, w_ada, b_ada.reshape(L, 1, N))


def _mod_spec(mod, bm):
    _, tm, d = mod.shape
    if tm == 1:
        return pl.BlockSpec((1, 1, d), lambda b, i, *_: (b, 0, 0))
    return pl.BlockSpec((1, bm, d), lambda b, i, *_: (b, i, 0))


def _in_kernel(x_ref, sc_ref, sh_ref, g_ref, w_ref, o_ref, hb_ref):
    @pl.when(pl.program_id(2) == 0)
    def _():
        h = _rms(x_ref[0], g_ref[...]) * (1.0 + sc_ref[0]) + sh_ref[0]
        hb_ref[...] = h.astype(BF16)

    o_ref[0] = lax.dot_general(hb_ref[...], w_ref[...], (((1,), (1,)), ((), ())), preferred_element_type=F32)


def _in_proj(x, sc, sh, g, w_packed, bm, bn):
    B, T, D = x.shape
    N = w_packed.shape[0]
    return pl.pallas_call(
        _in_kernel,
        grid=(B, T // bm, N // bn),
        in_specs=[pl.BlockSpec((1, bm, D), lambda b, i, j: (b, i, 0)),
                  _mod_spec(sc, bm), _mod_spec(sh, bm),
                  pl.BlockSpec((1, D), lambda b, i, j: (0, 0)),
                  pl.BlockSpec((bn, D), lambda b, i, j: (j, 0))],
        out_specs=pl.BlockSpec((1, bm, bn), lambda b, i, j: (b, i, j)),
        out_shape=jax.ShapeDtypeStruct((B, T, N), F32),
        scratch_shapes=[pltpu.VMEM((bm, D), BF16)],
        name="in_proj",
        compiler_params=_cparams(("parallel", "parallel", "arbitrary")),
    )(x, sc, sh, g.reshape(1, D), w_packed)


SCAN_PAD = 64


def _rg_kernel(ux_ref, uy_ref, buf_ref, h0_ref, cw_ref, cb_ref, wa_ref, ba_ref, wx_ref, bx_ref, lam_ref,
               o_ref, nbuf_ref, hl_ref, xbuf, abuf, bbuf, hc, *, tt):
    i = pl.program_id(1)
    nt = pl.num_programs(1)

    @pl.when(i == 0)
    def _():
        xbuf[0:SUBLANES, :] = buf_ref[0]
        hc[...] = h0_ref[0]
        abuf[0:SCAN_PAD, :] = jnp.ones((SCAN_PAD, D_RNN), F32)
        bbuf[0:SCAN_PAD, :] = jnp.zeros((SCAN_PAD, D_RNN), F32)

    xbuf[SUBLANES:SUBLANES + tt, :] = ux_ref[0]
    lo = SUBLANES - (CONV_W - 1)
    if tt >= SUBLANES:
        xs = xbuf[...]
        xc = cb_ref[...] + cw_ref[CONV_W - 1:CONV_W, :] * xs[SUBLANES:, :]
        for j in range(CONV_W - 1):
            xc = xc + cw_ref[j:j + 1, :] * pltpu.roll(xs, CONV_W - 1 - j, axis=0)[SUBLANES:, :]
    else:
        xc = cb_ref[...] + cw_ref[0:1, :] * xbuf[lo:lo + tt, :]
        for j in range(1, CONV_W):
            xc = xc + cw_ref[j:j + 1, :] * xbuf[lo + j:lo + j + tt, :]

    @pl.when(i == nt - 1)
    def _():
        nbuf_ref[0] = xbuf[tt + lo:tt + SUBLANES, :]

    if tt >= SUBLANES:
        xbuf[0:SUBLANES, :] = xbuf[tt:tt + SUBLANES, :]

    xb = xc.astype(BF16)
    ra = jnp.concatenate([jnp.dot(xb[:, n * RG_BLOCK:(n + 1) * RG_BLOCK], wa_ref[n], preferred_element_type=F32)
                          for n in range(RG_BLOCKS)], axis=1)
    ri = jnp.concatenate([jnp.dot(xb[:, n * RG_BLOCK:(n + 1) * RG_BLOCK], wx_ref[n], preferred_element_type=F32)
                          for n in range(RG_BLOCKS)], axis=1)
    r = _sigmoid(ra + ba_ref[...])
    ig = _sigmoid(ri + bx_ref[...])
    log_a = (-RG_C) * r * _softplus(-lam_ref[...])
    a = jnp.exp(log_a)
    b = jnp.sqrt(-jnp.tanh(log_a) * (a * a + 1.0)) * (ig * xc)

    abuf[SCAN_PAD:SCAN_PAD + tt, :] = a
    bbuf[SCAN_PAD:SCAN_PAD + tt, :] = b
    d = 1
    while d < tt:
        a_s = abuf[SCAN_PAD - d:SCAN_PAD - d + tt, :]
        b_s = bbuf[SCAN_PAD - d:SCAN_PAD - d + tt, :]
        a0 = abuf[SCAN_PAD:SCAN_PAD + tt, :]
        b0 = bbuf[SCAN_PAD:SCAN_PAD + tt, :]
        abuf[SCAN_PAD:SCAN_PAD + tt, :] = a0 * a_s
        bbuf[SCAN_PAD:SCAN_PAD + tt, :] = a0 * b_s + b0
        d *= 2
    h = bbuf[SCAN_PAD:SCAN_PAD + tt, :] + abuf[SCAN_PAD:SCAN_PAD + tt, :] * hc[...]
    hc[...] = h[tt - 1:tt, :]
    o_ref[0] = (h * _gelu_tanh(uy_ref[0])).astype(BF16)

    @pl.when(i == nt - 1)
    def _():
        hl_ref[0] = h[tt - 1:tt, :]


def _rglru(u, buf, h0, cw, cb, wa, ba, wx, bx, lam, tt):
    B, T, _ = u.shape
    C = D_RNN
    buf8 = jnp.pad(buf, ((0, 0), (SUBLANES - (CONV_W - 1), 0), (0, 0)))
    vec = lambda v: v.reshape(1, C)
    const2 = lambda shape: pl.BlockSpec(shape, lambda b, i: (0,) * len(shape))
    return pl.pallas_call(
        functools.partial(_rg_kernel, tt=tt),
        grid=(B, T // tt),
        in_specs=[pl.BlockSpec((1, tt, C), lambda b, i: (b, i, U_RX // C)),
                  pl.BlockSpec((1, tt, C), lambda b, i: (b, i, U_RY // C)),
                  pl.BlockSpec((1, SUBLANES, C), lambda b, i: (b, 0, 0)),
                  pl.BlockSpec((1, 1, C), lambda b, i: (b, 0, 0)),
                  const2((CONV_W, C)), const2((1, C)),
                  const2((RG_BLOCKS, RG_BLOCK, RG_BLOCK)), const2((1, C)),
                  const2((RG_BLOCKS, RG_BLOCK, RG_BLOCK)), const2((1, C)), const2((1, C))],
        out_specs=[pl.BlockSpec((1, tt, C), lambda b, i: (b, i, 0)),
                   pl.BlockSpec((1, CONV_W - 1, C), lambda b, i: (b, 0, 0)),
                   pl.BlockSpec((1, 1, C), lambda b, i: (b, 0, 0))],
        out_shape=[jax.ShapeDtypeStruct((B, T, C), BF16),
                   jax.ShapeDtypeStruct((B, CONV_W - 1, C), F32),
                   jax.ShapeDtypeStruct((B, 1, C), F32)],
        scratch_shapes=[pltpu.VMEM((tt + SUBLANES, C), F32),
                        pltpu.VMEM((SCAN_PAD + tt, C), F32),
                        pltpu.VMEM((SCAN_PAD + tt, C), F32),
                        pltpu.VMEM((1, C), F32)],
        name="rglru",
        compiler_params=_cparams(("parallel", "arbitrary")),
    )(u, u, buf8, h0.reshape(B, 1, C), cw, vec(cb), wa.astype(BF16), vec(ba), wx.astype(BF16), vec(bx), vec(lam))


TRI_BASE = 8
INV_GROUP = 8
NEWTON_STEPS = 2


def _cat_dot3(x, y, bd_b, n):
    C = x.shape[0]
    xh, xl = _split2(x)
    yh, yl = _split2(y)
    d = functools.partial(jnp.dot, preferred_element_type=F32)
    r = d(jnp.concatenate([xh, xl], axis=0), jnp.concatenate([yh] * n, axis=0) * bd_b)
    return r[:C] + (r[C:] + d(xh, jnp.concatenate([yl] * n, axis=0) * bd_b))


def _cat_dot1(x, y, bd_b, n):
    yb = jnp.concatenate([y.astype(BF16)] * n, axis=0) * bd_b
    return jnp.dot(x.astype(BF16), yb, preferred_element_type=F32)


def _tri_inv_cat(Lcs, masks, bd_b, n):
    eye, base_mask, level_masks = masks
    mm = functools.partial(_cat_dot1, bd_b=bd_b, n=n)
    mm3 = functools.partial(_cat_dot3, bd_b=bd_b, n=n)
    N = [jnp.where(base_mask, -Lc, 0.0) for Lc in Lcs]
    P = [eye + x for x in N]
    N2 = [mm(x, x) for x in N]
    P = [p + mm(p, x) for p, x in zip(P, N2)]
    N4 = [mm(x, x) for x in N2]
    P = [p + mm(p, x) for p, x in zip(P, N4)]
    for m in level_masks:
        PO = [mm(p, jnp.where(m, Lc, 0.0)) for p, Lc in zip(P, Lcs)]
        P = [p - mm(po, p) for p, po in zip(P, PO)]
    for _ in range(NEWTON_STEPS):
        R = [eye - p - mm3(Lc, p) for p, Lc in zip(P, Lcs)]
        P = [p + mm(p, r) for p, r in zip(P, R)]
    return P


def _cat_masks(C, n):
    rr = lax.broadcasted_iota(jnp.int32, (C, n * C), 0)
    jj = lax.broadcasted_iota(jnp.int32, (C, n * C), 1) % C
    same = lambda s: (rr // s) == (jj // s)
    eye = (rr == jj).astype(F32)
    levels = []
    s = TRI_BASE
    while s < C:
        levels.append(jnp.logical_and(same(2 * s), jnp.logical_not(same(s))))
        s *= 2
    return eye, same(TRI_BASE), levels


def _gdn_kernel(qkv_ref, z_ref, ab_ref, buf_ref, cw_ref, alog_ref, dtb_ref, ng_ref,
                o_ref, nbuf_ref, S_ref, xbuf, act, S_sc, wq_sc, uv_sc, qk_sc, kd_sc, gl_sc, lc_sc, rhs_sc, *, tt):
    i = pl.program_id(1)
    nt = pl.num_programs(1)
    C = GDN_CHUNK
    H = GDN_HEADS
    n = tt // C

    @pl.when(i == 0)
    def _():
        xbuf[0:SUBLANES, :] = buf_ref[0]
        S_sc[...] = jnp.zeros(S_sc.shape, F32)

    xbuf[SUBLANES:SUBLANES + tt, :] = qkv_ref[0]
    lo = SUBLANES - (CONV_W - 1)
    xs = xbuf[...]
    y = cw_ref[CONV_W - 1:CONV_W, :] * xs[SUBLANES:, :]
    for j in range(CONV_W - 1):
        y = y + cw_ref[j:j + 1, :] * pltpu.roll(xs, CONV_W - 1 - j, axis=0)[SUBLANES:, :]
    act[...] = _silu(y)

    @pl.when(i == nt - 1)
    def _():
        nbuf_ref[0] = xbuf[tt + lo:tt + SUBLANES, :]

    xbuf[0:SUBLANES, :] = xbuf[tt:tt + SUBLANES, :]

    ab = ab_ref[0]
    g = -jnp.exp(alog_ref[...]) * _softplus(ab + dtb_ref[...])
    beta = _sigmoid(ab)

    ri = lax.broadcasted_iota(jnp.int32, (tt, tt), 0)
    ci = lax.broadcasted_iota(jnp.int32, (tt, tt), 1)
    bd = (ri // C) == (ci // C)
    bd_incl = jnp.logical_and(bd, ri >= ci)
    bd_strict = jnp.logical_and(bd, ri > ci)
    bd_b = bd.astype(BF16)
    last_b = jnp.logical_and(bd, ci % C == C - 1).astype(BF16)
    gcum = _dot_exact_lhs(bd_incl.astype(BF16), g)
    glast = _dot_exact_lhs(last_b, gcum)
    gl_sc[...] = jnp.exp(glast)
    gcum_t = gcum.T
    masks = _cat_masks(C, n)
    ng = ng_ref[...]

    for h in range(H):
        q = act[:, h * GDN_DK:(h + 1) * GDN_DK]
        k = act[:, GDN_KW + h * GDN_DK:GDN_KW + (h + 1) * GDN_DK]
        v = act[:, 2 * GDN_KW + h * GDN_DV:2 * GDN_KW + (h + 1) * GDN_DV]
        q = q * lax.rsqrt(jnp.sum(q * q, axis=-1, keepdims=True) + EPS) * (GDN_DK ** -0.5)
        k = k * lax.rsqrt(jnp.sum(k * k, axis=-1, keepdims=True) + EPS)
        gc = gcum[:, h:h + 1]
        gr = gcum_t[h:h + 1, :]
        e = jnp.exp(jnp.where(bd_incl, gc - gr, -jnp.inf))
        bt = beta[:, H + h:H + h + 1]
        kb = k.astype(BF16)
        qb = q.astype(BF16)
        Lf = jnp.where(bd_strict, bt * _dot_nt(kb, kb) * e, 0.0)
        qkd = (_dot_nt(qb, kb) * e).astype(BF16)
        for c in range(n):
            qk_sc[h, c] = qkd[c * C:(c + 1) * C, c * C:(c + 1) * C]
        Lc = Lf[0:C]
        for c in range(1, n):
            Lc = Lc + Lf[c * C:(c + 1) * C]
        lc_sc[h] = Lc
        eg = jnp.exp(gc)
        rhs_sc[h, :, 0:GDN_DV] = bt * v
        rhs_sc[h, :, GDN_DV:] = (bt * eg) * k
        qe = (eg * q).astype(BF16)
        for c in range(n):
            wq_sc[h, c, C:2 * C, :] = qe[c * C:(c + 1) * C]
        kd_sc[h] = (k * jnp.exp(glast[:, h:h + 1] - gc)).astype(BF16)

    for h0 in range(0, H, INV_GROUP):
        hs = range(h0, h0 + INV_GROUP)
        t_cats = _tri_inv_cat([lc_sc[h] for h in hs], masks, bd_b, n)
        for h, t_cat in zip(hs, t_cats):
            t_bd = jnp.where(bd, jnp.concatenate([t_cat] * n, axis=0), 0.0)
            uw = _dot3(t_bd, rhs_sc[h])
            uv_sc[h] = uw[:, :GDN_DV]
            w = uw[:, GDN_DV:].astype(BF16)
            for c in range(n):
                wq_sc[h, c, 0:C, :] = w[c * C:(c + 1) * C]

    def chunk(c, carry):
        r0 = pl.multiple_of(c * C, C)
        rows = pl.ds(r0, C)
        gl = gl_sc[pl.ds(r0, 1), :]
        S = [S_sc[h] for h in range(H)]
        wq = [jnp.dot(wq_sc[h, c], S[h].astype(BF16), preferred_element_type=F32) for h in range(H)]
        Ub = [(uv_sc[h, rows, :] - wq[h][:C]).astype(BF16) for h in range(H)]
        o = [wq[h][C:] + jnp.dot(qk_sc[h, c], Ub[h], preferred_element_type=F32) for h in range(H)]
        for h in range(H):
            S_sc[h] = gl[:, h:h + 1] * S[h] + _dot_tn(kd_sc[h, rows, :], Ub[h])
        for h in range(H):
            zz = z_ref[0, rows, h * GDN_DV:(h + 1) * GDN_DV]
            o_ref[0, rows, h * GDN_DV:(h + 1) * GDN_DV] = (_rms(o[h], ng) * _silu(zz)).astype(BF16)
        return carry

    lax.fori_loop(0, n, chunk, 0)

    @pl.when(i == nt - 1)
    def _():
        S_ref[0] = S_sc[...]


def _gdn_prompt(u, cw, alog, dtb, ng, tt):
    B, T, _ = u.shape
    Cc = GDN_CONV_C
    H = GDN_HEADS
    buf8 = jnp.zeros((B, SUBLANES, Cc), F32)
    lane_vec = lambda v: jnp.pad(v, (0, LANES - v.shape[0])).reshape(1, LANES)
    const2 = lambda shape: pl.BlockSpec(shape, lambda b, i: (0,) * len(shape))
    return pl.pallas_call(
        functools.partial(_gdn_kernel, tt=tt),
        grid=(B, T // tt),
        in_specs=[pl.BlockSpec((1, tt, Cc), lambda b, i: (b, i, U_QKV // Cc)),
                  pl.BlockSpec((1, tt, GDN_VW), lambda b, i: (b, i, U_Z // GDN_VW)),
                  pl.BlockSpec((1, tt, W_AB), lambda b, i: (b, i, U_AB // W_AB)),
                  pl.BlockSpec((1, SUBLANES, Cc), lambda b, i: (b, 0, 0)),
                  const2((CONV_W, Cc)), const2((1, LANES)), const2((1, LANES)), const2((1, GDN_DV))],
        out_specs=[pl.BlockSpec((1, tt, GDN_VW), lambda b, i: (b, i, 0)),
                   pl.BlockSpec((1, CONV_W - 1, Cc), lambda b, i: (b, 0, 0)),
                   pl.BlockSpec((1, H, GDN_DK, GDN_DV), lambda b, i: (b, 0, 0, 0))],
        out_shape=[jax.ShapeDtypeStruct((B, T, GDN_VW), BF16),
                   jax.ShapeDtypeStruct((B, CONV_W - 1, Cc), F32),
                   jax.ShapeDtypeStruct((B, H, GDN_DK, GDN_DV), F32)],
        scratch_shapes=[pltpu.VMEM((tt + SUBLANES, Cc), F32),
                        pltpu.VMEM((tt, Cc), F32),
                        pltpu.VMEM((H, GDN_DK, GDN_DV), F32),
                        pltpu.VMEM((H, tt // GDN_CHUNK, 2 * GDN_CHUNK, GDN_DK), BF16),
                        pltpu.VMEM((H, tt, GDN_DV), F32),
                        pltpu.VMEM((H, tt // GDN_CHUNK, GDN_CHUNK, GDN_CHUNK), BF16),
                        pltpu.VMEM((H, tt, GDN_DK), BF16),
                        pltpu.VMEM((tt, LANES), F32),
                        pltpu.VMEM((H, GDN_CHUNK, tt), F32),
                        pltpu.VMEM((H, tt, GDN_DV + GDN_DK), F32)],
        name="gdn_chunked",
        compiler_params=_cparams(("parallel", "arbitrary")),
    )(u, u, u, buf8, cw, lane_vec(alog), lane_vec(dtb), ng.reshape(1, GDN_DV))


def _gdn_step_kernel(qkv_ref, z_ref, ab_ref, buf_ref, S0_ref, cw_ref, alog_ref, dtb_ref, ng_ref,
                     o_ref, nbuf_ref, S_ref, xbuf):
    H = GDN_HEADS
    xbuf[0:SUBLANES, :] = buf_ref[0]
    xbuf[SUBLANES:SUBLANES + 1, :] = qkv_ref[0]
    lo = SUBLANES - (CONV_W - 1)
    y = cw_ref[0:1, :] * xbuf[lo:lo + 1, :]
    for j in range(1, CONV_W):
        y = y + cw_ref[j:j + 1, :] * xbuf[lo + j:lo + j + 1, :]
    y = _silu(y)
    nbuf_ref[0] = xbuf[lo + 1:SUBLANES + 1, :]

    ab = ab_ref[0]
    g_all = -jnp.exp(alog_ref[...]) * _softplus(ab + dtb_ref[...])
    beta_all = _sigmoid(ab)
    ii = lax.broadcasted_iota(jnp.int32, (GDN_DK, GDN_DK), 0)
    jj = lax.broadcasted_iota(jnp.int32, (GDN_DK, GDN_DK), 1)
    eye = ii == jj
    ng = ng_ref[...]
    for h in range(H):
        q = y[:, h * GDN_DK:(h + 1) * GDN_DK]
        k = y[:, GDN_KW + h * GDN_DK:GDN_KW + (h + 1) * GDN_DK]
        v = y[:, 2 * GDN_KW + h * GDN_DV:2 * GDN_KW + (h + 1) * GDN_DV]
        q = q * lax.rsqrt(jnp.sum(q * q, axis=-1, keepdims=True) + EPS) * (GDN_DK ** -0.5)
        k = k * lax.rsqrt(jnp.sum(k * k, axis=-1, keepdims=True) + EPS)
        eg = jnp.exp(g_all[:, h:h + 1])
        bt = beta_all[:, H + h:H + h + 1]
        S = S0_ref[0, h]
        Sb = S.astype(BF16)
        kb = k.astype(BF16)
        qb = q.astype(BF16)
        u = bt * (v - eg * jnp.dot(kb, Sb, preferred_element_type=F32))
        ub = u.astype(BF16)
        qk = jnp.sum(qb.astype(F32) * kb.astype(F32), axis=-1, keepdims=True)
        o = eg * jnp.dot(qb, Sb, preferred_element_type=F32) + qk.astype(BF16).astype(F32) * ub.astype(F32)
        kdiag = jnp.where(eye, jnp.broadcast_to(kb.astype(F32), (GDN_DK, GDN_DK)), 0.0).astype(BF16)
        urows = jnp.broadcast_to(ub, (GDN_DK, GDN_DV))
        S_ref[0, h] = eg * S + jnp.dot(kdiag, urows, preferred_element_type=F32)
        zz = z_ref[0, :, h * GDN_DV:(h + 1) * GDN_DV]
        o_ref[0, :, h * GDN_DV:(h + 1) * GDN_DV] = (_rms(o, ng) * _silu(zz)).astype(BF16)


def _gdn_step(u, buf, S0, cw, alog, dtb, ng):
    B = u.shape[0]
    Cc = GDN_CONV_C
    H = GDN_HEADS
    buf8 = jnp.pad(buf, ((0, 0), (SUBLANES - (CONV_W - 1), 0), (0, 0)))
    lane_vec = lambda v: jnp.pad(v, (0, LANES - v.shape[0])).reshape(1, LANES)
    const1 = lambda shape: pl.BlockSpec(shape, lambda b: (0,) * len(shape))
    return pl.pallas_call(
        _gdn_step_kernel,
        grid=(B,),
        in_specs=[pl.BlockSpec((1, 1, Cc), lambda b: (b, 0, U_QKV // Cc)),
                  pl.BlockSpec((1, 1, GDN_VW), lambda b: (b, 0, U_Z // GDN_VW)),
                  pl.BlockSpec((1, 1, W_AB), lambda b: (b, 0, U_AB // W_AB)),
                  pl.BlockSpec((1, SUBLANES, Cc), lambda b: (b, 0, 0)),
                  pl.BlockSpec((1, H, GDN_DK, GDN_DV), lambda b: (b, 0, 0, 0)),
                  const1((CONV_W, Cc)), const1((1, LANES)), const1((1, LANES)), const1((1, GDN_DV))],
        out_specs=[pl.BlockSpec((1, 1, GDN_VW), lambda b: (b, 0, 0)),
                   pl.BlockSpec((1, CONV_W - 1, Cc), lambda b: (b, 0, 0)),
                   pl.BlockSpec((1, H, GDN_DK, GDN_DV), lambda b: (b, 0, 0, 0))],
        out_shape=[jax.ShapeDtypeStruct((B, 1, GDN_VW), BF16),
                   jax.ShapeDtypeStruct((B, CONV_W - 1, Cc), F32),
                   jax.ShapeDtypeStruct((B, H, GDN_DK, GDN_DV), F32)],
        scratch_shapes=[pltpu.VMEM((2 * SUBLANES, Cc), F32)],
        name="gdn_step",
        compiler_params=_cparams(("parallel",)),
    )(u, u, u, buf8, S0, cw, lane_vec(alog), lane_vec(dtb), ng.reshape(1, GDN_DV))


def _mla_prep_kernel(mq_ref, mkv_ref, cos_ref, sin_ref, gq_ref, gkv_ref, wq_ref, wkv_ref,
                     q_ref, k_ref, v_ref, ckv_ref, kr_ref):
    H = MLA_HEADS
    W = LANES * H
    cos = cos_ref[0]
    sin = sin_ref[0]
    cq = _rms(mq_ref[0], gq_ref[...])
    qa = _dot(cq, wq_ref[...])
    mkv = mkv_ref[0]
    ckv = _rms(mkv[:, :KV_LORA], gkv_ref[...])
    ckv_ref[0] = ckv
    kro = mkv[:, KV_LORA:KV_LORA + LANES] * cos + mkv[:, KV_LORA + LANES:KV_LORA + 2 * LANES] * sin
    kr_ref[0] = kro[:, ROPE_LANE0:ROPE_LANE0 + QK_ROPE]
    kv = _dot(ckv, wkv_ref[...])
    for h in range(H):
        sl = slice(h * LANES, (h + 1) * LANES)
        qh = qa[:, sl] * cos + qa[:, W + h * LANES:W + (h + 1) * LANES] * sin
        q_ref[0, :, sl] = (qh * MLA_SCALE).astype(BF16)
        k_ref[0, :, sl] = (kv[:, sl] + kro).astype(BF16)
    v_ref[0] = kv[:, W:].astype(BF16)


def _mla_prep(u, cos, sin, gq, gkv, wq_p, wkv_p, bm):
    B, T, _ = u.shape
    W = LANES * MLA_HEADS
    const2 = lambda shape: pl.BlockSpec(shape, lambda b, i: (0,) * len(shape))
    return pl.pallas_call(
        _mla_prep_kernel,
        grid=(B, T // bm),
        in_specs=[pl.BlockSpec((1, bm, Q_LORA), lambda b, i: (b, i, U_MQ // Q_LORA)),
                  pl.BlockSpec((1, bm, W_MKV), lambda b, i: (b, i, U_MKV // W_MKV)),
                  pl.BlockSpec((1, bm, LANES), lambda b, i: (0, i, 0)),
                  pl.BlockSpec((1, bm, LANES), lambda b, i: (0, i, 0)),
                  const2((1, Q_LORA)), const2((1, KV_LORA)),
                  const2((Q_LORA, 2 * W)), const2((KV_LORA, 2 * W))],
        out_specs=[pl.BlockSpec((1, bm, W), lambda b, i: (b, i, 0)),
                   pl.BlockSpec((1, bm, W), lambda b, i: (b, i, 0)),
                   pl.BlockSpec((1, bm, W), lambda b, i: (b, i, 0)),
                   pl.BlockSpec((1, bm, KV_LORA), lambda b, i: (b, i, 0)),
                   pl.BlockSpec((1, bm, QK_ROPE), lambda b, i: (b, i, 0))],
        out_shape=[jax.ShapeDtypeStruct((B, T, W), BF16),
                   jax.ShapeDtypeStruct((B, T, W), BF16),
                   jax.ShapeDtypeStruct((B, T, W), BF16),
                   jax.ShapeDtypeStruct((B, T, KV_LORA), F32),
                   jax.ShapeDtypeStruct((B, T, QK_ROPE), F32)],
        name="mla_prep",
        compiler_params=_cparams(("parallel", "parallel")),
    )(u, u, cos, sin, gq.reshape(1, Q_LORA), gkv.reshape(1, KV_LORA), wq_p, wkv_p)


NEG_BIG = -1e30


FLASH_HEADS = 2
FLASH_TK = 512


def _flash_kernel(q_ref, k_ref, v_ref, o_ref, *, tq):
    qi = pl.program_id(2)
    hp = FLASH_HEADS
    heads = [slice(h * LANES, (h + 1) * LANES) for h in range(hp)]
    qs = [q_ref[0, :, sl] for sl in heads]
    tk = min(FLASH_TK, tq)
    n_full = (qi * tq) // tk

    def step(j, carry, diagonal):
        ks = pl.ds(pl.multiple_of(j * tk, tk), tk)
        if diagonal:
            row = qi * tq + lax.broadcasted_iota(jnp.int32, (tq, tk), 0)
            col = j * tk + lax.broadcasted_iota(jnp.int32, (tq, tk), 1)
            causal = col <= row
        out = []
        for h, sl in enumerate(heads):
            m, l, acc = carry[h]
            s = lax.dot_general(qs[h], k_ref[0, ks, sl], (((1,), (1,)), ((), ())), preferred_element_type=F32)
            if diagonal:
                s = jnp.where(causal, s, NEG_BIG)
            m_new = jnp.maximum(m, jnp.max(s, axis=-1, keepdims=True))
            alpha = jnp.exp(m - m_new)
            p = jnp.exp(s - m_new)
            l = alpha * l + jnp.sum(p, axis=-1, keepdims=True)
            acc = alpha * acc + jnp.dot(p.astype(BF16), v_ref[0, ks, sl], preferred_element_type=F32)
            out.append((m_new, l, acc))
        return tuple(out)

    init = tuple((jnp.full((tq, 1), NEG_BIG, F32), jnp.zeros((tq, 1), F32), jnp.zeros((tq, LANES), F32))
                 for _ in heads)
    carry = lax.fori_loop(0, n_full, lambda j, c: step(j, c, False), init)
    for t in range(tq // tk):
        carry = step(n_full + t, carry, True)
    for h, sl in enumerate(heads):
        _, l, acc = carry[h]
        o_ref[0, :, sl] = (acc / l).astype(BF16)


def _flash(q, k, v, tq):
    B, T, W = q.shape
    wb = FLASH_HEADS * LANES
    H = W // wb
    return pl.pallas_call(
        functools.partial(_flash_kernel, tq=tq),
        grid=(B, H, T // tq),
        in_specs=[pl.BlockSpec((1, tq, wb), lambda b, h, i: (b, i, h)),
                  pl.BlockSpec((1, T, wb), lambda b, h, i: (b, 0, h)),
                  pl.BlockSpec((1, T, wb), lambda b, h, i: (b, 0, h))],
        out_specs=pl.BlockSpec((1, tq, wb), lambda b, h, i: (b, i, h)),
        out_shape=jax.ShapeDtypeStruct((B, T, W), BF16),
        name="mla_flash",
        compiler_params=_cparams(("parallel", "parallel", "arbitrary")),
    )(q, k, v)


DEC_GROUP = 16
DEC_SLOTS = 2


def _sattn_kernel(pt_ref, q_ref, cn_ref, krn_ref, wuk_ref, wuv_ref, ckv_hbm, krt_hbm, o_ref,
                  ck_buf, kr_buf, sem, *, layer, n_pages):
    H = MLA_HEADS
    G = DEC_GROUP
    n_groups = n_pages // G
    b = pl.program_id(0)
    nb = pl.num_programs(0)

    def group_copies(seq, g, slot):
        out = []
        for k in range(G):
            page = pt_ref[seq, g * G + k]
            out.append(pltpu.make_async_copy(ckv_hbm.at[layer, page],
                                             ck_buf.at[slot, pl.ds(k * PAGE_SIZE, PAGE_SIZE), :], sem.at[0, slot]))
            out.append(pltpu.make_async_copy(krt_hbm.at[layer, page],
                                             kr_buf.at[slot, :, pl.ds(k * PAGE_SIZE, PAGE_SIZE)], sem.at[1, slot]))
        return out

    @pl.when(b == 0)
    def _():
        for c in group_copies(0, 0, 0):
            c.start()

    q = q_ref[0]
    q_rope = q[:, ROPE_LANE0:ROPE_LANE0 + QK_ROPE]
    rr = lax.broadcasted_iota(jnp.int32, (H, H * LANES), 0)
    cc = lax.broadcasted_iota(jnp.int32, (H, H * LANES), 1)
    q_bd = jnp.where(cc // LANES == rr, jnp.concatenate([q] * H, axis=1), jnp.zeros((), BF16))
    qlat = jnp.dot(q_bd, wuk_ref[...], preferred_element_type=F32).astype(BF16)

    def body(g, carry):
        m, l, acc = carry
        slot = g % DEC_SLOTS
        nxt = (g + 1) % DEC_SLOTS

        @pl.when(g + 1 < n_groups)
        def _():
            for c in group_copies(b, g + 1, nxt):
                c.start()

        @pl.when(jnp.logical_and(g + 1 == n_groups, b + 1 < nb))
        def _():
            for c in group_copies(b + 1, 0, nxt):
                c.start()

        for c in group_copies(b, g, slot):
            c.wait()
        ck = ck_buf[slot].astype(BF16)
        kr_t = kr_buf[slot].astype(BF16)
        s = _dot_nt(qlat, ck) + jnp.dot(q_rope, kr_t, preferred_element_type=F32)
        m_new = jnp.maximum(m, jnp.max(s, axis=-1, keepdims=True))
        alpha = jnp.exp(m - m_new)
        p = jnp.exp(s - m_new)
        l = alpha * l + jnp.sum(p, axis=-1, keepdims=True)
        acc = alpha * acc + jnp.dot(p.astype(BF16), ck, preferred_element_type=F32)
        return m_new, l, acc

    init = (jnp.full((H, 1), NEG_BIG, F32), jnp.zeros((H, 1), F32), jnp.zeros((H, KV_LORA), F32))
    m, l, acc = lax.fori_loop(0, n_groups, body, init)

    cn = cn_ref[0].astype(BF16).astype(F32)
    krn = krn_ref[0].astype(BF16).astype(F32)
    s_n = (jnp.sum(qlat.astype(F32) * cn, axis=-1, keepdims=True)
           + jnp.sum(q_rope.astype(F32) * krn, axis=-1, keepdims=True))
    m_f = jnp.maximum(m, s_n)
    a_f = jnp.exp(m - m_f)
    p_n = jnp.exp(s_n - m_f)
    l_f = a_f * l + p_n
    o_lat = (a_f * acc + p_n.astype(BF16).astype(F32) * cn) / l_f
    o_all = jnp.dot(o_lat.astype(BF16), wuv_ref[...], preferred_element_type=F32)
    o_ref[0] = jnp.sum(jnp.where(cc // LANES == rr, o_all, 0.0), axis=0, keepdims=True).astype(BF16)


def _sample_attention(page_table, q, ckv_new, kr_new, wuk_s, wuv_c, cache_ckv, cache_krope_t, layer):
    Bd = q.shape[0]
    H = MLA_HEADS
    n_pages = page_table.shape[1]
    assert n_pages % (DEC_GROUP * DEC_SLOTS) == 0
    q3 = q.reshape(Bd, H, LANES)
    const = lambda shape: pl.BlockSpec(shape, lambda b, pt: (0,) * len(shape))
    grid_spec = pltpu.PrefetchScalarGridSpec(
        num_scalar_prefetch=1,
        grid=(Bd,),
        in_specs=[pl.BlockSpec((1, H, LANES), lambda b, pt: (b, 0, 0)),
                  pl.BlockSpec((1, 1, KV_LORA), lambda b, pt: (b, 0, 0)),
                  pl.BlockSpec((1, 1, QK_ROPE), lambda b, pt: (b, 0, 0)),
                  const((H * LANES, KV_LORA)), const((KV_LORA, H * V_HEAD)),
                  pl.BlockSpec(memory_space=pl.ANY), pl.BlockSpec(memory_space=pl.ANY)],
        out_specs=pl.BlockSpec((1, 1, H * V_HEAD), lambda b, pt: (b, 0, 0)),
        scratch_shapes=[pltpu.VMEM((DEC_SLOTS, DEC_GROUP * PAGE_SIZE, KV_LORA), F32),
                        pltpu.VMEM((DEC_SLOTS, QK_ROPE, DEC_GROUP * PAGE_SIZE), F32),
                        pltpu.SemaphoreType.DMA((2, DEC_SLOTS))],
    )
    return pl.pallas_call(
        functools.partial(_sattn_kernel, layer=layer, n_pages=n_pages),
        grid_spec=grid_spec,
        out_shape=jax.ShapeDtypeStruct((Bd, 1, H * V_HEAD), BF16),
        name="mla_decode",
        compiler_params=_cparams(("arbitrary",)),
    )(page_table, q3, ckv_new, kr_new, wuk_s, wuv_c, cache_ckv, cache_krope_t)


def _merge_kernel(x_ref, org_ref, ogdn_ref, omla_ref, ga_ref, gb_ref, gc_ref, gt_ref,
                  wrg_ref, wgdn_ref, wmla_ref, wo_ref, o_ref):
    d = functools.partial(jnp.dot, preferred_element_type=F32)
    m = (_sigmoid(ga_ref[0]) * d(org_ref[0], wrg_ref[...])
         + _sigmoid(gb_ref[0]) * d(ogdn_ref[0], wgdn_ref[...])
         + _sigmoid(gc_ref[0]) * d(omla_ref[0], wmla_ref[...]))
    o_ref[0] = x_ref[0] + gt_ref[0] * d(m.astype(BF16), wo_ref[...])


def _merge(x, o_rg, o_gdn, o_mla, u, gt, wrg, wgdn, wmla, wo, bm):
    B, T, D = x.shape
    tok = lambda w, col: pl.BlockSpec((1, bm, w), lambda b, i: (b, i, col))
    const2 = lambda shape: pl.BlockSpec(shape, lambda b, i: (0,) * len(shape))
    return pl.pallas_call(
        _merge_kernel,
        grid=(B, T // bm),
        in_specs=[tok(D, 0), tok(D_RNN, 0), tok(GDN_VW, 0), tok(D, 0),
                  tok(D, U_GA // D), tok(D, U_GB // D), tok(D, U_GC // D),
                  _mod_spec(gt, bm),
                  const2((D_RNN, D)), const2((GDN_VW, D)), const2((D, D)), const2((D, D))],
        out_specs=tok(D, 0),
        out_shape=jax.ShapeDtypeStruct((B, T, D), F32),
        name="branch_merge",
        compiler_params=_cparams(("parallel", "parallel")),
    )(x, o_rg, o_gdn, o_mla, u, u, u, gt, wrg, wgdn, wmla, wo)


def _ffn_kernel(x_ref, sc_ref, sh_ref, gt_ref, g_ref, gf_ref, wg_ref, wu_ref, wd_ref, o_ref, hb_ref, acc_ref, *, final):
    j = pl.program_id(2)

    @pl.when(j == 0)
    def _():
        h = _rms(x_ref[0], g_ref[...]) * (1.0 + sc_ref[0]) + sh_ref[0]
        hb_ref[...] = h.astype(BF16)
        acc_ref[...] = jnp.zeros(acc_ref.shape, F32)

    hb = hb_ref[...]
    gate = jnp.dot(hb, wg_ref[...], preferred_element_type=F32)
    up = jnp.dot(hb, wu_ref[...], preferred_element_type=F32)
    acc_ref[...] += jnp.dot((_silu(gate) * up).astype(BF16), wd_ref[...], preferred_element_type=F32)

    @pl.when(j == pl.num_programs(2) - 1)
    def _():
        y = x_ref[0] + gt_ref[0] * acc_ref[...]
        o_ref[0] = _rms(y, gf_ref[...]) if final else y


def _ffn(x, sc, sh, gt, g, g_final, w_in_b, w_out_b, bm, bf, final):
    B, T, D = x.shape
    nf = D_FF // bf
    return pl.pallas_call(
        functools.partial(_ffn_kernel, final=final),
        grid=(B, T // bm, nf),
        in_specs=[pl.BlockSpec((1, bm, D), lambda b, i, j: (b, i, 0)),
                  _mod_spec(sc, bm), _mod_spec(sh, bm), _mod_spec(gt, bm),
                  pl.BlockSpec((1, D), lambda b, i, j: (0, 0)),
                  pl.BlockSpec((1, D), lambda b, i, j: (0, 0)),
                  pl.BlockSpec((D, bf), lambda b, i, j: (0, j)),
                  pl.BlockSpec((D, bf), lambda b, i, j: (0, nf + j)),
                  pl.BlockSpec((bf, D), lambda b, i, j: (j, 0))],
        out_specs=pl.BlockSpec((1, bm, D), lambda b, i, j: (b, i, 0)),
        out_shape=jax.ShapeDtypeStruct((B, T, D), F32),
        scratch_shapes=[pltpu.VMEM((bm, D), BF16), pltpu.VMEM((bm, D), F32)],
        name="swiglu",
        compiler_params=_cparams(("parallel", "parallel", "arbitrary")),
    )(x, sc, sh, gt, g.reshape(1, D), g_final.reshape(1, D), w_in_b, w_in_b, w_out_b)


def _rot_half(w):
    half = QK_ROPE // 2
    return jnp.concatenate([-w[..., half:], w[..., :half]], axis=-1)


def _pack_w_in(w):
    D = w.shape[0]
    wt = w.T
    offs = np.cumsum((0,) + IN_SIZES)
    rx, ry, qkv, z, a, b, mq, mkv, gate = [wt[offs[i]:offs[i + 1]] for i in range(len(IN_SIZES))]
    zeros = lambda n: jnp.zeros((n, D), w.dtype)
    kr = mkv[KV_LORA:]
    half = QK_ROPE // 2
    kr_rot = jnp.concatenate([-kr[half:], kr[:half]], axis=0)
    tail = LANES - ROPE_LANE0 - QK_ROPE
    rows = [rx, ry, zeros(U_QKV - 2 * D_RNN), qkv, z, gate,
            a, b, zeros(W_AB - 2 * GDN_HEADS), mq,
            mkv[:KV_LORA], zeros(ROPE_LANE0), kr, zeros(tail), zeros(ROPE_LANE0), kr_rot, zeros(tail)]
    out = jnp.concatenate(rows, axis=0).astype(BF16)
    assert out.shape[0] == N_U
    return out


def _pack_w_uq(w):
    H = MLA_HEADS
    w = w.reshape(Q_LORA, H, QK_NOPE + QK_ROPE)
    nope, rope = w[..., :QK_NOPE], w[..., QK_NOPE:]
    tail = jnp.zeros((Q_LORA, H, LANES - QK_NOPE - QK_ROPE), w.dtype)
    a = jnp.concatenate([nope, rope, tail], axis=-1).reshape(Q_LORA, H * LANES)
    b = jnp.concatenate([jnp.zeros_like(nope), _rot_half(rope), tail], axis=-1).reshape(Q_LORA, H * LANES)
    return jnp.concatenate([a, b], axis=1).astype(BF16)


def _pack_w_ukv(w):
    H = MLA_HEADS
    w3 = w.reshape(KV_LORA, H, QK_NOPE + V_HEAD)
    w_uk, w_uv = w3[..., :QK_NOPE], w3[..., QK_NOPE:]
    kpad = jnp.concatenate([w_uk, jnp.zeros((KV_LORA, H, LANES - QK_NOPE), w.dtype)], axis=-1)
    wkv_p = jnp.concatenate([kpad.reshape(KV_LORA, H * LANES), w_uv.reshape(KV_LORA, H * V_HEAD)], axis=1)
    wuk_s = jnp.transpose(kpad, (1, 2, 0)).reshape(H * LANES, KV_LORA)
    wuv_c = w_uv.reshape(KV_LORA, H * V_HEAD)
    return wkv_p.astype(BF16), wuk_s.astype(BF16), wuv_c.astype(BF16)


def _rope_tables(T, pos0):
    inv = ROPE_BASE ** (-jnp.arange(0, QK_ROPE, 2, dtype=F32) / QK_ROPE)
    ang = (jnp.arange(T, dtype=F32) + pos0)[:, None] * inv[None, :]
    cos, sin = jnp.cos(ang), jnp.sin(ang)
    tail = jnp.zeros((T, LANES - ROPE_LANE0 - QK_ROPE), F32)
    cos_t = jnp.concatenate([jnp.ones((T, ROPE_LANE0), F32), cos, cos, tail], axis=1)
    sin_t = jnp.concatenate([jnp.zeros((T, ROPE_LANE0), F32), sin, sin, tail], axis=1)
    return cos_t[None], sin_t[None]


def _split_mod(mod, per_token):
    R = mod.shape[0]
    parts = jnp.split(mod, 6, axis=-1)
    return [p.reshape(1, R, D_MODEL) if per_token else p.reshape(R, 1, D_MODEL) for p in parts]


def kernel(x_prompt, x_sample, cache_ckv, cache_krope, state_rg_conv, state_rg_h, state_gdn_conv, state_gdn_S,
           page_table, c_prompt, c_sample, w_ada, b_ada, g_norm1, g_norm2, w_in, rg_conv_w, rg_conv_b, rg_wa,
           rg_ba, rg_wx, rg_bx, rg_lambda, gdn_conv_w, gdn_A_log, gdn_dt_bias, gdn_norm_g, mla_q_norm_g, w_uq,
           mla_kv_norm_g, w_ukv, w_rg_proj, w_gdn_proj, w_mla_proj, w_o, w_ffn_in, w_ffn_out, g_final):
    L = w_in.shape[0]
    Bp, T, D = x_prompt.shape
    Bd = x_sample.shape[0]
    past_len = page_table.shape[1] * PAGE_SIZE

    mod = _modulation(jnp.concatenate([c_prompt, c_sample], axis=0), w_ada, b_ada)
    cos_p, sin_p = _rope_tables(T, 0.0)
    cos_s, sin_s = _rope_tables(1, float(past_len))
    cos_s = jnp.broadcast_to(cos_s, (1, Bd, LANES))
    sin_s = jnp.broadcast_to(sin_s, (1, Bd, LANES))

    cache_krope_t = jnp.swapaxes(cache_krope, 2, 3)
    xp = x_prompt
    xs = x_sample.reshape(1, Bd, D)
    outs_p, outs_s = [], []
    for l in range(L):
        w_in_p = _pack_w_in(w_in[l])
        wq_p = _pack_w_uq(w_uq[l])
        wkv_p, wuk_s, wuv_c = _pack_w_ukv(w_ukv[l])
        wrg, wgdn, wmla, wo = (w_rg_proj[l].astype(BF16), w_gdn_proj[l].astype(BF16),
                               w_mla_proj[l].astype(BF16), w_o[l].astype(BF16))
        wfi, wfo = w_ffn_in[l].astype(BF16), w_ffn_out[l].astype(BF16)
        final = l == L - 1
        rg_w = (rg_conv_w[l], rg_conv_b[l], rg_wa[l], rg_ba[l], rg_wx[l], rg_bx[l], rg_lambda[l])
        gdn_w = (gdn_conv_w[l], gdn_A_log[l], gdn_dt_bias[l], gdn_norm_g[l])

        sh1, sc1, gt1, sh2, sc2, gt2 = _split_mod(mod[l, :Bp], per_token=False)
        u = _in_proj(xp, sc1, sh1, g_norm1[l], w_in_p, bm=min(1024, T), bn=1024)
        o_rg, rg_buf, rg_h = _rglru(u, jnp.zeros((Bp, CONV_W - 1, D_RNN), F32), jnp.zeros((Bp, D_RNN), F32),
                                    *rg_w, tt=128)
        o_gdn, gdn_buf, gdn_S = _gdn_prompt(u, *gdn_w, tt=256)
        q, k, v, ckv, kr = _mla_prep(u, cos_p, sin_p, mla_q_norm_g[l], mla_kv_norm_g[l], wq_p, wkv_p, bm=512)
        o_mla = _flash(q, k, v, tq=min(512, T))
        x1 = _merge(xp, o_rg, o_gdn, o_mla, u, gt1, wrg, wgdn, wmla, wo, bm=256)
        xp = _ffn(x1, sc2, sh2, gt2, g_norm2[l], g_final, wfi, wfo, bm=512, bf=1408, final=final)
        outs_p.append((ckv, kr, rg_buf, rg_h.reshape(Bp, D_RNN), gdn_buf, gdn_S))

        sh1, sc1, gt1, sh2, sc2, gt2 = _split_mod(mod[l, Bp:], per_token=True)
        us = _in_proj(xs, sc1, sh1, g_norm1[l], w_in_p, bm=Bd, bn=1024)
        us_seq = us.reshape(Bd, 1, N_U)
        o_rg, rg_buf, rg_h = _rglru(us_seq, state_rg_conv[l], state_rg_h[l], *rg_w, tt=1)
        o_gdn, gdn_buf, gdn_S = _gdn_step(us_seq, state_gdn_conv[l], state_gdn_S[l], *gdn_w)
        q, k, v, ckv, kr = _mla_prep(us, cos_s, sin_s, mla_q_norm_g[l], mla_kv_norm_g[l], wq_p, wkv_p, bm=Bd)
        o_mla = _sample_attention(page_table, q.reshape(Bd, MLA_HEADS * LANES), ckv.reshape(Bd, 1, KV_LORA),
                                  kr.reshape(Bd, 1, QK_ROPE), wuk_s, wuv_c, cache_ckv, cache_krope_t, l)
        x1 = _merge(xs, o_rg.reshape(1, Bd, D_RNN), o_gdn.reshape(1, Bd, GDN_VW), o_mla.reshape(1, Bd, D),
                    us, gt1, wrg, wgdn, wmla, wo, bm=Bd)
        xs = _ffn(x1, sc2, sh2, gt2, g_norm2[l], g_final, wfi, wfo, bm=Bd, bf=1408, final=final)
        outs_s.append((ckv.reshape(Bd, 1, KV_LORA), kr.reshape(Bd, 1, QK_ROPE), rg_buf, rg_h.reshape(Bd, D_RNN),
                       gdn_buf, gdn_S))

    stack = lambda outs, i: jnp.stack([o[i] for o in outs])
    return (xp, xs.reshape(Bd, 1, D),
            stack(outs_p, 0), stack(outs_p, 1), stack(outs_p, 2), stack(outs_p, 3), stack(outs_p, 4), stack(outs_p, 5),
            stack(outs_s, 0), stack(outs_s, 1), stack(outs_s, 2), stack(outs_s, 3), stack(outs_s, 4), stack(outs_s, 5))
```

```python
import functools
import math

import numpy as np
import jax
import jax.numpy as jnp
from jax import lax
from jax.experimental import pallas as pl
from jax.experimental.pallas import tpu as pltpu

F32 = jnp.float32
BF16 = jnp.bfloat16

D_MODEL = 1024
CONV_W = 4
D_RNN = 1280
RG_BLOCK = 128
RG_BLOCKS = D_RNN // RG_BLOCK
RG_C = 8.0
GDN_HEADS = 8
GDN_DK = 128
GDN_DV = 128
GDN_KW = GDN_HEADS * GDN_DK
GDN_VW = GDN_HEADS * GDN_DV
GDN_CONV_C = 2 * GDN_KW + GDN_VW
GDN_CHUNK = 64
MLA_HEADS = 8
Q_LORA = 384
KV_LORA = 256
QK_NOPE = 64
QK_ROPE = 32
V_HEAD = 128
MLA_SCALE = (QK_NOPE + QK_ROPE) ** -0.5
ROPE_BASE = 10000.0
D_FF = 2816
N_BRANCH = 3
IN_SIZES = (D_RNN, D_RNN, GDN_CONV_C, GDN_VW, GDN_HEADS, GDN_HEADS, Q_LORA, KV_LORA + QK_ROPE, N_BRANCH * D_MODEL)
EPS = 1e-6
PAGE_SIZE = 128

LANES = 128
SUBLANES = 8
VMEM_LIMIT = 56 * 1024 * 1024

U_RX = 0
U_RY = D_RNN
U_QKV = 3072
U_Z = 6144
U_GA = 7168
U_GB = 8192
U_GC = 9216
U_AB = 10240
U_MQ = 10368
U_MKV = 10752
N_U = 11264
W_AB = LANES
W_MKV = 512
ROPE_LANE0 = QK_NOPE


def _cparams(sem):
    return pltpu.CompilerParams(dimension_semantics=sem, vmem_limit_bytes=VMEM_LIMIT)


def _sigmoid(x):
    return 1.0 / (1.0 + jnp.exp(-x))


def _silu(x):
    return x * _sigmoid(x)


def _softplus(x):
    return jnp.maximum(x, 0.0) + jnp.log1p(jnp.exp(-jnp.abs(x)))


def _gelu_tanh(x):
    c = math.sqrt(2.0 / math.pi)
    return 0.5 * x * (1.0 + jnp.tanh(c * (x + 0.044715 * (x * x * x))))


def _rms(x, g):
    return x * lax.rsqrt(jnp.mean(x * x, axis=-1, keepdims=True) + EPS) * g


def _dot(a, b):
    return jnp.dot(a.astype(BF16), b.astype(BF16), preferred_element_type=F32)


def _dot_nt(a, b):
    return lax.dot_general(a.astype(BF16), b.astype(BF16), (((1,), (1,)), ((), ())), preferred_element_type=F32)


def _dot_tn(a, b):
    return lax.dot_general(a.astype(BF16), b.astype(BF16), (((0,), (0,)), ((), ())), preferred_element_type=F32)


def _split2(a):
    hi = a.astype(BF16)
    lo = (a - hi.astype(F32)).astype(BF16)
    return hi, lo


def _dot3(a, b):
    ah, al = _split2(a)
    bh, bl = _split2(b)
    d = functools.partial(jnp.dot, preferred_element_type=F32)
    return d(ah, bh) + (d(ah, bl) + d(al, bh))


def _dot_exact_lhs(a_bf16, b):
    b0 = b.astype(BF16)
    r1 = b - b0.astype(F32)
    b1 = r1.astype(BF16)
    b2 = (r1 - b1.astype(F32)).astype(BF16)
    d = functools.partial(jnp.dot, preferred_element_type=F32)
    return d(a_bf16, b0) + (d(a_bf16, b1) + d(a_bf16, b2))


def _mod_kernel(c_ref, w_ref, b_ref, o_ref):
    c = c_ref[...]
    o_ref[0] = _dot(_silu(c), w_ref[0]) + b_ref[0]


def _modulation(c_all, w_ada, b_ada):
    L, D, N = w_ada.shape
    R = c_all.shape[0]
    bn = 1536
    return pl.pallas_call(
        _mod_kernel,
        grid=(L, N // bn),
        in_specs=[pl.BlockSpec((R, D), lambda l, j: (0, 0)),
                  pl.BlockSpec((1, D, bn), lambda l, j: (l, 0, j)),
                  pl.BlockSpec((1, 1, bn), lambda l, j: (l, 0, j))],
        out_specs=pl.BlockSpec((1, R, bn), lambda l, j: (l, 0, j)),
        out_shape=jax.ShapeDtypeStruct((L, R, N), F32),
        name="adaln_mod",
        compiler_params=_cparams(("parallel", "parallel")),
    )(c_all, w_ada, b_ada.reshape(L, 1, N))


MOD_SH1, MOD_SC1, MOD_GT1, MOD_SH2, MOD_SC2, MOD_GT2 = range(6)


class _Mod:
    def __init__(self, mod, layer, per_token, row0):
        self.layer, self.per_token, self.row0 = layer, per_token, row0
        L, R, N = mod.shape
        self.array = mod if per_token else mod.reshape(L, R, 1, N)

    def spec(self, part, bm):
        l, r0 = self.layer, self.row0
        if self.per_token:
            assert r0 == 0
            return pl.BlockSpec((None, bm, D_MODEL), lambda b, i, *_: (l, i, part))
        return pl.BlockSpec((None, None, 1, D_MODEL), lambda b, i, *_: (l, r0 + b, 0, part))


def _lspec(shape, l):
    return pl.BlockSpec((None,) + tuple(shape), lambda *_: (l,) + (0,) * len(shape))


def _in_kernel(x_ref, sc_ref, sh_ref, g_ref, w_ref, o_ref, hb_ref):
    @pl.when(pl.program_id(2) == 0)
    def _():
        h = _rms(x_ref[0], g_ref[...]) * (1.0 + sc_ref[...]) + sh_ref[...]
        hb_ref[...] = h.astype(BF16)

    o_ref[0] = lax.dot_general(hb_ref[...], w_ref[...], (((1,), (1,)), ((), ())), preferred_element_type=F32)


def _in_proj(x, mod, g, w_packed, l, bm, bn):
    B, T, D = x.shape
    N = w_packed.shape[1]
    return pl.pallas_call(
        _in_kernel,
        grid=(B, T // bm, N // bn),
        in_specs=[pl.BlockSpec((1, bm, D), lambda b, i, j: (b, i, 0)),
                  mod.spec(MOD_SC1, bm), mod.spec(MOD_SH1, bm),
                  _lspec((1, D), l),
                  pl.BlockSpec((None, bn, D), lambda b, i, j: (l, j, 0))],
        out_specs=pl.BlockSpec((1, bm, bn), lambda b, i, j: (b, i, j)),
        out_shape=jax.ShapeDtypeStruct((B, T, N), F32),
        scratch_shapes=[pltpu.VMEM((bm, D), BF16)],
        name="in_proj",
        compiler_params=_cparams(("parallel", "parallel", "arbitrary")),
    )(x, mod.array, mod.array, g, w_packed)


SCAN_PAD = 64


def _rg_kernel(ux_ref, uy_ref, buf_ref, h0_ref, cw_ref, cb_ref, wa_ref, ba_ref, wx_ref, bx_ref, lam_ref,
               o_ref, nbuf_ref, hl_ref, xbuf, abuf, bbuf, hc, *, tt):
    i = pl.program_id(1)
    nt = pl.num_programs(1)

    @pl.when(i == 0)
    def _():
        xbuf[0:SUBLANES, :] = buf_ref[0]
        hc[...] = h0_ref[0]
        abuf[0:SCAN_PAD, :] = jnp.ones((SCAN_PAD, D_RNN), F32)
        bbuf[0:SCAN_PAD, :] = jnp.zeros((SCAN_PAD, D_RNN), F32)

    xbuf[SUBLANES:SUBLANES + tt, :] = ux_ref[0]
    lo = SUBLANES - (CONV_W - 1)
    if tt >= SUBLANES:
        xs = xbuf[...]
        xc = cb_ref[...] + cw_ref[CONV_W - 1:CONV_W, :] * xs[SUBLANES:, :]
        for j in range(CONV_W - 1):
            xc = xc + cw_ref[j:j + 1, :] * pltpu.roll(xs, CONV_W - 1 - j, axis=0)[SUBLANES:, :]
    else:
        xc = cb_ref[...] + cw_ref[0:1, :] * xbuf[lo:lo + tt, :]
        for j in range(1, CONV_W):
            xc = xc + cw_ref[j:j + 1, :] * xbuf[lo + j:lo + j + tt, :]

    @pl.when(i == nt - 1)
    def _():
        nbuf_ref[0] = xbuf[tt + lo:tt + SUBLANES, :]

    if tt >= SUBLANES:
        xbuf[0:SUBLANES, :] = xbuf[tt:tt + SUBLANES, :]

    xb = xc.astype(BF16)
    ra = jnp.concatenate([jnp.dot(xb[:, n * RG_BLOCK:(n + 1) * RG_BLOCK], wa_ref[n], preferred_element_type=F32)
                          for n in range(RG_BLOCKS)], axis=1)
    ri = jnp.concatenate([jnp.dot(xb[:, n * RG_BLOCK:(n + 1) * RG_BLOCK], wx_ref[n], preferred_element_type=F32)
                          for n in range(RG_BLOCKS)], axis=1)
    r = _sigmoid(ra + ba_ref[...])
    ig = _sigmoid(ri + bx_ref[...])
    log_a = (-RG_C) * r * _softplus(-lam_ref[...])
    a = jnp.exp(log_a)
    b = jnp.sqrt(-jnp.tanh(log_a) * (a * a + 1.0)) * (ig * xc)

    abuf[SCAN_PAD:SCAN_PAD + tt, :] = a
    bbuf[SCAN_PAD:SCAN_PAD + tt, :] = b
    d = 1
    while d < tt:
        a_s = abuf[SCAN_PAD - d:SCAN_PAD - d + tt, :]
        b_s = bbuf[SCAN_PAD - d:SCAN_PAD - d + tt, :]
        a0 = abuf[SCAN_PAD:SCAN_PAD + tt, :]
        b0 = bbuf[SCAN_PAD:SCAN_PAD + tt, :]
        abuf[SCAN_PAD:SCAN_PAD + tt, :] = a0 * a_s
        bbuf[SCAN_PAD:SCAN_PAD + tt, :] = a0 * b_s + b0
        d *= 2
    h = bbuf[SCAN_PAD:SCAN_PAD + tt, :] + abuf[SCAN_PAD:SCAN_PAD + tt, :] * hc[...]
    hc[...] = h[tt - 1:tt, :]
    o_ref[0] = (h * _gelu_tanh(uy_ref[0])).astype(BF16)

    @pl.when(i == nt - 1)
    def _():
        hl_ref[0] = h[tt - 1:tt, :]


def _rglru(u, buf8, h0, ls, p, l, tt):
    B, T, _ = u.shape
    C = D_RNN
    wblk = (RG_BLOCKS, RG_BLOCK, RG_BLOCK)
    return pl.pallas_call(
        functools.partial(_rg_kernel, tt=tt),
        grid=(B, T // tt),
        in_specs=[pl.BlockSpec((1, tt, C), lambda b, i: (b, i, U_RX // C)),
                  pl.BlockSpec((1, tt, C), lambda b, i: (b, i, U_RY // C)),
                  pl.BlockSpec((None, 1, SUBLANES, C), lambda b, i: (ls, b, 0, 0)),
                  pl.BlockSpec((None, 1, 1, C), lambda b, i: (ls, b, 0, 0)),
                  _lspec((CONV_W, C), l), _lspec((1, C), l),
                  _lspec(wblk, l), _lspec((1, C), l),
                  _lspec(wblk, l), _lspec((1, C), l), _lspec((1, C), l)],
        out_specs=[pl.BlockSpec((1, tt, C), lambda b, i: (b, i, 0)),
                   pl.BlockSpec((1, CONV_W - 1, C), lambda b, i: (b, 0, 0)),
                   pl.BlockSpec((1, 1, C), lambda b, i: (b, 0, 0))],
        out_shape=[jax.ShapeDtypeStruct((B, T, C), BF16),
                   jax.ShapeDtypeStruct((B, CONV_W - 1, C), F32),
                   jax.ShapeDtypeStruct((B, 1, C), F32)],
        scratch_shapes=[pltpu.VMEM((tt + SUBLANES, C), F32),
                        pltpu.VMEM((SCAN_PAD + tt, C), F32),
                        pltpu.VMEM((SCAN_PAD + tt, C), F32),
                        pltpu.VMEM((1, C), F32)],
        name="rglru",
        compiler_params=_cparams(("parallel", "arbitrary")),
    )(u, u, buf8, h0, p["cw"], p["cb"], p["wa"], p["ba"], p["wx"], p["bx"], p["lam"])


TRI_BASE = 8
INV_GROUP = 8
NEWTON_STEPS = 2


def _cat_dot3(x, y, bd_b, n):
    C = x.shape[0]
    xh, xl = _split2(x)
    yh, yl = _split2(y)
    d = functools.partial(jnp.dot, preferred_element_type=F32)
    r = d(jnp.concatenate([xh, xl], axis=0), jnp.concatenate([yh] * n, axis=0) * bd_b)
    return r[:C] + (r[C:] + d(xh, jnp.concatenate([yl] * n, axis=0) * bd_b))


def _cat_dot1(x, y, bd_b, n):
    yb = jnp.concatenate([y.astype(BF16)] * n, axis=0) * bd_b
    return jnp.dot(x.astype(BF16), yb, preferred_element_type=F32)


def _tri_inv_cat(Lcs, masks, bd_b, n):
    eye, base_mask, level_masks = masks
    mm = functools.partial(_cat_dot1, bd_b=bd_b, n=n)
    mm3 = functools.partial(_cat_dot3, bd_b=bd_b, n=n)
    N = [jnp.where(base_mask, -Lc, 0.0) for Lc in Lcs]
    P = [eye + x for x in N]
    N2 = [mm(x, x) for x in N]
    P = [p + mm(p, x) for p, x in zip(P, N2)]
    N4 = [mm(x, x) for x in N2]
    P = [p + mm(p, x) for p, x in zip(P, N4)]
    for m in level_masks:
        PO = [mm(p, jnp.where(m, Lc, 0.0)) for p, Lc in zip(P, Lcs)]
        P = [p - mm(po, p) for p, po in zip(P, PO)]
    for _ in range(NEWTON_STEPS):
        R = [eye - p - mm3(Lc, p) for p, Lc in zip(P, Lcs)]
        P = [p + mm(p, r) for p, r in zip(P, R)]
    return P


def _cat_masks(C, n):
    rr = lax.broadcasted_iota(jnp.int32, (C, n * C), 0)
    jj = lax.broadcasted_iota(jnp.int32, (C, n * C), 1) % C
    same = lambda s: (rr // s) == (jj // s)
    eye = (rr == jj).astype(F32)
    levels = []
    s = TRI_BASE
    while s < C:
        levels.append(jnp.logical_and(same(2 * s), jnp.logical_not(same(s))))
        s *= 2
    return eye, same(TRI_BASE), levels


def _gdn_kernel(qkv_ref, z_ref, ab_ref, buf_ref, cw_ref, alog_ref, dtb_ref, ng_ref,
                o_ref, nbuf_ref, S_ref, xbuf, act, S_sc, wq_sc, uv_sc, qk_sc, kd_sc, gl_sc, lc_sc, rhs_sc, *, tt):
    i = pl.program_id(1)
    nt = pl.num_programs(1)
    C = GDN_CHUNK
    H = GDN_HEADS
    n = tt // C

    @pl.when(i == 0)
    def _():
        xbuf[0:SUBLANES, :] = buf_ref[0]
        S_sc[...] = jnp.zeros(S_sc.shape, F32)

    xbuf[SUBLANES:SUBLANES + tt, :] = qkv_ref[0]
    lo = SUBLANES - (CONV_W - 1)
    xs = xbuf[...]
    y = cw_ref[CONV_W - 1:CONV_W, :] * xs[SUBLANES:, :]
    for j in range(CONV_W - 1):
        y = y + cw_ref[j:j + 1, :] * pltpu.roll(xs, CONV_W - 1 - j, axis=0)[SUBLANES:, :]
    act[...] = _silu(y)

    @pl.when(i == nt - 1)
    def _():
        nbuf_ref[0] = xbuf[tt + lo:tt + SUBLANES, :]

    xbuf[0:SUBLANES, :] = xbuf[tt:tt + SUBLANES, :]

    ab = ab_ref[0]
    g = -jnp.exp(alog_ref[...]) * _softplus(ab + dtb_ref[...])
    beta = _sigmoid(ab)

    ri = lax.broadcasted_iota(jnp.int32, (tt, tt), 0)
    ci = lax.broadcasted_iota(jnp.int32, (tt, tt), 1)
    bd = (ri // C) == (ci // C)
    bd_incl = jnp.logical_and(bd, ri >= ci)
    bd_strict = jnp.logical_and(bd, ri > ci)
    bd_b = bd.astype(BF16)
    last_b = jnp.logical_and(bd, ci % C == C - 1).astype(BF16)
    gcum = _dot_exact_lhs(bd_incl.astype(BF16), g)
    glast = _dot_exact_lhs(last_b, gcum)
    gl_sc[...] = jnp.exp(glast)
    gcum_t = gcum.T
    masks = _cat_masks(C, n)
    ng = ng_ref[...]

    for h in range(H):
        q = act[:, h * GDN_DK:(h + 1) * GDN_DK]
        k = act[:, GDN_KW + h * GDN_DK:GDN_KW + (h + 1) * GDN_DK]
        v = act[:, 2 * GDN_KW + h * GDN_DV:2 * GDN_KW + (h + 1) * GDN_DV]
        q = q * lax.rsqrt(jnp.sum(q * q, axis=-1, keepdims=True) + EPS) * (GDN_DK ** -0.5)
        k = k * lax.rsqrt(jnp.sum(k * k, axis=-1, keepdims=True) + EPS)
        gc = gcum[:, h:h + 1]
        gr = gcum_t[h:h + 1, :]
        e = jnp.exp(jnp.where(bd_incl, gc - gr, -jnp.inf))
        bt = beta[:, H + h:H + h + 1]
        kb = k.astype(BF16)
        qb = q.astype(BF16)
        Lf = jnp.where(bd_strict, bt * _dot_nt(kb, kb) * e, 0.0)
        qkd = (_dot_nt(qb, kb) * e).astype(BF16)
        for c in range(n):
            qk_sc[h, c] = qkd[c * C:(c + 1) * C, c * C:(c + 1) * C]
        Lc = Lf[0:C]
        for c in range(1, n):
            Lc = Lc + Lf[c * C:(c + 1) * C]
        lc_sc[h] = Lc
        eg = jnp.exp(gc)
        rhs_sc[h, :, 0:GDN_DV] = bt * v
        rhs_sc[h, :, GDN_DV:] = (bt * eg) * k
        qe = (eg * q).astype(BF16)
        for c in range(n):
            wq_sc[h, c, C:2 * C, :] = qe[c * C:(c + 1) * C]
        kd_sc[h] = (k * jnp.exp(glast[:, h:h + 1] - gc)).astype(BF16)

    for h0 in range(0, H, INV_GROUP):
        hs = range(h0, h0 + INV_GROUP)
        t_cats = _tri_inv_cat([lc_sc[h] for h in hs], masks, bd_b, n)
        for h, t_cat in zip(hs, t_cats):
            t_bd = jnp.where(bd, jnp.concatenate([t_cat] * n, axis=0), 0.0)
            uw = _dot3(t_bd, rhs_sc[h])
            uv_sc[h] = uw[:, :GDN_DV]
            w = uw[:, GDN_DV:].astype(BF16)
            for c in range(n):
                wq_sc[h, c, 0:C, :] = w[c * C:(c + 1) * C]

    def chunk(c, carry):
        r0 = pl.multiple_of(c * C, C)
        rows = pl.ds(r0, C)
        gl = gl_sc[pl.ds(r0, 1), :]
        S = [S_sc[h] for h in range(H)]
        wq = [jnp.dot(wq_sc[h, c], S[h].astype(BF16), preferred_element_type=F32) for h in range(H)]
        Ub = [(uv_sc[h, rows, :] - wq[h][:C]).astype(BF16) for h in range(H)]
        o = [wq[h][C:] + jnp.dot(qk_sc[h, c], Ub[h], preferred_element_type=F32) for h in range(H)]
        for h in range(H):
            S_sc[h] = gl[:, h:h + 1] * S[h] + _dot_tn(kd_sc[h, rows, :], Ub[h])
        for h in range(H):
            zz = z_ref[0, rows, h * GDN_DV:(h + 1) * GDN_DV]
            o_ref[0, rows, h * GDN_DV:(h + 1) * GDN_DV] = (_rms(o[h], ng) * _silu(zz)).astype(BF16)
        return carry

    lax.fori_loop(0, n, chunk, 0)

    @pl.when(i == nt - 1)
    def _():
        S_ref[0] = S_sc[...]


def _gdn_prompt(u, buf8, p, l, tt):
    B, T, _ = u.shape
    Cc = GDN_CONV_C
    H = GDN_HEADS
    return pl.pallas_call(
        functools.partial(_gdn_kernel, tt=tt),
        grid=(B, T // tt),
        in_specs=[pl.BlockSpec((1, tt, Cc), lambda b, i: (b, i, U_QKV // Cc)),
                  pl.BlockSpec((1, tt, GDN_VW), lambda b, i: (b, i, U_Z // GDN_VW)),
                  pl.BlockSpec((1, tt, W_AB), lambda b, i: (b, i, U_AB // W_AB)),
                  pl.BlockSpec((None, 1, SUBLANES, Cc), lambda b, i: (0, b, 0, 0)),
                  _lspec((CONV_W, Cc), l), _lspec((1, LANES), l), _lspec((1, LANES), l), _lspec((1, GDN_DV), l)],
        out_specs=[pl.BlockSpec((1, tt, GDN_VW), lambda b, i: (b, i, 0)),
                   pl.BlockSpec((1, CONV_W - 1, Cc), lambda b, i: (b, 0, 0)),
                   pl.BlockSpec((1, H, GDN_DK, GDN_DV), lambda b, i: (b, 0, 0, 0))],
        out_shape=[jax.ShapeDtypeStruct((B, T, GDN_VW), BF16),
                   jax.ShapeDtypeStruct((B, CONV_W - 1, Cc), F32),
                   jax.ShapeDtypeStruct((B, H, GDN_DK, GDN_DV), F32)],
        scratch_shapes=[pltpu.VMEM((tt + SUBLANES, Cc), F32),
                        pltpu.VMEM((tt, Cc), F32),
                        pltpu.VMEM((H, GDN_DK, GDN_DV), F32),
                        pltpu.VMEM((H, tt // GDN_CHUNK, 2 * GDN_CHUNK, GDN_DK), BF16),
                        pltpu.VMEM((H, tt, GDN_DV), F32),
                        pltpu.VMEM((H, tt // GDN_CHUNK, GDN_CHUNK, GDN_CHUNK), BF16),
                        pltpu.VMEM((H, tt, GDN_DK), BF16),
                        pltpu.VMEM((tt, LANES), F32),
                        pltpu.VMEM((H, GDN_CHUNK, tt), F32),
                        pltpu.VMEM((H, tt, GDN_DV + GDN_DK), F32)],
        name="gdn_chunked",
        compiler_params=_cparams(("parallel", "arbitrary")),
    )(u, u, u, buf8, p["cw"], p["alog"], p["dtb"], p["ng"])


def _gdn_step_kernel(qkv_ref, z_ref, ab_ref, buf_ref, S0_ref, cw_ref, alog_ref, dtb_ref, ng_ref,
                     o_ref, nbuf_ref, S_ref, xbuf):
    H = GDN_HEADS
    xbuf[0:SUBLANES, :] = buf_ref[0]
    xbuf[SUBLANES:SUBLANES + 1, :] = qkv_ref[0]
    lo = SUBLANES - (CONV_W - 1)
    y = cw_ref[0:1, :] * xbuf[lo:lo + 1, :]
    for j in range(1, CONV_W):
        y = y + cw_ref[j:j + 1, :] * xbuf[lo + j:lo + j + 1, :]
    y = _silu(y)
    nbuf_ref[0] = xbuf[lo + 1:SUBLANES + 1, :]

    ab = ab_ref[0]
    g_all = -jnp.exp(alog_ref[...]) * _softplus(ab + dtb_ref[...])
    beta_all = _sigmoid(ab)
    ii = lax.broadcasted_iota(jnp.int32, (GDN_DK, GDN_DK), 0)
    jj = lax.broadcasted_iota(jnp.int32, (GDN_DK, GDN_DK), 1)
    eye = ii == jj
    ng = ng_ref[...]
    for h in range(H):
        q = y[:, h * GDN_DK:(h + 1) * GDN_DK]
        k = y[:, GDN_KW + h * GDN_DK:GDN_KW + (h + 1) * GDN_DK]
        v = y[:, 2 * GDN_KW + h * GDN_DV:2 * GDN_KW + (h + 1) * GDN_DV]
        q = q * lax.rsqrt(jnp.sum(q * q, axis=-1, keepdims=True) + EPS) * (GDN_DK ** -0.5)
        k = k * lax.rsqrt(jnp.sum(k * k, axis=-1, keepdims=True) + EPS)
        eg = jnp.exp(g_all[:, h:h + 1])
        bt = beta_all[:, H + h:H + h + 1]
        S = S0_ref[0, h]
        Sb = S.astype(BF16)
        kb = k.astype(BF16)
        qb = q.astype(BF16)
        u = bt * (v - eg * jnp.dot(kb, Sb, preferred_element_type=F32))
        ub = u.astype(BF16)
        qk = jnp.sum(qb.astype(F32) * kb.astype(F32), axis=-1, keepdims=True)
        o = eg * jnp.dot(qb, Sb, preferred_element_type=F32) + qk.astype(BF16).astype(F32) * ub.astype(F32)
        kdiag = jnp.where(eye, jnp.broadcast_to(kb.astype(F32), (GDN_DK, GDN_DK)), 0.0).astype(BF16)
        urows = jnp.broadcast_to(ub, (GDN_DK, GDN_DV))
        S_ref[0, h] = eg * S + jnp.dot(kdiag, urows, preferred_element_type=F32)
        zz = z_ref[0, :, h * GDN_DV:(h + 1) * GDN_DV]
        o_ref[0, :, h * GDN_DV:(h + 1) * GDN_DV] = (_rms(o, ng) * _silu(zz)).astype(BF16)


def _gdn_step(u, buf8, S0, p, l):
    B = u.shape[0]
    Cc = GDN_CONV_C
    H = GDN_HEADS
    return pl.pallas_call(
        _gdn_step_kernel,
        grid=(B,),
        in_specs=[pl.BlockSpec((1, 1, Cc), lambda b: (b, 0, U_QKV // Cc)),
                  pl.BlockSpec((1, 1, GDN_VW), lambda b: (b, 0, U_Z // GDN_VW)),
                  pl.BlockSpec((1, 1, W_AB), lambda b: (b, 0, U_AB // W_AB)),
                  pl.BlockSpec((None, 1, SUBLANES, Cc), lambda b: (l, b, 0, 0)),
                  pl.BlockSpec((None, 1, H, GDN_DK, GDN_DV), lambda b: (l, b, 0, 0, 0)),
                  _lspec((CONV_W, Cc), l), _lspec((1, LANES), l), _lspec((1, LANES), l), _lspec((1, GDN_DV), l)],
        out_specs=[pl.BlockSpec((1, 1, GDN_VW), lambda b: (b, 0, 0)),
                   pl.BlockSpec((1, CONV_W - 1, Cc), lambda b: (b, 0, 0)),
                   pl.BlockSpec((1, H, GDN_DK, GDN_DV), lambda b: (b, 0, 0, 0))],
        out_shape=[jax.ShapeDtypeStruct((B, 1, GDN_VW), BF16),
                   jax.ShapeDtypeStruct((B, CONV_W - 1, Cc), F32),
                   jax.ShapeDtypeStruct((B, H, GDN_DK, GDN_DV), F32)],
        scratch_shapes=[pltpu.VMEM((2 * SUBLANES, Cc), F32)],
        name="gdn_step",
        compiler_params=_cparams(("parallel",)),
    )(u, u, u, buf8, S0, p["cw"], p["alog"], p["dtb"], p["ng"])


def _mla_prep_kernel(mq_ref, mkv_ref, cos_ref, sin_ref, gq_ref, gkv_ref, wq_ref, wkv_ref,
                     q_ref, k_ref, v_ref, ckv_ref, kr_ref):
    H = MLA_HEADS
    W = LANES * H
    cos = cos_ref[0]
    sin = sin_ref[0]
    cq = _rms(mq_ref[0], gq_ref[...])
    qa = _dot(cq, wq_ref[...])
    mkv = mkv_ref[0]
    ckv = _rms(mkv[:, :KV_LORA], gkv_ref[...])
    ckv_ref[0] = ckv
    kro = mkv[:, KV_LORA:KV_LORA + LANES] * cos + mkv[:, KV_LORA + LANES:KV_LORA + 2 * LANES] * sin
    kr_ref[0] = kro[:, ROPE_LANE0:ROPE_LANE0 + QK_ROPE]
    kv = _dot(ckv, wkv_ref[...])
    for h in range(H):
        sl = slice(h * LANES, (h + 1) * LANES)
        qh = qa[:, sl] * cos + qa[:, W + h * LANES:W + (h + 1) * LANES] * sin
        q_ref[0, :, sl] = (qh * MLA_SCALE).astype(BF16)
        k_ref[0, :, sl] = (kv[:, sl] + kro).astype(BF16)
    v_ref[0] = kv[:, W:].astype(BF16)


def _mla_prep(u, cos, sin, p, l, bm):
    B, T, _ = u.shape
    W = LANES * MLA_HEADS
    return pl.pallas_call(
        _mla_prep_kernel,
        grid=(B, T // bm),
        in_specs=[pl.BlockSpec((1, bm, Q_LORA), lambda b, i: (b, i, U_MQ // Q_LORA)),
                  pl.BlockSpec((1, bm, W_MKV), lambda b, i: (b, i, U_MKV // W_MKV)),
                  pl.BlockSpec((1, bm, LANES), lambda b, i: (0, i, 0)),
                  pl.BlockSpec((1, bm, LANES), lambda b, i: (0, i, 0)),
                  _lspec((1, Q_LORA), l), _lspec((1, KV_LORA), l),
                  _lspec((Q_LORA, 2 * W), l), _lspec((KV_LORA, 2 * W), l)],
        out_specs=[pl.BlockSpec((1, bm, W), lambda b, i: (b, i, 0)),
                   pl.BlockSpec((1, bm, W), lambda b, i: (b, i, 0)),
                   pl.BlockSpec((1, bm, W), lambda b, i: (b, i, 0)),
                   pl.BlockSpec((1, bm, KV_LORA), lambda b, i: (b, i, 0)),
                   pl.BlockSpec((1, bm, QK_ROPE), lambda b, i: (b, i, 0))],
        out_shape=[jax.ShapeDtypeStruct((B, T, W), BF16),
                   jax.ShapeDtypeStruct((B, T, W), BF16),
                   jax.ShapeDtypeStruct((B, T, W), BF16),
                   jax.ShapeDtypeStruct((B, T, KV_LORA), F32),
                   jax.ShapeDtypeStruct((B, T, QK_ROPE), F32)],
        name="mla_prep",
        compiler_params=_cparams(("parallel", "parallel")),
    )(u, u, cos, sin, p["gq"], p["gkv"], p["wq"], p["wkv"])


NEG_BIG = -1e30


FLASH_HEADS = 2
FLASH_TK = 512


def _flash_kernel(q_ref, k_ref, v_ref, o_ref, *, tq):
    qi = pl.program_id(2)
    hp = FLASH_HEADS
    heads = [slice(h * LANES, (h + 1) * LANES) for h in range(hp)]
    qs = [q_ref[0, :, sl] for sl in heads]
    tk = min(FLASH_TK, tq)
    n_full = (qi * tq) // tk

    def step(j, carry, diagonal):
        ks = pl.ds(pl.multiple_of(j * tk, tk), tk)
        if diagonal:
            row = qi * tq + lax.broadcasted_iota(jnp.int32, (tq, tk), 0)
            col = j * tk + lax.broadcasted_iota(jnp.int32, (tq, tk), 1)
            causal = col <= row
        out = []
        for h, sl in enumerate(heads):
            m, l, acc = carry[h]
            s = lax.dot_general(qs[h], k_ref[0, ks, sl], (((1,), (1,)), ((), ())), preferred_element_type=F32)
            if diagonal:
                s = jnp.where(causal, s, NEG_BIG)
            m_new = jnp.maximum(m, jnp.max(s, axis=-1, keepdims=True))
            alpha = jnp.exp(m - m_new)
            p = jnp.exp(s - m_new)
            l = alpha * l + jnp.sum(p, axis=-1, keepdims=True)
            acc = alpha * acc + jnp.dot(p.astype(BF16), v_ref[0, ks, sl], preferred_element_type=F32)
            out.append((m_new, l, acc))
        return tuple(out)

    init = tuple((jnp.full((tq, 1), NEG_BIG, F32), jnp.zeros((tq, 1), F32), jnp.zeros((tq, LANES), F32))
                 for _ in heads)
    carry = lax.fori_loop(0, n_full, lambda j, c: step(j, c, False), init)
    for t in range(tq // tk):
        carry = step(n_full + t, carry, True)
    for h, sl in enumerate(heads):
        _, l, acc = carry[h]
        o_ref[0, :, sl] = (acc / l).astype(BF16)


def _flash(q, k, v, tq):
    B, T, W = q.shape
    wb = FLASH_HEADS * LANES
    H = W // wb
    return pl.pallas_call(
        functools.partial(_flash_kernel, tq=tq),
        grid=(B, H, T // tq),
        in_specs=[pl.BlockSpec((1, tq, wb), lambda b, h, i: (b, i, h)),
                  pl.BlockSpec((1, T, wb), lambda b, h, i: (b, 0, h)),
                  pl.BlockSpec((1, T, wb), lambda b, h, i: (b, 0, h))],
        out_specs=pl.BlockSpec((1, tq, wb), lambda b, h, i: (b, i, h)),
        out_shape=jax.ShapeDtypeStruct((B, T, W), BF16),
        name="mla_flash",
        compiler_params=_cparams(("parallel", "parallel", "arbitrary")),
    )(q, k, v)


DEC_GROUP = 16
DEC_SLOTS = 3


def _sattn_kernel(pt_ref, q_ref, cn_ref, krn_ref, wuk_ref, wuv_ref, ckv_hbm, krt_hbm, o_ref,
                  ck_buf, kr_buf, sem, *, layer, n_pages):
    H = MLA_HEADS
    G = DEC_GROUP
    n_groups = n_pages // G
    b = pl.program_id(0)
    nb = pl.num_programs(0)

    total = nb * n_groups
    ahead = DEC_SLOTS - 1

    def group_copies(x):
        seq, g, slot = x // n_groups, x % n_groups, x % DEC_SLOTS
        out = []
        for k in range(G):
            page = pt_ref[seq, g * G + k]
            out.append(pltpu.make_async_copy(ckv_hbm.at[layer, page],
                                             ck_buf.at[slot, pl.ds(k * PAGE_SIZE, PAGE_SIZE), :], sem.at[0, slot]))
            out.append(pltpu.make_async_copy(krt_hbm.at[layer, page],
                                             kr_buf.at[slot, :, pl.ds(k * PAGE_SIZE, PAGE_SIZE)], sem.at[1, slot]))
        return out

    @pl.when(b == 0)
    def _():
        for x in range(ahead):
            for c in group_copies(x):
                c.start()

    q = q_ref[0]
    q_rope = q[:, ROPE_LANE0:ROPE_LANE0 + QK_ROPE]
    rr = lax.broadcasted_iota(jnp.int32, (H, H * LANES), 0)
    cc = lax.broadcasted_iota(jnp.int32, (H, H * LANES), 1)
    q_bd = jnp.where(cc // LANES == rr, jnp.concatenate([q] * H, axis=1), jnp.zeros((), BF16))
    qlat = jnp.dot(q_bd, wuk_ref[...], preferred_element_type=F32).astype(BF16)

    def body(g, carry):
        m, l, acc = carry
        x = b * n_groups + g
        slot = x % DEC_SLOTS

        @pl.when(x + ahead < total)
        def _():
            for c in group_copies(x + ahead):
                c.start()

        for c in group_copies(x):
            c.wait()
        ck = ck_buf[slot].astype(BF16)
        kr_t = kr_buf[slot].astype(BF16)
        s = _dot_nt(qlat, ck) + jnp.dot(q_rope, kr_t, preferred_element_type=F32)
        m_new = jnp.maximum(m, jnp.max(s, axis=-1, keepdims=True))
        alpha = jnp.exp(m - m_new)
        p = jnp.exp(s - m_new)
        l = alpha * l + jnp.sum(p, axis=-1, keepdims=True)
        acc = alpha * acc + jnp.dot(p.astype(BF16), ck, preferred_element_type=F32)
        return m_new, l, acc

    init = (jnp.full((H, 1), NEG_BIG, F32), jnp.zeros((H, 1), F32), jnp.zeros((H, KV_LORA), F32))
    m, l, acc = lax.fori_loop(0, n_groups, body, init)

    cn = cn_ref[0].astype(BF16).astype(F32)
    krn = krn_ref[0].astype(BF16).astype(F32)
    s_n = (jnp.sum(qlat.astype(F32) * cn, axis=-1, keepdims=True)
           + jnp.sum(q_rope.astype(F32) * krn, axis=-1, keepdims=True))
    m_f = jnp.maximum(m, s_n)
    a_f = jnp.exp(m - m_f)
    p_n = jnp.exp(s_n - m_f)
    l_f = a_f * l + p_n
    o_lat = (a_f * acc + p_n.astype(BF16).astype(F32) * cn) / l_f
    o_all = jnp.dot(o_lat.astype(BF16), wuv_ref[...], preferred_element_type=F32)
    o_ref[0] = jnp.sum(jnp.where(cc // LANES == rr, o_all, 0.0), axis=0, keepdims=True).astype(BF16)


def _sample_attention(page_table, q, ckv_new, kr_new, wuk_s, wuv_c, cache_ckv, cache_krope_t, layer):
    Bd = q.shape[0]
    H = MLA_HEADS
    n_pages = page_table.shape[1]
    assert n_pages % DEC_GROUP == 0 and Bd * (n_pages // DEC_GROUP) >= DEC_SLOTS
    q3 = q.reshape(Bd, H, LANES)
    grid_spec = pltpu.PrefetchScalarGridSpec(
        num_scalar_prefetch=1,
        grid=(Bd,),
        in_specs=[pl.BlockSpec((1, H, LANES), lambda b, pt: (b, 0, 0)),
                  pl.BlockSpec((1, 1, KV_LORA), lambda b, pt: (b, 0, 0)),
                  pl.BlockSpec((1, 1, QK_ROPE), lambda b, pt: (b, 0, 0)),
                  _lspec((H * LANES, KV_LORA), layer), _lspec((KV_LORA, H * V_HEAD), layer),
                  pl.BlockSpec(memory_space=pl.ANY), pl.BlockSpec(memory_space=pl.ANY)],
        out_specs=pl.BlockSpec((1, 1, H * V_HEAD), lambda b, pt: (b, 0, 0)),
        scratch_shapes=[pltpu.VMEM((DEC_SLOTS, DEC_GROUP * PAGE_SIZE, KV_LORA), F32),
                        pltpu.VMEM((DEC_SLOTS, QK_ROPE, DEC_GROUP * PAGE_SIZE), F32),
                        pltpu.SemaphoreType.DMA((2, DEC_SLOTS))],
    )
    return pl.pallas_call(
        functools.partial(_sattn_kernel, layer=layer, n_pages=n_pages),
        grid_spec=grid_spec,
        out_shape=jax.ShapeDtypeStruct((Bd, 1, H * V_HEAD), BF16),
        name="mla_decode",
        compiler_params=_cparams(("arbitrary",)),
    )(page_table, q3, ckv_new, kr_new, wuk_s, wuv_c, cache_ckv, cache_krope_t)


def _merge_kernel(x_ref, org_ref, ogdn_ref, omla_ref, ga_ref, gb_ref, gc_ref, gt_ref,
                  wrg_ref, wgdn_ref, wmla_ref, wo_ref, o_ref):
    d = functools.partial(jnp.dot, preferred_element_type=F32)
    m = (_sigmoid(ga_ref[0]) * d(org_ref[0], wrg_ref[...])
         + _sigmoid(gb_ref[0]) * d(ogdn_ref[0], wgdn_ref[...])
         + _sigmoid(gc_ref[0]) * d(omla_ref[0], wmla_ref[...]))
    o_ref[0] = x_ref[0] + gt_ref[...] * d(m.astype(BF16), wo_ref[...])


def _merge(x, o_rg, o_gdn, o_mla, u, mod, p, l, bm):
    B, T, D = x.shape
    tok = lambda w, col: pl.BlockSpec((1, bm, w), lambda b, i: (b, i, col))
    return pl.pallas_call(
        _merge_kernel,
        grid=(B, T // bm),
        in_specs=[tok(D, 0), tok(D_RNN, 0), tok(GDN_VW, 0), tok(D, 0),
                  tok(D, U_GA // D), tok(D, U_GB // D), tok(D, U_GC // D),
                  mod.spec(MOD_GT1, bm),
                  _lspec((D_RNN, D), l), _lspec((GDN_VW, D), l), _lspec((D, D), l), _lspec((D, D), l)],
        out_specs=tok(D, 0),
        out_shape=jax.ShapeDtypeStruct((B, T, D), F32),
        name="branch_merge",
        compiler_params=_cparams(("parallel", "parallel")),
    )(x, o_rg, o_gdn, o_mla, u, u, u, mod.array, p["wrg"], p["wgdn"], p["wmla"], p["wo"])


def _ffn_kernel(x_ref, sc_ref, sh_ref, gt_ref, g_ref, gf_ref, wg_ref, wu_ref, wd_ref, o_ref, hb_ref, acc_ref, *, final):
    j = pl.program_id(2)

    @pl.when(j == 0)
    def _():
        h = _rms(x_ref[0], g_ref[...]) * (1.0 + sc_ref[...]) + sh_ref[...]
        hb_ref[...] = h.astype(BF16)
        acc_ref[...] = jnp.zeros(acc_ref.shape, F32)

    hb = hb_ref[...]
    gate = jnp.dot(hb, wg_ref[...], preferred_element_type=F32)
    up = jnp.dot(hb, wu_ref[...], preferred_element_type=F32)
    acc_ref[...] += jnp.dot((_silu(gate) * up).astype(BF16), wd_ref[...], preferred_element_type=F32)

    @pl.when(j == pl.num_programs(2) - 1)
    def _():
        y = x_ref[0] + gt_ref[...] * acc_ref[...]
        o_ref[0] = _rms(y, gf_ref[...]) if final else y


def _ffn(x, mod, g, g_final, w_in_b, w_out_b, l, bm, bf, final):
    B, T, D = x.shape
    nf = D_FF // bf
    return pl.pallas_call(
        functools.partial(_ffn_kernel, final=final),
        grid=(B, T // bm, nf),
        in_specs=[pl.BlockSpec((1, bm, D), lambda b, i, j: (b, i, 0)),
                  mod.spec(MOD_SC2, bm), mod.spec(MOD_SH2, bm), mod.spec(MOD_GT2, bm),
                  _lspec((1, D), l),
                  pl.BlockSpec((1, D), lambda b, i, j: (0, 0)),
                  pl.BlockSpec((None, D, bf), lambda b, i, j: (l, 0, j)),
                  pl.BlockSpec((None, D, bf), lambda b, i, j: (l, 0, nf + j)),
                  pl.BlockSpec((None, bf, D), lambda b, i, j: (l, j, 0))],
        out_specs=pl.BlockSpec((1, bm, D), lambda b, i, j: (b, i, 0)),
        out_shape=jax.ShapeDtypeStruct((B, T, D), F32),
        scratch_shapes=[pltpu.VMEM((bm, D), BF16), pltpu.VMEM((bm, D), F32)],
        name="swiglu",
        compiler_params=_cparams(("parallel", "parallel", "arbitrary")),
    )(x, mod.array, mod.array, mod.array, g, g_final, w_in_b, w_in_b, w_out_b)


def _rot_half(w):
    half = QK_ROPE // 2
    return jnp.concatenate([-w[..., half:], w[..., :half]], axis=-1)


def _pack_w_in(w):
    L, D, _ = w.shape
    wt = jnp.swapaxes(w, 1, 2)
    offs = np.cumsum((0,) + IN_SIZES)
    rx, ry, qkv, z, a, b, mq, mkv, gate = [wt[:, offs[i]:offs[i + 1]] for i in range(len(IN_SIZES))]
    zeros = lambda n: jnp.zeros((L, n, D), w.dtype)
    kr = mkv[:, KV_LORA:]
    half = QK_ROPE // 2
    kr_rot = jnp.concatenate([-kr[:, half:], kr[:, :half]], axis=1)
    tail = LANES - ROPE_LANE0 - QK_ROPE
    rows = [rx, ry, zeros(U_QKV - 2 * D_RNN), qkv, z, gate,
            a, b, zeros(W_AB - 2 * GDN_HEADS), mq,
            mkv[:, :KV_LORA], zeros(ROPE_LANE0), kr, zeros(tail), zeros(ROPE_LANE0), kr_rot, zeros(tail)]
    out = jnp.concatenate(rows, axis=1).astype(BF16)
    assert out.shape[1] == N_U
    return out


def _pack_w_uq(w):
    H = MLA_HEADS
    L = w.shape[0]
    w = w.reshape(L, Q_LORA, H, QK_NOPE + QK_ROPE)
    nope, rope = w[..., :QK_NOPE], w[..., QK_NOPE:]
    tail = jnp.zeros((L, Q_LORA, H, LANES - QK_NOPE - QK_ROPE), w.dtype)
    a = jnp.concatenate([nope, rope, tail], axis=-1).reshape(L, Q_LORA, H * LANES)
    b = jnp.concatenate([jnp.zeros_like(nope), _rot_half(rope), tail], axis=-1).reshape(L, Q_LORA, H * LANES)
    return jnp.concatenate([a, b], axis=-1).astype(BF16)


def _pack_w_ukv(w):
    H = MLA_HEADS
    L = w.shape[0]
    w3 = w.reshape(L, KV_LORA, H, QK_NOPE + V_HEAD)
    w_uk, w_uv = w3[..., :QK_NOPE], w3[..., QK_NOPE:]
    kpad = jnp.concatenate([w_uk, jnp.zeros((L, KV_LORA, H, LANES - QK_NOPE), w.dtype)], axis=-1)
    wuv_c = w_uv.reshape(L, KV_LORA, H * V_HEAD)
    wkv_p = jnp.concatenate([kpad.reshape(L, KV_LORA, H * LANES), wuv_c], axis=-1)
    wuk_s = jnp.transpose(kpad, (0, 2, 3, 1)).reshape(L, H * LANES, KV_LORA)
    return wkv_p.astype(BF16), wuk_s.astype(BF16), wuv_c.astype(BF16)


def _rope_tables(T, pos0):
    inv = ROPE_BASE ** (-jnp.arange(0, QK_ROPE, 2, dtype=F32) / QK_ROPE)
    ang = (jnp.arange(T, dtype=F32) + pos0)[:, None] * inv[None, :]
    cos, sin = jnp.cos(ang), jnp.sin(ang)
    tail = jnp.zeros((T, LANES - ROPE_LANE0 - QK_ROPE), F32)
    cos_t = jnp.concatenate([jnp.ones((T, ROPE_LANE0), F32), cos, cos, tail], axis=1)
    sin_t = jnp.concatenate([jnp.zeros((T, ROPE_LANE0), F32), sin, sin, tail], axis=1)
    return cos_t[None], sin_t[None]


def kernel(x_prompt, x_sample, cache_ckv, cache_krope, state_rg_conv, state_rg_h, state_gdn_conv, state_gdn_S,
           page_table, c_prompt, c_sample, w_ada, b_ada, g_norm1, g_norm2, w_in, rg_conv_w, rg_conv_b, rg_wa,
           rg_ba, rg_wx, rg_bx, rg_lambda, gdn_conv_w, gdn_A_log, gdn_dt_bias, gdn_norm_g, mla_q_norm_g, w_uq,
           mla_kv_norm_g, w_ukv, w_rg_proj, w_gdn_proj, w_mla_proj, w_o, w_ffn_in, w_ffn_out, g_final):
    L = w_in.shape[0]
    Bp, T, D = x_prompt.shape
    Bd = x_sample.shape[0]
    past_len = page_table.shape[1] * PAGE_SIZE

    mod = _modulation(jnp.concatenate([c_sample, c_prompt], axis=0), w_ada, b_ada)
    cos_p, sin_p = _rope_tables(T, 0.0)
    cos_s, sin_s = _rope_tables(1, float(past_len))
    cos_s = jnp.broadcast_to(cos_s, (1, Bd, LANES))
    sin_s = jnp.broadcast_to(sin_s, (1, Bd, LANES))

    row = lambda v: v.reshape(L, 1, v.shape[-1])
    lane_row = lambda v: jnp.pad(v, ((0, 0), (0, LANES - v.shape[-1]))).reshape(L, 1, LANES)
    pad8 = lambda s: jnp.pad(s, ((0, 0), (0, 0), (SUBLANES - (CONV_W - 1), 0), (0, 0)))
    g1, g2, gf = row(g_norm1), row(g_norm2), g_final.reshape(1, D)
    w_in_p = _pack_w_in(w_in)
    wkv_p, wuk_s, wuv_c = _pack_w_ukv(w_ukv)
    rg_p = dict(cw=rg_conv_w, cb=row(rg_conv_b), wa=rg_wa.astype(BF16), ba=row(rg_ba),
                wx=rg_wx.astype(BF16), bx=row(rg_bx), lam=row(rg_lambda))
    gdn_p = dict(cw=gdn_conv_w, alog=lane_row(gdn_A_log), dtb=lane_row(gdn_dt_bias), ng=row(gdn_norm_g))
    mla_p = dict(gq=row(mla_q_norm_g), gkv=row(mla_kv_norm_g), wq=_pack_w_uq(w_uq), wkv=wkv_p)
    mrg_p = dict(wrg=w_rg_proj.astype(BF16), wgdn=w_gdn_proj.astype(BF16), wmla=w_mla_proj.astype(BF16),
                 wo=w_o.astype(BF16))
    wfi, wfo = w_ffn_in.astype(BF16), w_ffn_out.astype(BF16)
    rg_buf_s, gdn_buf_s = pad8(state_rg_conv), pad8(state_gdn_conv)
    rg_h_s = state_rg_h.reshape(L, Bd, 1, D_RNN)
    rg_buf_0 = jnp.zeros((1, Bp, SUBLANES, D_RNN), F32)
    rg_h_0 = jnp.zeros((1, Bp, 1, D_RNN), F32)
    gdn_buf_0 = jnp.zeros((1, Bp, SUBLANES, GDN_CONV_C), F32)
    cache_krope_t = jnp.swapaxes(cache_krope, 2, 3)

    xp = x_prompt
    xs = x_sample.reshape(1, Bd, D)
    outs_p, outs_s = [], []
    for l in range(L):
        final = l == L - 1

        m_p = _Mod(mod, l, per_token=False, row0=Bd)
        u = _in_proj(xp, m_p, g1, w_in_p, l, bm=min(1024, T), bn=1024)
        o_rg, rg_buf, rg_h = _rglru(u, rg_buf_0, rg_h_0, 0, rg_p, l, tt=128)
        o_gdn, gdn_buf, gdn_S = _gdn_prompt(u, gdn_buf_0, gdn_p, l, tt=256)
        q, k, v, ckv, kr = _mla_prep(u, cos_p, sin_p, mla_p, l, bm=512)
        o_mla = _flash(q, k, v, tq=min(512, T))
        x1 = _merge(xp, o_rg, o_gdn, o_mla, u, m_p, mrg_p, l, bm=256)
        xp = _ffn(x1, m_p, g2, gf, wfi, wfo, l, bm=512, bf=1408, final=final)
        outs_p.append((ckv, kr, rg_buf, rg_h.reshape(Bp, D_RNN), gdn_buf, gdn_S))

        m_s = _Mod(mod, l, per_token=True, row0=0)
        us = _in_proj(xs, m_s, g1, w_in_p, l, bm=Bd, bn=1024)
        us_seq = us.reshape(Bd, 1, N_U)
        o_rg, rg_buf, rg_h = _rglru(us_seq, rg_buf_s, rg_h_s, l, rg_p, l, tt=1)
        o_gdn, gdn_buf, gdn_S = _gdn_step(us_seq, gdn_buf_s, state_gdn_S, gdn_p, l)
        q, k, v, ckv, kr = _mla_prep(us, cos_s, sin_s, mla_p, l, bm=Bd)
        o_mla = _sample_attention(page_table, q.reshape(Bd, MLA_HEADS * LANES), ckv.reshape(Bd, 1, KV_LORA),
                                  kr.reshape(Bd, 1, QK_ROPE), wuk_s, wuv_c, cache_ckv, cache_krope_t, l)
        x1 = _merge(xs, o_rg.reshape(1, Bd, D_RNN), o_gdn.reshape(1, Bd, GDN_VW), o_mla.reshape(1, Bd, D),
                    us, m_s, mrg_p, l, bm=Bd)
        xs = _ffn(x1, m_s, g2, gf, wfi, wfo, l, bm=Bd, bf=1408, final=final)
        outs_s.append((ckv.reshape(Bd, 1, KV_LORA), kr.reshape(Bd, 1, QK_ROPE), rg_buf, rg_h.reshape(Bd, D_RNN),
                       gdn_buf, gdn_S))

    stack = lambda outs, i: jnp.stack([o[i] for o in outs])
    return (xp, xs.reshape(Bd, 1, D),
            stack(outs_p, 0), stack(outs_p, 1), stack(outs_p, 2), stack(outs_p, 3), stack(outs_p, 4), stack(outs_p, 5),
            stack(outs_s, 0), stack(outs_s, 1), stack(outs_s, 2), stack(outs_s, 3), stack(outs_s, 4), stack(outs_s, 5))
```

```python
import functools
import math

import numpy as np
import jax
import jax.numpy as jnp
from jax import lax
from jax.experimental import pallas as pl
from jax.experimental.pallas import tpu as pltpu

F32 = jnp.float32
BF16 = jnp.bfloat16

D_MODEL = 1024
CONV_W = 4
D_RNN = 1280
RG_BLOCK = 128
RG_BLOCKS = D_RNN // RG_BLOCK
RG_C = 8.0
GDN_HEADS = 8
GDN_DK = 128
GDN_DV = 128
GDN_KW = GDN_HEADS * GDN_DK
GDN_VW = GDN_HEADS * GDN_DV
GDN_CONV_C = 2 * GDN_KW + GDN_VW
GDN_CHUNK = 64
MLA_HEADS = 8
Q_LORA = 384
KV_LORA = 256
QK_NOPE = 64
QK_ROPE = 32
V_HEAD = 128
MLA_SCALE = (QK_NOPE + QK_ROPE) ** -0.5
ROPE_BASE = 10000.0
D_FF = 2816
N_BRANCH = 3
IN_SIZES = (D_RNN, D_RNN, GDN_CONV_C, GDN_VW, GDN_HEADS, GDN_HEADS, Q_LORA, KV_LORA + QK_ROPE, N_BRANCH * D_MODEL)
EPS = 1e-6
PAGE_SIZE = 128

LANES = 128
SUBLANES = 8
VMEM_LIMIT = 56 * 1024 * 1024

U_RX = 0
U_RY = D_RNN
U_QKV = 3072
U_Z = 6144
U_GA = 7168
U_GB = 8192
U_GC = 9216
U_AB = 10240
U_MQ = 10368
U_MKV = 10752
N_U = 11264
W_AB = LANES
W_MKV = 512
ROPE_LANE0 = QK_NOPE


def _cparams(sem):
    return pltpu.CompilerParams(dimension_semantics=sem, vmem_limit_bytes=VMEM_LIMIT)


def _sigmoid(x):
    return 1.0 / (1.0 + jnp.exp(-x))


def _silu(x):
    return x * _sigmoid(x)


def _softplus(x):
    return jnp.maximum(x, 0.0) + jnp.log1p(jnp.exp(-jnp.abs(x)))


def _gelu_tanh(x):
    c = math.sqrt(2.0 / math.pi)
    return 0.5 * x * (1.0 + jnp.tanh(c * (x + 0.044715 * (x * x * x))))


def _rms(x, g):
    return x * lax.rsqrt(jnp.mean(x * x, axis=-1, keepdims=True) + EPS) * g


def _dot(a, b):
    return jnp.dot(a.astype(BF16), b.astype(BF16), preferred_element_type=F32)


def _dot_nt(a, b):
    return lax.dot_general(a.astype(BF16), b.astype(BF16), (((1,), (1,)), ((), ())), preferred_element_type=F32)


def _dot_tn(a, b):
    return lax.dot_general(a.astype(BF16), b.astype(BF16), (((0,), (0,)), ((), ())), preferred_element_type=F32)


def _split2(a):
    hi = a.astype(BF16)
    lo = (a - hi.astype(F32)).astype(BF16)
    return hi, lo


def _dot3(a, b):
    ah, al = _split2(a)
    bh, bl = _split2(b)
    d = functools.partial(jnp.dot, preferred_element_type=F32)
    return d(ah, bh) + (d(ah, bl) + d(al, bh))


def _dot_exact_lhs(a_bf16, b):
    b0 = b.astype(BF16)
    r1 = b - b0.astype(F32)
    b1 = r1.astype(BF16)
    b2 = (r1 - b1.astype(F32)).astype(BF16)
    d = functools.partial(jnp.dot, preferred_element_type=F32)
    return d(a_bf16, b0) + (d(a_bf16, b1) + d(a_bf16, b2))


def _mod_kernel(c_ref, w_ref, b_ref, o_ref):
    c = c_ref[...]
    o_ref[0] = _dot(_silu(c), w_ref[0]) + b_ref[0]


def _modulation(c_all, w_ada, b_ada):
    L, D, N = w_ada.shape
    R = c_all.shape[0]
    bn = 1536
    return pl.pallas_call(
        _mod_kernel,
        grid=(L, N // bn),
        in_specs=[pl.BlockSpec((R, D), lambda l, j: (0, 0)),
                  pl.BlockSpec((1, D, bn), lambda l, j: (l, 0, j)),
                  pl.BlockSpec((1, 1, bn), lambda l, j: (l, 0, j))],
        out_specs=pl.BlockSpec((1, R, bn), lambda l, j: (l, 0, j)),
        out_shape=jax.ShapeDtypeStruct((L, R, N), F32),
        name="adaln_mod",
        compiler_params=_cparams(("parallel", "parallel")),
    )(c_all, w_ada, b_ada.reshape(L, 1, N))


MOD_SH1, MOD_SC1, MOD_GT1, MOD_SH2, MOD_SC2, MOD_GT2 = range(6)


class _Mod:
    def __init__(self, mod, layer, per_token, row0):
        self.layer, self.per_token, self.row0 = layer, per_token, row0
        L, R, N = mod.shape
        self.array = mod if per_token else mod.reshape(L, R, 1, N)

    def spec(self, part, bm):
        l, r0 = self.layer, self.row0
        if self.per_token:
            assert r0 == 0
            return pl.BlockSpec((None, bm, D_MODEL), lambda b, i, *_: (l, i, part))
        return pl.BlockSpec((None, None, 1, D_MODEL), lambda b, i, *_: (l, r0 + b, 0, part))


def _lspec(shape, l):
    return pl.BlockSpec((None,) + tuple(shape), lambda *_: (l,) + (0,) * len(shape))


def _in_kernel(x_ref, sc_ref, sh_ref, g_ref, w_ref, o_ref, hb_ref):
    @pl.when(pl.program_id(2) == 0)
    def _():
        h = _rms(x_ref[0], g_ref[...]) * (1.0 + sc_ref[...]) + sh_ref[...]
        hb_ref[...] = h.astype(BF16)

    o_ref[0] = lax.dot_general(hb_ref[...], w_ref[...], (((1,), (1,)), ((), ())), preferred_element_type=F32)


def _in_proj(x, mod, g, w_packed, l, bm, bn):
    B, T, D = x.shape
    N = w_packed.shape[1]
    return pl.pallas_call(
        _in_kernel,
        grid=(B, T // bm, N // bn),
        in_specs=[pl.BlockSpec((1, bm, D), lambda b, i, j: (b, i, 0)),
                  mod.spec(MOD_SC1, bm), mod.spec(MOD_SH1, bm),
                  _lspec((1, D), l),
                  pl.BlockSpec((None, bn, D), lambda b, i, j: (l, j, 0))],
        out_specs=pl.BlockSpec((1, bm, bn), lambda b, i, j: (b, i, j)),
        out_shape=jax.ShapeDtypeStruct((B, T, N), F32),
        scratch_shapes=[pltpu.VMEM((bm, D), BF16)],
        name="in_proj",
        compiler_params=_cparams(("parallel", "parallel", "arbitrary")),
    )(x, mod.array, mod.array, g, w_packed)


SCAN_PAD = 64


def _rg_kernel(ux_ref, uy_ref, buf_ref, h0_ref, cw_ref, cb_ref, wa_ref, ba_ref, wx_ref, bx_ref, lam_ref,
               o_ref, nbuf_ref, hl_ref, xbuf, abuf, bbuf, hc, *, tt):
    i = pl.program_id(1)
    nt = pl.num_programs(1)

    @pl.when(i == 0)
    def _():
        xbuf[0:SUBLANES, :] = buf_ref[0]
        hc[...] = h0_ref[0]
        abuf[0:SCAN_PAD, :] = jnp.ones((SCAN_PAD, D_RNN), F32)
        bbuf[0:SCAN_PAD, :] = jnp.zeros((SCAN_PAD, D_RNN), F32)

    xbuf[SUBLANES:SUBLANES + tt, :] = ux_ref[0]
    lo = SUBLANES - (CONV_W - 1)
    if tt >= SUBLANES:
        xs = xbuf[...]
        xc = cb_ref[...] + cw_ref[CONV_W - 1:CONV_W, :] * xs[SUBLANES:, :]
        for j in range(CONV_W - 1):
            xc = xc + cw_ref[j:j + 1, :] * pltpu.roll(xs, CONV_W - 1 - j, axis=0)[SUBLANES:, :]
    else:
        xc = cb_ref[...] + cw_ref[0:1, :] * xbuf[lo:lo + tt, :]
        for j in range(1, CONV_W):
            xc = xc + cw_ref[j:j + 1, :] * xbuf[lo + j:lo + j + tt, :]

    @pl.when(i == nt - 1)
    def _():
        nbuf_ref[0] = xbuf[tt + lo:tt + SUBLANES, :]

    if tt >= SUBLANES:
        xbuf[0:SUBLANES, :] = xbuf[tt:tt + SUBLANES, :]

    xb = xc.astype(BF16)
    ra = jnp.concatenate([jnp.dot(xb[:, n * RG_BLOCK:(n + 1) * RG_BLOCK], wa_ref[n], preferred_element_type=F32)
                          for n in range(RG_BLOCKS)], axis=1)
    ri = jnp.concatenate([jnp.dot(xb[:, n * RG_BLOCK:(n + 1) * RG_BLOCK], wx_ref[n], preferred_element_type=F32)
                          for n in range(RG_BLOCKS)], axis=1)
    r = _sigmoid(ra + ba_ref[...])
    ig = _sigmoid(ri + bx_ref[...])
    log_a = (-RG_C) * r * _softplus(-lam_ref[...])
    a = jnp.exp(log_a)
    b = jnp.sqrt(-jnp.tanh(log_a) * (a * a + 1.0)) * (ig * xc)

    abuf[SCAN_PAD:SCAN_PAD + tt, :] = a
    bbuf[SCAN_PAD:SCAN_PAD + tt, :] = b
    d = 1
    while d < tt:
        a_s = abuf[SCAN_PAD - d:SCAN_PAD - d + tt, :]
        b_s = bbuf[SCAN_PAD - d:SCAN_PAD - d + tt, :]
        a0 = abuf[SCAN_PAD:SCAN_PAD + tt, :]
        b0 = bbuf[SCAN_PAD:SCAN_PAD + tt, :]
        abuf[SCAN_PAD:SCAN_PAD + tt, :] = a0 * a_s
        bbuf[SCAN_PAD:SCAN_PAD + tt, :] = a0 * b_s + b0
        d *= 2
    h = bbuf[SCAN_PAD:SCAN_PAD + tt, :] + abuf[SCAN_PAD:SCAN_PAD + tt, :] * hc[...]
    hc[...] = h[tt - 1:tt, :]
    o_ref[0] = (h * _gelu_tanh(uy_ref[0])).astype(BF16)

    @pl.when(i == nt - 1)
    def _():
        hl_ref[0] = h[tt - 1:tt, :]


def _rglru(u, buf8, h0, ls, p, l, tt):
    B, T, _ = u.shape
    C = D_RNN
    wblk = (RG_BLOCKS, RG_BLOCK, RG_BLOCK)
    return pl.pallas_call(
        functools.partial(_rg_kernel, tt=tt),
        grid=(B, T // tt),
        in_specs=[pl.BlockSpec((1, tt, C), lambda b, i: (b, i, U_RX // C)),
                  pl.BlockSpec((1, tt, C), lambda b, i: (b, i, U_RY // C)),
                  pl.BlockSpec((None, 1, SUBLANES, C), lambda b, i: (ls, b, 0, 0)),
                  pl.BlockSpec((None, 1, 1, C), lambda b, i: (ls, b, 0, 0)),
                  _lspec((CONV_W, C), l), _lspec((1, C), l),
                  _lspec(wblk, l), _lspec((1, C), l),
                  _lspec(wblk, l), _lspec((1, C), l), _lspec((1, C), l)],
        out_specs=[pl.BlockSpec((1, tt, C), lambda b, i: (b, i, 0)),
                   pl.BlockSpec((1, CONV_W - 1, C), lambda b, i: (b, 0, 0)),
                   pl.BlockSpec((1, 1, C), lambda b, i: (b, 0, 0))],
        out_shape=[jax.ShapeDtypeStruct((B, T, C), BF16),
                   jax.ShapeDtypeStruct((B, CONV_W - 1, C), F32),
                   jax.ShapeDtypeStruct((B, 1, C), F32)],
        scratch_shapes=[pltpu.VMEM((tt + SUBLANES, C), F32),
                        pltpu.VMEM((SCAN_PAD + tt, C), F32),
                        pltpu.VMEM((SCAN_PAD + tt, C), F32),
                        pltpu.VMEM((1, C), F32)],
        name="rglru",
        compiler_params=_cparams(("parallel", "arbitrary")),
    )(u, u, buf8, h0, p["cw"], p["cb"], p["wa"], p["ba"], p["wx"], p["bx"], p["lam"])


TRI_BASE = 8
INV_GROUP = 8
NEWTON_STEPS = 2


def _cat_dot3(x, y, bd_b, n):
    C = x.shape[0]
    xh, xl = _split2(x)
    yh, yl = _split2(y)
    d = functools.partial(jnp.dot, preferred_element_type=F32)
    r = d(jnp.concatenate([xh, xl], axis=0), jnp.concatenate([yh] * n, axis=0) * bd_b)
    return r[:C] + (r[C:] + d(xh, jnp.concatenate([yl] * n, axis=0) * bd_b))


def _cat_dot1(x, y, bd_b, n):
    yb = jnp.concatenate([y.astype(BF16)] * n, axis=0) * bd_b
    return jnp.dot(x.astype(BF16), yb, preferred_element_type=F32)


def _tri_inv_cat(Lcs, masks, bd_b, n):
    eye, base_mask, level_masks = masks
    mm = functools.partial(_cat_dot1, bd_b=bd_b, n=n)
    mm3 = functools.partial(_cat_dot3, bd_b=bd_b, n=n)
    N = [jnp.where(base_mask, -Lc, 0.0) for Lc in Lcs]
    P = [eye + x for x in N]
    N2 = [mm(x, x) for x in N]
    P = [p + mm(p, x) for p, x in zip(P, N2)]
    N4 = [mm(x, x) for x in N2]
    P = [p + mm(p, x) for p, x in zip(P, N4)]
    for m in level_masks:
        PO = [mm(p, jnp.where(m, Lc, 0.0)) for p, Lc in zip(P, Lcs)]
        P = [p - mm(po, p) for p, po in zip(P, PO)]
    for _ in range(NEWTON_STEPS):
        R = [eye - p - mm3(Lc, p) for p, Lc in zip(P, Lcs)]
        P = [p + mm(p, r) for p, r in zip(P, R)]
    return P


def _cat_masks(C, n):
    rr = lax.broadcasted_iota(jnp.int32, (C, n * C), 0)
    jj = lax.broadcasted_iota(jnp.int32, (C, n * C), 1) % C
    same = lambda s: (rr // s) == (jj // s)
    eye = (rr == jj).astype(F32)
    levels = []
    s = TRI_BASE
    while s < C:
        levels.append(jnp.logical_and(same(2 * s), jnp.logical_not(same(s))))
        s *= 2
    return eye, same(TRI_BASE), levels


def _gdn_kernel(qkv_ref, z_ref, ab_ref, buf_ref, cw_ref, alog_ref, dtb_ref, ng_ref,
                o_ref, nbuf_ref, S_ref, xbuf, act, S_sc, wq_sc, uv_sc, qk_sc, kd_sc, gl_sc, lc_sc, rhs_sc, *, tt):
    i = pl.program_id(1)
    nt = pl.num_programs(1)
    C = GDN_CHUNK
    H = GDN_HEADS
    n = tt // C

    @pl.when(i == 0)
    def _():
        xbuf[0:SUBLANES, :] = buf_ref[0]
        S_sc[...] = jnp.zeros(S_sc.shape, F32)

    xbuf[SUBLANES:SUBLANES + tt, :] = qkv_ref[0]
    lo = SUBLANES - (CONV_W - 1)
    xs = xbuf[...]
    y = cw_ref[CONV_W - 1:CONV_W, :] * xs[SUBLANES:, :]
    for j in range(CONV_W - 1):
        y = y + cw_ref[j:j + 1, :] * pltpu.roll(xs, CONV_W - 1 - j, axis=0)[SUBLANES:, :]
    act[...] = _silu(y)

    @pl.when(i == nt - 1)
    def _():
        nbuf_ref[0] = xbuf[tt + lo:tt + SUBLANES, :]

    xbuf[0:SUBLANES, :] = xbuf[tt:tt + SUBLANES, :]

    ab = ab_ref[0]
    g = -jnp.exp(alog_ref[...]) * _softplus(ab + dtb_ref[...])
    beta = _sigmoid(ab)

    ri = lax.broadcasted_iota(jnp.int32, (tt, tt), 0)
    ci = lax.broadcasted_iota(jnp.int32, (tt, tt), 1)
    bd = (ri // C) == (ci // C)
    bd_incl = jnp.logical_and(bd, ri >= ci)
    bd_strict = jnp.logical_and(bd, ri > ci)
    bd_b = bd.astype(BF16)
    last_b = jnp.logical_and(bd, ci % C == C - 1).astype(BF16)
    gcum = _dot_exact_lhs(bd_incl.astype(BF16), g)
    glast = _dot_exact_lhs(last_b, gcum)
    gl_sc[...] = jnp.exp(glast)
    gcum_t = gcum.T
    masks = _cat_masks(C, n)
    ng = ng_ref[...]

    for h in range(H):
        q = act[:, h * GDN_DK:(h + 1) * GDN_DK]
        k = act[:, GDN_KW + h * GDN_DK:GDN_KW + (h + 1) * GDN_DK]
        v = act[:, 2 * GDN_KW + h * GDN_DV:2 * GDN_KW + (h + 1) * GDN_DV]
        q = q * lax.rsqrt(jnp.sum(q * q, axis=-1, keepdims=True) + EPS) * (GDN_DK ** -0.5)
        k = k * lax.rsqrt(jnp.sum(k * k, axis=-1, keepdims=True) + EPS)
        gc = gcum[:, h:h + 1]
        gr = gcum_t[h:h + 1, :]
        e = jnp.exp(jnp.where(bd_incl, gc - gr, -jnp.inf))
        bt = beta[:, H + h:H + h + 1]
        kb = k.astype(BF16)
        qb = q.astype(BF16)
        Lf = jnp.where(bd_strict, bt * _dot_nt(kb, kb) * e, 0.0)
        qkd = (_dot_nt(qb, kb) * e).astype(BF16)
        for c in range(n):
            qk_sc[h, c] = qkd[c * C:(c + 1) * C, c * C:(c + 1) * C]
        Lc = Lf[0:C]
        for c in range(1, n):
            Lc = Lc + Lf[c * C:(c + 1) * C]
        lc_sc[h] = Lc
        eg = jnp.exp(gc)
        rhs_sc[h, :, 0:GDN_DV] = bt * v
        rhs_sc[h, :, GDN_DV:] = (bt * eg) * k
        qe = (eg * q).astype(BF16)
        for c in range(n):
            wq_sc[h, c, C:2 * C, :] = qe[c * C:(c + 1) * C]
        kd_sc[h] = (k * jnp.exp(glast[:, h:h + 1] - gc)).astype(BF16)

    for h0 in range(0, H, INV_GROUP):
        hs = range(h0, h0 + INV_GROUP)
        t_cats = _tri_inv_cat([lc_sc[h] for h in hs], masks, bd_b, n)
        for h, t_cat in zip(hs, t_cats):
            t_bd = jnp.where(bd, jnp.concatenate([t_cat] * n, axis=0), 0.0)
            uw = _dot3(t_bd, rhs_sc[h])
            uv_sc[h] = uw[:, :GDN_DV]
            w = uw[:, GDN_DV:].astype(BF16)
            for c in range(n):
                wq_sc[h, c, 0:C, :] = w[c * C:(c + 1) * C]

    def chunk(c, carry):
        r0 = pl.multiple_of(c * C, C)
        rows = pl.ds(r0, C)
        gl = gl_sc[pl.ds(r0, 1), :]
        S = [S_sc[h] for h in range(H)]
        wq = [jnp.dot(wq_sc[h, c], S[h].astype(BF16), preferred_element_type=F32) for h in range(H)]
        Ub = [(uv_sc[h, rows, :] - wq[h][:C]).astype(BF16) for h in range(H)]
        o = [wq[h][C:] + jnp.dot(qk_sc[h, c], Ub[h], preferred_element_type=F32) for h in range(H)]
        for h in range(H):
            S_sc[h] = gl[:, h:h + 1] * S[h] + _dot_tn(kd_sc[h, rows, :], Ub[h])
        for h in range(H):
            zz = z_ref[0, rows, h * GDN_DV:(h + 1) * GDN_DV]
            o_ref[0, rows, h * GDN_DV:(h + 1) * GDN_DV] = (_rms(o[h], ng) * _silu(zz)).astype(BF16)
        return carry

    lax.fori_loop(0, n, chunk, 0)

    @pl.when(i == nt - 1)
    def _():
        S_ref[0] = S_sc[...]


def _gdn_prompt(u, buf8, p, l, tt):
    B, T, _ = u.shape
    Cc = GDN_CONV_C
    H = GDN_HEADS
    return pl.pallas_call(
        functools.partial(_gdn_kernel, tt=tt),
        grid=(B, T // tt),
        in_specs=[pl.BlockSpec((1, tt, Cc), lambda b, i: (b, i, U_QKV // Cc)),
                  pl.BlockSpec((1, tt, GDN_VW), lambda b, i: (b, i, U_Z // GDN_VW)),
                  pl.BlockSpec((1, tt, W_AB), lambda b, i: (b, i, U_AB // W_AB)),
                  pl.BlockSpec((None, 1, SUBLANES, Cc), lambda b, i: (0, b, 0, 0)),
                  _lspec((CONV_W, Cc), l), _lspec((1, LANES), l), _lspec((1, LANES), l), _lspec((1, GDN_DV), l)],
        out_specs=[pl.BlockSpec((1, tt, GDN_VW), lambda b, i: (b, i, 0)),
                   pl.BlockSpec((1, CONV_W - 1, Cc), lambda b, i: (b, 0, 0)),
                   pl.BlockSpec((1, H, GDN_DK, GDN_DV), lambda b, i: (b, 0, 0, 0))],
        out_shape=[jax.ShapeDtypeStruct((B, T, GDN_VW), BF16),
                   jax.ShapeDtypeStruct((B, CONV_W - 1, Cc), F32),
                   jax.ShapeDtypeStruct((B, H, GDN_DK, GDN_DV), F32)],
        scratch_shapes=[pltpu.VMEM((tt + SUBLANES, Cc), F32),
                        pltpu.VMEM((tt, Cc), F32),
                        pltpu.VMEM((H, GDN_DK, GDN_DV), F32),
                        pltpu.VMEM((H, tt // GDN_CHUNK, 2 * GDN_CHUNK, GDN_DK), BF16),
                        pltpu.VMEM((H, tt, GDN_DV), F32),
                        pltpu.VMEM((H, tt // GDN_CHUNK, GDN_CHUNK, GDN_CHUNK), BF16),
                        pltpu.VMEM((H, tt, GDN_DK), BF16),
                        pltpu.VMEM((tt, LANES), F32),
                        pltpu.VMEM((H, GDN_CHUNK, tt), F32),
                        pltpu.VMEM((H, tt, GDN_DV + GDN_DK), F32)],
        name="gdn_chunked",
        compiler_params=_cparams(("parallel", "arbitrary")),
    )(u, u, u, buf8, p["cw"], p["alog"], p["dtb"], p["ng"])


def _gdn_step_kernel(qkv_ref, z_ref, ab_ref, buf_ref, S0_ref, cw_ref, alog_ref, dtb_ref, ng_ref,
                     o_ref, nbuf_ref, S_ref, xbuf):
    H = GDN_HEADS
    xbuf[0:SUBLANES, :] = buf_ref[0]
    xbuf[SUBLANES:SUBLANES + 1, :] = qkv_ref[0]
    lo = SUBLANES - (CONV_W - 1)
    y = cw_ref[0:1, :] * xbuf[lo:lo + 1, :]
    for j in range(1, CONV_W):
        y = y + cw_ref[j:j + 1, :] * xbuf[lo + j:lo + j + 1, :]
    y = _silu(y)
    nbuf_ref[0] = xbuf[lo + 1:SUBLANES + 1, :]

    ab = ab_ref[0]
    g_all = -jnp.exp(alog_ref[...]) * _softplus(ab + dtb_ref[...])
    beta_all = _sigmoid(ab)
    ii = lax.broadcasted_iota(jnp.int32, (GDN_DK, GDN_DK), 0)
    jj = lax.broadcasted_iota(jnp.int32, (GDN_DK, GDN_DK), 1)
    eye = ii == jj
    ng = ng_ref[...]
    for h in range(H):
        q = y[:, h * GDN_DK:(h + 1) * GDN_DK]
        k = y[:, GDN_KW + h * GDN_DK:GDN_KW + (h + 1) * GDN_DK]
        v = y[:, 2 * GDN_KW + h * GDN_DV:2 * GDN_KW + (h + 1) * GDN_DV]
        q = q * lax.rsqrt(jnp.sum(q * q, axis=-1, keepdims=True) + EPS) * (GDN_DK ** -0.5)
        k = k * lax.rsqrt(jnp.sum(k * k, axis=-1, keepdims=True) + EPS)
        eg = jnp.exp(g_all[:, h:h + 1])
        bt = beta_all[:, H + h:H + h + 1]
        S = S0_ref[0, h]
        Sb = S.astype(BF16)
        kb = k.astype(BF16)
        qb = q.astype(BF16)
        u = bt * (v - eg * jnp.dot(kb, Sb, preferred_element_type=F32))
        ub = u.astype(BF16)
        qk = jnp.sum(qb.astype(F32) * kb.astype(F32), axis=-1, keepdims=True)
        o = eg * jnp.dot(qb, Sb, preferred_element_type=F32) + qk.astype(BF16).astype(F32) * ub.astype(F32)
        kdiag = jnp.where(eye, jnp.broadcast_to(kb.astype(F32), (GDN_DK, GDN_DK)), 0.0).astype(BF16)
        urows = jnp.broadcast_to(ub, (GDN_DK, GDN_DV))
        S_ref[0, h] = eg * S + jnp.dot(kdiag, urows, preferred_element_type=F32)
        zz = z_ref[0, :, h * GDN_DV:(h + 1) * GDN_DV]
        o_ref[0, :, h * GDN_DV:(h + 1) * GDN_DV] = (_rms(o, ng) * _silu(zz)).astype(BF16)


def _gdn_step(u, buf8, S0, p, l):
    B = u.shape[0]
    Cc = GDN_CONV_C
    H = GDN_HEADS
    return pl.pallas_call(
        _gdn_step_kernel,
        grid=(B,),
        in_specs=[pl.BlockSpec((1, 1, Cc), lambda b: (b, 0, U_QKV // Cc)),
                  pl.BlockSpec((1, 1, GDN_VW), lambda b: (b, 0, U_Z // GDN_VW)),
                  pl.BlockSpec((1, 1, W_AB), lambda b: (b, 0, U_AB // W_AB)),
                  pl.BlockSpec((None, 1, SUBLANES, Cc), lambda b: (l, b, 0, 0)),
                  pl.BlockSpec((None, 1, H, GDN_DK, GDN_DV), lambda b: (l, b, 0, 0, 0)),
                  _lspec((CONV_W, Cc), l), _lspec((1, LANES), l), _lspec((1, LANES), l), _lspec((1, GDN_DV), l)],
        out_specs=[pl.BlockSpec((1, 1, GDN_VW), lambda b: (b, 0, 0)),
                   pl.BlockSpec((1, CONV_W - 1, Cc), lambda b: (b, 0, 0)),
                   pl.BlockSpec((1, H, GDN_DK, GDN_DV), lambda b: (b, 0, 0, 0))],
        out_shape=[jax.ShapeDtypeStruct((B, 1, GDN_VW), BF16),
                   jax.ShapeDtypeStruct((B, CONV_W - 1, Cc), F32),
                   jax.ShapeDtypeStruct((B, H, GDN_DK, GDN_DV), F32)],
        scratch_shapes=[pltpu.VMEM((2 * SUBLANES, Cc), F32)],
        name="gdn_step",
        compiler_params=_cparams(("parallel",)),
    )(u, u, u, buf8, S0, p["cw"], p["alog"], p["dtb"], p["ng"])


def _mla_prep_kernel(mq_ref, mkv_ref, cos_ref, sin_ref, gq_ref, gkv_ref, wq_ref, wkv_ref,
                     q_ref, k_ref, v_ref, ckv_ref, kr_ref):
    H = MLA_HEADS
    W = LANES * H
    cos = cos_ref[0]
    sin = sin_ref[0]
    cq = _rms(mq_ref[0], gq_ref[...])
    qa = _dot(cq, wq_ref[...])
    mkv = mkv_ref[0]
    ckv = _rms(mkv[:, :KV_LORA], gkv_ref[...])
    ckv_ref[0] = ckv
    kro = mkv[:, KV_LORA:KV_LORA + LANES] * cos + mkv[:, KV_LORA + LANES:KV_LORA + 2 * LANES] * sin
    kr_ref[0] = kro[:, ROPE_LANE0:ROPE_LANE0 + QK_ROPE]
    kv = _dot(ckv, wkv_ref[...])
    for h in range(H):
        sl = slice(h * LANES, (h + 1) * LANES)
        qh = qa[:, sl] * cos + qa[:, W + h * LANES:W + (h + 1) * LANES] * sin
        q_ref[0, :, sl] = (qh * MLA_SCALE).astype(BF16)
        k_ref[0, :, sl] = (kv[:, sl] + kro).astype(BF16)
    v_ref[0] = kv[:, W:].astype(BF16)


def _mla_prep(u, cos, sin, p, l, bm):
    B, T, _ = u.shape
    W = LANES * MLA_HEADS
    return pl.pallas_call(
        _mla_prep_kernel,
        grid=(B, T // bm),
        in_specs=[pl.BlockSpec((1, bm, Q_LORA), lambda b, i: (b, i, U_MQ // Q_LORA)),
                  pl.BlockSpec((1, bm, W_MKV), lambda b, i: (b, i, U_MKV // W_MKV)),
                  pl.BlockSpec((1, bm, LANES), lambda b, i: (0, i, 0)),
                  pl.BlockSpec((1, bm, LANES), lambda b, i: (0, i, 0)),
                  _lspec((1, Q_LORA), l), _lspec((1, KV_LORA), l),
                  _lspec((Q_LORA, 2 * W), l), _lspec((KV_LORA, 2 * W), l)],
        out_specs=[pl.BlockSpec((1, bm, W), lambda b, i: (b, i, 0)),
                   pl.BlockSpec((1, bm, W), lambda b, i: (b, i, 0)),
                   pl.BlockSpec((1, bm, W), lambda b, i: (b, i, 0)),
                   pl.BlockSpec((1, bm, KV_LORA), lambda b, i: (b, i, 0)),
                   pl.BlockSpec((1, bm, QK_ROPE), lambda b, i: (b, i, 0))],
        out_shape=[jax.ShapeDtypeStruct((B, T, W), BF16),
                   jax.ShapeDtypeStruct((B, T, W), BF16),
                   jax.ShapeDtypeStruct((B, T, W), BF16),
                   jax.ShapeDtypeStruct((B, T, KV_LORA), F32),
                   jax.ShapeDtypeStruct((B, T, QK_ROPE), F32)],
        name="mla_prep",
        compiler_params=_cparams(("parallel", "parallel")),
    )(u, u, cos, sin, p["gq"], p["gkv"], p["wq"], p["wkv"])


NEG_BIG = -1e30


FLASH_HEADS = 2
FLASH_TK = 512


def _flash_kernel(q_ref, k_ref, v_ref, o_ref, *, tq):
    qi = pl.program_id(2)
    hp = FLASH_HEADS
    heads = [slice(h * LANES, (h + 1) * LANES) for h in range(hp)]
    qs = [q_ref[0, :, sl] for sl in heads]
    tk = min(FLASH_TK, tq)
    n_full = (qi * tq) // tk

    def step(j, carry, diagonal):
        ks = pl.ds(pl.multiple_of(j * tk, tk), tk)
        if diagonal:
            row = qi * tq + lax.broadcasted_iota(jnp.int32, (tq, tk), 0)
            col = j * tk + lax.broadcasted_iota(jnp.int32, (tq, tk), 1)
            causal = col <= row
        out = []
        for h, sl in enumerate(heads):
            m, l, acc = carry[h]
            s = lax.dot_general(qs[h], k_ref[0, ks, sl], (((1,), (1,)), ((), ())), preferred_element_type=F32)
            if diagonal:
                s = jnp.where(causal, s, NEG_BIG)
            m_new = jnp.maximum(m, jnp.max(s, axis=-1, keepdims=True))
            alpha = jnp.exp(m - m_new)
            p = jnp.exp(s - m_new)
            l = alpha * l + jnp.sum(p, axis=-1, keepdims=True)
            acc = alpha * acc + jnp.dot(p.astype(BF16), v_ref[0, ks, sl], preferred_element_type=F32)
            out.append((m_new, l, acc))
        return tuple(out)

    init = tuple((jnp.full((tq, 1), NEG_BIG, F32), jnp.zeros((tq, 1), F32), jnp.zeros((tq, LANES), F32))
                 for _ in heads)
    carry = lax.fori_loop(0, n_full, lambda j, c: step(j, c, False), init)
    for t in range(tq // tk):
        carry = step(n_full + t, carry, True)
    for h, sl in enumerate(heads):
        _, l, acc = carry[h]
        o_ref[0, :, sl] = (acc / l).astype(BF16)


def _flash(q, k, v, tq):
    B, T, W = q.shape
    wb = FLASH_HEADS * LANES
    H = W // wb
    return pl.pallas_call(
        functools.partial(_flash_kernel, tq=tq),
        grid=(B, H, T // tq),
        in_specs=[pl.BlockSpec((1, tq, wb), lambda b, h, i: (b, i, h)),
                  pl.BlockSpec((1, T, wb), lambda b, h, i: (b, 0, h)),
                  pl.BlockSpec((1, T, wb), lambda b, h, i: (b, 0, h))],
        out_specs=pl.BlockSpec((1, tq, wb), lambda b, h, i: (b, i, h)),
        out_shape=jax.ShapeDtypeStruct((B, T, W), BF16),
        name="mla_flash",
        compiler_params=_cparams(("parallel", "parallel", "arbitrary")),
    )(q, k, v)


DEC_GROUP = 16
DEC_SLOTS = 3


def _sattn_kernel(pt_ref, q_ref, cn_ref, krn_ref, wuk_ref, wuv_ref, ckv_hbm, krt_hbm, o_ref,
                  ck_buf, kr_buf, sem, *, layer, n_pages):
    H = MLA_HEADS
    G = DEC_GROUP
    n_groups = n_pages // G
    b = pl.program_id(0)
    nb = pl.num_programs(0)

    total = nb * n_groups
    ahead = DEC_SLOTS - 1

    def group_copies(x):
        seq, g, slot = x // n_groups, x % n_groups, x % DEC_SLOTS
        out = []
        for k in range(G):
            page = pt_ref[seq, g * G + k]
            out.append(pltpu.make_async_copy(ckv_hbm.at[layer, page],
                                             ck_buf.at[slot, pl.ds(k * PAGE_SIZE, PAGE_SIZE), :], sem.at[0, slot]))
            out.append(pltpu.make_async_copy(krt_hbm.at[layer, page],
                                             kr_buf.at[slot, :, pl.ds(k * PAGE_SIZE, PAGE_SIZE)], sem.at[1, slot]))
        return out

    @pl.when(b == 0)
    def _():
        for x in range(ahead):
            for c in group_copies(x):
                c.start()

    q = q_ref[0]
    q_rope = q[:, ROPE_LANE0:ROPE_LANE0 + QK_ROPE]
    rr = lax.broadcasted_iota(jnp.int32, (H, H * LANES), 0)
    cc = lax.broadcasted_iota(jnp.int32, (H, H * LANES), 1)
    q_bd = jnp.where(cc // LANES == rr, jnp.concatenate([q] * H, axis=1), jnp.zeros((), BF16))
    qlat = jnp.dot(q_bd, wuk_ref[...], preferred_element_type=F32).astype(BF16)

    def body(g, carry):
        m, l, acc = carry
        x = b * n_groups + g
        slot = x % DEC_SLOTS

        @pl.when(x + ahead < total)
        def _():
            for c in group_copies(x + ahead):
                c.start()

        for c in group_copies(x):
            c.wait()
        ck = ck_buf[slot].astype(BF16)
        kr_t = kr_buf[slot].astype(BF16)
        s = _dot_nt(qlat, ck) + jnp.dot(q_rope, kr_t, preferred_element_type=F32)
        m_new = jnp.maximum(m, jnp.max(s, axis=-1, keepdims=True))
        alpha = jnp.exp(m - m_new)
        p = jnp.exp(s - m_new)
        l = alpha * l + jnp.sum(p, axis=-1, keepdims=True)
        acc = alpha * acc + jnp.dot(p.astype(BF16), ck, preferred_element_type=F32)
        return m_new, l, acc

    init = (jnp.full((H, 1), NEG_BIG, F32), jnp.zeros((H, 1), F32), jnp.zeros((H, KV_LORA), F32))
    m, l, acc = lax.fori_loop(0, n_groups, body, init)

    cn = cn_ref[0].astype(BF16).astype(F32)
    krn = krn_ref[0].astype(BF16).astype(F32)
    s_n = (jnp.sum(qlat.astype(F32) * cn, axis=-1, keepdims=True)
           + jnp.sum(q_rope.astype(F32) * krn, axis=-1, keepdims=True))
    m_f = jnp.maximum(m, s_n)
    a_f = jnp.exp(m - m_f)
    p_n = jnp.exp(s_n - m_f)
    l_f = a_f * l + p_n
    o_lat = (a_f * acc + p_n.astype(BF16).astype(F32) * cn) / l_f
    o_all = jnp.dot(o_lat.astype(BF16), wuv_ref[...], preferred_element_type=F32)
    o_ref[0] = jnp.sum(jnp.where(cc // LANES == rr, o_all, 0.0), axis=0, keepdims=True).astype(BF16)


def _sample_attention(page_table, q, ckv_new, kr_new, wuk_s, wuv_c, cache_ckv, cache_krope_t, layer):
    Bd = q.shape[0]
    H = MLA_HEADS
    n_pages = page_table.shape[1]
    assert n_pages % DEC_GROUP == 0 and Bd * (n_pages // DEC_GROUP) >= DEC_SLOTS
    q3 = q.reshape(Bd, H, LANES)
    grid_spec = pltpu.PrefetchScalarGridSpec(
        num_scalar_prefetch=1,
        grid=(Bd,),
        in_specs=[pl.BlockSpec((1, H, LANES), lambda b, pt: (b, 0, 0)),
                  pl.BlockSpec((1, 1, KV_LORA), lambda b, pt: (b, 0, 0)),
                  pl.BlockSpec((1, 1, QK_ROPE), lambda b, pt: (b, 0, 0)),
                  _lspec((H * LANES, KV_LORA), layer), _lspec((KV_LORA, H * V_HEAD), layer),
                  pl.BlockSpec(memory_space=pl.ANY), pl.BlockSpec(memory_space=pl.ANY)],
        out_specs=pl.BlockSpec((1, 1, H * V_HEAD), lambda b, pt: (b, 0, 0)),
        scratch_shapes=[pltpu.VMEM((DEC_SLOTS, DEC_GROUP * PAGE_SIZE, KV_LORA), F32),
                        pltpu.VMEM((DEC_SLOTS, QK_ROPE, DEC_GROUP * PAGE_SIZE), F32),
                        pltpu.SemaphoreType.DMA((2, DEC_SLOTS))],
    )
    return pl.pallas_call(
        functools.partial(_sattn_kernel, layer=layer, n_pages=n_pages),
        grid_spec=grid_spec,
        out_shape=jax.ShapeDtypeStruct((Bd, 1, H * V_HEAD), BF16),
        name="mla_decode",
        compiler_params=_cparams(("arbitrary",)),
    )(page_table, q3, ckv_new, kr_new, wuk_s, wuv_c, cache_ckv, cache_krope_t)


def _merge_kernel(x_ref, org_ref, ogdn_ref, omla_ref, ga_ref, gb_ref, gc_ref, gt_ref,
                  wrg_ref, wgdn_ref, wmla_ref, wo_ref, o_ref):
    d = functools.partial(jnp.dot, preferred_element_type=F32)
    m = (_sigmoid(ga_ref[0]) * d(org_ref[0], wrg_ref[...])
         + _sigmoid(gb_ref[0]) * d(ogdn_ref[0], wgdn_ref[...])
         + _sigmoid(gc_ref[0]) * d(omla_ref[0], wmla_ref[...]))
    o_ref[0] = x_ref[0] + gt_ref[...] * d(m.astype(BF16), wo_ref[...])


def _merge(x, o_rg, o_gdn, o_mla, u, mod, p, l, bm):
    B, T, D = x.shape
    tok = lambda w, col: pl.BlockSpec((1, bm, w), lambda b, i: (b, i, col))
    return pl.pallas_call(
        _merge_kernel,
        grid=(B, T // bm),
        in_specs=[tok(D, 0), tok(D_RNN, 0), tok(GDN_VW, 0), tok(D, 0),
                  tok(D, U_GA // D), tok(D, U_GB // D), tok(D, U_GC // D),
                  mod.spec(MOD_GT1, bm),
                  _lspec((D_RNN, D), l), _lspec((GDN_VW, D), l), _lspec((D, D), l), _lspec((D, D), l)],
        out_specs=tok(D, 0),
        out_shape=jax.ShapeDtypeStruct((B, T, D), F32),
        name="branch_merge",
        compiler_params=_cparams(("parallel", "parallel")),
    )(x, o_rg, o_gdn, o_mla, u, u, u, mod.array, p["wrg"], p["wgdn"], p["wmla"], p["wo"])


def _ffn_kernel(x_ref, sc_ref, sh_ref, gt_ref, g_ref, gf_ref, wg_ref, wu_ref, wd_ref, o_ref, hb_ref, acc_ref, *, final):
    j = pl.program_id(2)

    @pl.when(j == 0)
    def _():
        h = _rms(x_ref[0], g_ref[...]) * (1.0 + sc_ref[...]) + sh_ref[...]
        hb_ref[...] = h.astype(BF16)
        acc_ref[...] = jnp.zeros(acc_ref.shape, F32)

    hb = hb_ref[...]
    gate = jnp.dot(hb, wg_ref[...], preferred_element_type=F32)
    up = jnp.dot(hb, wu_ref[...], preferred_element_type=F32)
    acc_ref[...] += jnp.dot((_silu(gate) * up).astype(BF16), wd_ref[...], preferred_element_type=F32)

    @pl.when(j == pl.num_programs(2) - 1)
    def _():
        y = x_ref[0] + gt_ref[...] * acc_ref[...]
        o_ref[0] = _rms(y, gf_ref[...]) if final else y


def _ffn(x, mod, g, g_final, w_in_b, w_out_b, l, bm, bf, final):
    B, T, D = x.shape
    nf = D_FF // bf
    return pl.pallas_call(
        functools.partial(_ffn_kernel, final=final),
        grid=(B, T // bm, nf),
        in_specs=[pl.BlockSpec((1, bm, D), lambda b, i, j: (b, i, 0)),
                  mod.spec(MOD_SC2, bm), mod.spec(MOD_SH2, bm), mod.spec(MOD_GT2, bm),
                  _lspec((1, D), l),
                  pl.BlockSpec((1, D), lambda b, i, j: (0, 0)),
                  pl.BlockSpec((None, D, bf), lambda b, i, j: (l, 0, j)),
                  pl.BlockSpec((None, D, bf), lambda b, i, j: (l, 0, nf + j)),
                  pl.BlockSpec((None, bf, D), lambda b, i, j: (l, j, 0))],
        out_specs=pl.BlockSpec((1, bm, D), lambda b, i, j: (b, i, 0)),
        out_shape=jax.ShapeDtypeStruct((B, T, D), F32),
        scratch_shapes=[pltpu.VMEM((bm, D), BF16), pltpu.VMEM((bm, D), F32)],
        name="swiglu",
        compiler_params=_cparams(("parallel", "parallel", "arbitrary")),
    )(x, mod.array, mod.array, mod.array, g, g_final, w_in_b, w_in_b, w_out_b)


def _rot_half(w):
    half = QK_ROPE // 2
    return jnp.concatenate([-w[..., half:], w[..., :half]], axis=-1)


def _pack_w_in(w):
    L, D, _ = w.shape
    wt = jnp.swapaxes(w, 1, 2)
    offs = np.cumsum((0,) + IN_SIZES)
    rx, ry, qkv, z, a, b, mq, mkv, gate = [wt[:, offs[i]:offs[i + 1]] for i in range(len(IN_SIZES))]
    zeros = lambda n: jnp.zeros((L, n, D), w.dtype)
    kr = mkv[:, KV_LORA:]
    half = QK_ROPE // 2
    kr_rot = jnp.concatenate([-kr[:, half:], kr[:, :half]], axis=1)
    tail = LANES - ROPE_LANE0 - QK_ROPE
    rows = [rx, ry, zeros(U_QKV - 2 * D_RNN), qkv, z, gate,
            a, b, zeros(W_AB - 2 * GDN_HEADS), mq,
            mkv[:, :KV_LORA], zeros(ROPE_LANE0), kr, zeros(tail), zeros(ROPE_LANE0), kr_rot, zeros(tail)]
    out = jnp.concatenate(rows, axis=1).astype(BF16)
    assert out.shape[1] == N_U
    return out


def _pack_w_uq(w):
    H = MLA_HEADS
    L = w.shape[0]
    w = w.reshape(L, Q_LORA, H, QK_NOPE + QK_ROPE)
    nope, rope = w[..., :QK_NOPE], w[..., QK_NOPE:]
    tail = jnp.zeros((L, Q_LORA, H, LANES - QK_NOPE - QK_ROPE), w.dtype)
    a = jnp.concatenate([nope, rope, tail], axis=-1).reshape(L, Q_LORA, H * LANES)
    b = jnp.concatenate([jnp.zeros_like(nope), _rot_half(rope), tail], axis=-1).reshape(L, Q_LORA, H * LANES)
    return jnp.concatenate([a, b], axis=-1).astype(BF16)


def _pack_w_ukv(w):
    H = MLA_HEADS
    L = w.shape[0]
    w3 = w.reshape(L, KV_LORA, H, QK_NOPE + V_HEAD)
    w_uk, w_uv = w3[..., :QK_NOPE], w3[..., QK_NOPE:]
    kpad = jnp.concatenate([w_uk, jnp.zeros((L, KV_LORA, H, LANES - QK_NOPE), w.dtype)], axis=-1)
    wuv_c = w_uv.reshape(L, KV_LORA, H * V_HEAD)
    wkv_p = jnp.concatenate([kpad.reshape(L, KV_LORA, H * LANES), wuv_c], axis=-1)
    wuk_s = jnp.transpose(kpad, (0, 2, 3, 1)).reshape(L, H * LANES, KV_LORA)
    return wkv_p.astype(BF16), wuk_s.astype(BF16), wuv_c.astype(BF16)


def _rope_tables(T, pos0):
    inv = ROPE_BASE ** (-jnp.arange(0, QK_ROPE, 2, dtype=F32) / QK_ROPE)
    ang = (jnp.arange(T, dtype=F32) + pos0)[:, None] * inv[None, :]
    cos, sin = jnp.cos(ang), jnp.sin(ang)
    tail = jnp.zeros((T, LANES - ROPE_LANE0 - QK_ROPE), F32)
    cos_t = jnp.concatenate([jnp.ones((T, ROPE_LANE0), F32), cos, cos, tail], axis=1)
    sin_t = jnp.concatenate([jnp.zeros((T, ROPE_LANE0), F32), sin, sin, tail], axis=1)
    return cos_t[None], sin_t[None]


def kernel(x_prompt, x_sample, cache_ckv, cache_krope, state_rg_conv, state_rg_h, state_gdn_conv, state_gdn_S,
           page_table, c_prompt, c_sample, w_ada, b_ada, g_norm1, g_norm2, w_in, rg_conv_w, rg_conv_b, rg_wa,
           rg_ba, rg_wx, rg_bx, rg_lambda, gdn_conv_w, gdn_A_log, gdn_dt_bias, gdn_norm_g, mla_q_norm_g, w_uq,
           mla_kv_norm_g, w_ukv, w_rg_proj, w_gdn_proj, w_mla_proj, w_o, w_ffn_in, w_ffn_out, g_final):
    L = w_in.shape[0]
    Bp, T, D = x_prompt.shape
    Bd = x_sample.shape[0]
    past_len = page_table.shape[1] * PAGE_SIZE

    mod = _modulation(jnp.concatenate([c_sample, c_prompt], axis=0), w_ada, b_ada)
    cos_p, sin_p = _rope_tables(T, 0.0)
    cos_s, sin_s = _rope_tables(1, float(past_len))
    cos_s = jnp.broadcast_to(cos_s, (1, Bd, LANES))
    sin_s = jnp.broadcast_to(sin_s, (1, Bd, LANES))

    row = lambda v: v.reshape(L, 1, v.shape[-1])
    lane_row = lambda v: jnp.pad(v, ((0, 0), (0, LANES - v.shape[-1]))).reshape(L, 1, LANES)
    pad8 = lambda s: jnp.pad(s, ((0, 0), (0, 0), (SUBLANES - (CONV_W - 1), 0), (0, 0)))
    g1, g2, gf = row(g_norm1), row(g_norm2), g_final.reshape(1, D)
    w_in_p = _pack_w_in(w_in)
    wkv_p, wuk_s, wuv_c = _pack_w_ukv(w_ukv)
    rg_p = dict(cw=rg_conv_w, cb=row(rg_conv_b), wa=rg_wa.astype(BF16), ba=row(rg_ba),
                wx=rg_wx.astype(BF16), bx=row(rg_bx), lam=row(rg_lambda))
    gdn_p = dict(cw=gdn_conv_w, alog=lane_row(gdn_A_log), dtb=lane_row(gdn_dt_bias), ng=row(gdn_norm_g))
    mla_p = dict(gq=row(mla_q_norm_g), gkv=row(mla_kv_norm_g), wq=_pack_w_uq(w_uq), wkv=wkv_p)
    mrg_p = dict(wrg=w_rg_proj.astype(BF16), wgdn=w_gdn_proj.astype(BF16), wmla=w_mla_proj.astype(BF16),
                 wo=w_o.astype(BF16))
    wfi, wfo = w_ffn_in.astype(BF16), w_ffn_out.astype(BF16)
    rg_buf_s, gdn_buf_s = pad8(state_rg_conv), pad8(state_gdn_conv)
    rg_h_s = state_rg_h.reshape(L, Bd, 1, D_RNN)
    rg_buf_0 = jnp.zeros((1, Bp, SUBLANES, D_RNN), F32)
    rg_h_0 = jnp.zeros((1, Bp, 1, D_RNN), F32)
    gdn_buf_0 = jnp.zeros((1, Bp, SUBLANES, GDN_CONV_C), F32)
    cache_krope_t = jnp.swapaxes(cache_krope, 2, 3)

    xp = x_prompt
    xs = x_sample.reshape(1, Bd, D)
    outs_p, outs_s = [], []
    for l in range(L):
        final = l == L - 1

        m_p = _Mod(mod, l, per_token=False, row0=Bd)
        u = _in_proj(xp, m_p, g1, w_in_p, l, bm=min(2048, T), bn=1024)
        o_rg, rg_buf, rg_h = _rglru(u, rg_buf_0, rg_h_0, 0, rg_p, l, tt=128)
        o_gdn, gdn_buf, gdn_S = _gdn_prompt(u, gdn_buf_0, gdn_p, l, tt=256)
        q, k, v, ckv, kr = _mla_prep(u, cos_p, sin_p, mla_p, l, bm=512)
        o_mla = _flash(q, k, v, tq=min(512, T))
        x1 = _merge(xp, o_rg, o_gdn, o_mla, u, m_p, mrg_p, l, bm=512)
        xp = _ffn(x1, m_p, g2, gf, wfi, wfo, l, bm=512, bf=1408, final=final)
        outs_p.append((ckv, kr, rg_buf, rg_h.reshape(Bp, D_RNN), gdn_buf, gdn_S))

        m_s = _Mod(mod, l, per_token=True, row0=0)
        us = _in_proj(xs, m_s, g1, w_in_p, l, bm=Bd, bn=1024)
        us_seq = us.reshape(Bd, 1, N_U)
        o_rg, rg_buf, rg_h = _rglru(us_seq, rg_buf_s, rg_h_s, l, rg_p, l, tt=1)
        o_gdn, gdn_buf, gdn_S = _gdn_step(us_seq, gdn_buf_s, state_gdn_S, gdn_p, l)
        q, k, v, ckv, kr = _mla_prep(us, cos_s, sin_s, mla_p, l, bm=Bd)
        o_mla = _sample_attention(page_table, q.reshape(Bd, MLA_HEADS * LANES), ckv.reshape(Bd, 1, KV_LORA),
                                  kr.reshape(Bd, 1, QK_ROPE), wuk_s, wuv_c, cache_ckv, cache_krope_t, l)
        x1 = _merge(xs, o_rg.reshape(1, Bd, D_RNN), o_gdn.reshape(1, Bd, GDN_VW), o_mla.reshape(1, Bd, D),
                    us, m_s, mrg_p, l, bm=Bd)
        xs = _ffn(x1, m_s, g2, gf, wfi, wfo, l, bm=Bd, bf=1408, final=final)
        outs_s.append((ckv.reshape(Bd, 1, KV_LORA), kr.reshape(Bd, 1, QK_ROPE), rg_buf, rg_h.reshape(Bd, D_RNN),
                       gdn_buf, gdn_S))

    stack = lambda outs, i: jnp.stack([o[i] for o in outs])
    return (xp, xs.reshape(Bd, 1, D),
            stack(outs_p, 0), stack(outs_p, 1), stack(outs_p, 2), stack(outs_p, 3), stack(outs_p, 4), stack(outs_p, 5),
            stack(outs_s, 0), stack(outs_s, 1), stack(outs_s, 2), stack(outs_s, 3), stack(outs_s, 4), stack(outs_s, 5))
```

```python
import functools
import math

import numpy as np
import jax
import jax.numpy as jnp
from jax import lax
from jax.experimental import pallas as pl
from jax.experimental.pallas import tpu as pltpu

F32 = jnp.float32
BF16 = jnp.bfloat16

D_MODEL = 1024
CONV_W = 4
D_RNN = 1280
RG_BLOCK = 128
RG_BLOCKS = D_RNN // RG_BLOCK
RG_C = 8.0
GDN_HEADS = 8
GDN_DK = 128
GDN_DV = 128
GDN_KW = GDN_HEADS * GDN_DK
GDN_VW = GDN_HEADS * GDN_DV
GDN_CONV_C = 2 * GDN_KW + GDN_VW
GDN_CHUNK = 64
MLA_HEADS = 8
Q_LORA = 384
KV_LORA = 256
QK_NOPE = 64
QK_ROPE = 32
V_HEAD = 128
MLA_SCALE = (QK_NOPE + QK_ROPE) ** -0.5
ROPE_BASE = 10000.0
D_FF = 2816
N_BRANCH = 3
IN_SIZES = (D_RNN, D_RNN, GDN_CONV_C, GDN_VW, GDN_HEADS, GDN_HEADS, Q_LORA, KV_LORA + QK_ROPE, N_BRANCH * D_MODEL)
EPS = 1e-6
PAGE_SIZE = 128

LANES = 128
SUBLANES = 8
VMEM_LIMIT = 56 * 1024 * 1024

U_RX = 0
U_RY = D_RNN
U_QKV = 3072
U_Z = 6144
U_GA = 7168
U_GB = 8192
U_GC = 9216
U_AB = 10240
U_MQ = 10368
U_MKV = 10752
N_U = 11264
W_AB = LANES
W_MKV = 512
ROPE_LANE0 = QK_NOPE


def _cparams(sem):
    return pltpu.CompilerParams(dimension_semantics=sem, vmem_limit_bytes=VMEM_LIMIT)


def _sigmoid(x):
    return 1.0 / (1.0 + jnp.exp(-x))


def _silu(x):
    return x * _sigmoid(x)


def _softplus(x):
    return jnp.maximum(x, 0.0) + jnp.log1p(jnp.exp(-jnp.abs(x)))


def _gelu_tanh(x):
    c = math.sqrt(2.0 / math.pi)
    return 0.5 * x * (1.0 + jnp.tanh(c * (x + 0.044715 * (x * x * x))))


def _rms(x, g):
    return x * lax.rsqrt(jnp.mean(x * x, axis=-1, keepdims=True) + EPS) * g


def _dot(a, b):
    return jnp.dot(a.astype(BF16), b.astype(BF16), preferred_element_type=F32)


def _dot_nt(a, b):
    return lax.dot_general(a.astype(BF16), b.astype(BF16), (((1,), (1,)), ((), ())), preferred_element_type=F32)


def _dot_tn(a, b):
    return lax.dot_general(a.astype(BF16), b.astype(BF16), (((0,), (0,)), ((), ())), preferred_element_type=F32)


def _split2(a):
    hi = a.astype(BF16)
    lo = (a - hi.astype(F32)).astype(BF16)
    return hi, lo


def _dot3(a, b):
    ah, al = _split2(a)
    bh, bl = _split2(b)
    d = functools.partial(jnp.dot, preferred_element_type=F32)
    return d(ah, bh) + (d(ah, bl) + d(al, bh))


def _dot_exact_lhs(a_bf16, b):
    b0 = b.astype(BF16)
    r1 = b - b0.astype(F32)
    b1 = r1.astype(BF16)
    b2 = (r1 - b1.astype(F32)).astype(BF16)
    d = functools.partial(jnp.dot, preferred_element_type=F32)
    return d(a_bf16, b0) + (d(a_bf16, b1) + d(a_bf16, b2))


def _mod_kernel(c_ref, w_ref, b_ref, o_ref):
    c = c_ref[...]
    o_ref[0] = _dot(_silu(c), w_ref[0]) + b_ref[0]


def _modulation(c_all, w_ada, b_ada):
    L, D, N = w_ada.shape
    R = c_all.shape[0]
    bn = 1536
    return pl.pallas_call(
        _mod_kernel,
        grid=(L, N // bn),
        in_specs=[pl.BlockSpec((R, D), lambda l, j: (0, 0)),
                  pl.BlockSpec((1, D, bn), lambda l, j: (l, 0, j)),
                  pl.BlockSpec((1, 1, bn), lambda l, j: (l, 0, j))],
        out_specs=pl.BlockSpec((1, R, bn), lambda l, j: (l, 0, j)),
        out_shape=jax.ShapeDtypeStruct((L, R, N), F32),
        name="adaln_mod",
        compiler_params=_cparams(("parallel", "parallel")),
    )(c_all, w_ada, b_ada.reshape(L, 1, N))


MOD_SH1, MOD_SC1, MOD_GT1, MOD_SH2, MOD_SC2, MOD_GT2 = range(6)


class _Mod:
    def __init__(self, mod, layer, per_token, row0):
        self.layer, self.per_token, self.row0 = layer, per_token, row0
        L, R, N = mod.shape
        self.array = mod if per_token else mod.reshape(L, R, 1, N)

    def spec(self, part, bm):
        l, r0 = self.layer, self.row0
        if self.per_token:
            assert r0 == 0
            return pl.BlockSpec((None, bm, D_MODEL), lambda b, i, *_: (l, i, part))
        return pl.BlockSpec((None, None, 1, D_MODEL), lambda b, i, *_: (l, r0 + b, 0, part))


def _lspec(shape, l):
    return pl.BlockSpec((None,) + tuple(shape), lambda *_: (l,) + (0,) * len(shape))


def _in_kernel(x_ref, sc_ref, sh_ref, g_ref, w_ref, o_ref, hb_ref):
    @pl.when(pl.program_id(2) == 0)
    def _():
        h = _rms(x_ref[0], g_ref[...]) * (1.0 + sc_ref[...]) + sh_ref[...]
        hb_ref[...] = h.astype(BF16)

    o_ref[0] = lax.dot_general(hb_ref[...], w_ref[...], (((1,), (1,)), ((), ())), preferred_element_type=F32)


def _in_proj(x, mod, g, w_packed, l, bm, bn):
    B, T, D = x.shape
    N = w_packed.shape[1]
    return pl.pallas_call(
        _in_kernel,
        grid=(B, T // bm, N // bn),
        in_specs=[pl.BlockSpec((1, bm, D), lambda b, i, j: (b, i, 0)),
                  mod.spec(MOD_SC1, bm), mod.spec(MOD_SH1, bm),
                  _lspec((1, D), l),
                  pl.BlockSpec((None, bn, D), lambda b, i, j: (l, j, 0))],
        out_specs=pl.BlockSpec((1, bm, bn), lambda b, i, j: (b, i, j)),
        out_shape=jax.ShapeDtypeStruct((B, T, N), F32),
        scratch_shapes=[pltpu.VMEM((bm, D), BF16)],
        name="in_proj",
        compiler_params=_cparams(("parallel", "parallel", "arbitrary")),
    )(x, mod.array, mod.array, g, w_packed)


SCAN_PAD = 64


def _rg_kernel(ux_ref, uy_ref, buf_ref, h0_ref, cw_ref, cb_ref, wa_ref, ba_ref, wx_ref, bx_ref, lam_ref,
               o_ref, nbuf_ref, hl_ref, xbuf, abuf, bbuf, hc, *, tt):
    i = pl.program_id(1)
    nt = pl.num_programs(1)

    @pl.when(i == 0)
    def _():
        xbuf[0:SUBLANES, :] = buf_ref[0]
        hc[...] = h0_ref[0]
        abuf[0:SCAN_PAD, :] = jnp.ones((SCAN_PAD, D_RNN), F32)
        bbuf[0:SCAN_PAD, :] = jnp.zeros((SCAN_PAD, D_RNN), F32)

    xbuf[SUBLANES:SUBLANES + tt, :] = ux_ref[0]
    lo = SUBLANES - (CONV_W - 1)
    if tt >= SUBLANES:
        xs = xbuf[...]
        xc = cb_ref[...] + cw_ref[CONV_W - 1:CONV_W, :] * xs[SUBLANES:, :]
        for j in range(CONV_W - 1):
            xc = xc + cw_ref[j:j + 1, :] * pltpu.roll(xs, CONV_W - 1 - j, axis=0)[SUBLANES:, :]
    else:
        xc = cb_ref[...] + cw_ref[0:1, :] * xbuf[lo:lo + tt, :]
        for j in range(1, CONV_W):
            xc = xc + cw_ref[j:j + 1, :] * xbuf[lo + j:lo + j + tt, :]

    @pl.when(i == nt - 1)
    def _():
        nbuf_ref[0] = xbuf[tt + lo:tt + SUBLANES, :]

    if tt >= SUBLANES:
        xbuf[0:SUBLANES, :] = xbuf[tt:tt + SUBLANES, :]

    xb = xc.astype(BF16)
    ra = jnp.concatenate([jnp.dot(xb[:, n * RG_BLOCK:(n + 1) * RG_BLOCK], wa_ref[n], preferred_element_type=F32)
                          for n in range(RG_BLOCKS)], axis=1)
    ri = jnp.concatenate([jnp.dot(xb[:, n * RG_BLOCK:(n + 1) * RG_BLOCK], wx_ref[n], preferred_element_type=F32)
                          for n in range(RG_BLOCKS)], axis=1)
    r = _sigmoid(ra + ba_ref[...])
    ig = _sigmoid(ri + bx_ref[...])
    log_a = (-RG_C) * r * _softplus(-lam_ref[...])
    a = jnp.exp(log_a)
    b = jnp.sqrt(-jnp.tanh(log_a) * (a * a + 1.0)) * (ig * xc)

    abuf[SCAN_PAD:SCAN_PAD + tt, :] = a
    bbuf[SCAN_PAD:SCAN_PAD + tt, :] = b
    d = 1
    while d < tt:
        a_s = abuf[SCAN_PAD - d:SCAN_PAD - d + tt, :]
        b_s = bbuf[SCAN_PAD - d:SCAN_PAD - d + tt, :]
        a0 = abuf[SCAN_PAD:SCAN_PAD + tt, :]
        b0 = bbuf[SCAN_PAD:SCAN_PAD + tt, :]
        abuf[SCAN_PAD:SCAN_PAD + tt, :] = a0 * a_s
        bbuf[SCAN_PAD:SCAN_PAD + tt, :] = a0 * b_s + b0
        d *= 2
    h = bbuf[SCAN_PAD:SCAN_PAD + tt, :] + abuf[SCAN_PAD:SCAN_PAD + tt, :] * hc[...]
    hc[...] = h[tt - 1:tt, :]
    o_ref[0] = (h * _gelu_tanh(uy_ref[0])).astype(BF16)

    @pl.when(i == nt - 1)
    def _():
        hl_ref[0] = h[tt - 1:tt, :]


def _rglru(u, buf8, h0, ls, p, l, tt):
    B, T, _ = u.shape
    C = D_RNN
    wblk = (RG_BLOCKS, RG_BLOCK, RG_BLOCK)
    return pl.pallas_call(
        functools.partial(_rg_kernel, tt=tt),
        grid=(B, T // tt),
        in_specs=[pl.BlockSpec((1, tt, C), lambda b, i: (b, i, U_RX // C)),
                  pl.BlockSpec((1, tt, C), lambda b, i: (b, i, U_RY // C)),
                  pl.BlockSpec((None, 1, SUBLANES, C), lambda b, i: (ls, b, 0, 0)),
                  pl.BlockSpec((None, 1, 1, C), lambda b, i: (ls, b, 0, 0)),
                  _lspec((CONV_W, C), l), _lspec((1, C), l),
                  _lspec(wblk, l), _lspec((1, C), l),
                  _lspec(wblk, l), _lspec((1, C), l), _lspec((1, C), l)],
        out_specs=[pl.BlockSpec((1, tt, C), lambda b, i: (b, i, 0)),
                   pl.BlockSpec((1, CONV_W - 1, C), lambda b, i: (b, 0, 0)),
                   pl.BlockSpec((1, 1, C), lambda b, i: (b, 0, 0))],
        out_shape=[jax.ShapeDtypeStruct((B, T, C), BF16),
                   jax.ShapeDtypeStruct((B, CONV_W - 1, C), F32),
                   jax.ShapeDtypeStruct((B, 1, C), F32)],
        scratch_shapes=[pltpu.VMEM((tt + SUBLANES, C), F32),
                        pltpu.VMEM((SCAN_PAD + tt, C), F32),
                        pltpu.VMEM((SCAN_PAD + tt, C), F32),
                        pltpu.VMEM((1, C), F32)],
        name="rglru",
        compiler_params=_cparams(("parallel", "arbitrary")),
    )(u, u, buf8, h0, p["cw"], p["cb"], p["wa"], p["ba"], p["wx"], p["bx"], p["lam"])


TRI_BASE = 8
INV_GROUP = 8
NEWTON_STEPS = 2


def _cat_dot3(x, y, bd_b, n):
    C = x.shape[0]
    xh, xl = _split2(x)
    yh, yl = _split2(y)
    d = functools.partial(jnp.dot, preferred_element_type=F32)
    r = d(jnp.concatenate([xh, xl], axis=0), jnp.concatenate([yh] * n, axis=0) * bd_b)
    return r[:C] + (r[C:] + d(xh, jnp.concatenate([yl] * n, axis=0) * bd_b))


def _cat_dot1(x, y, bd_b, n):
    yb = jnp.concatenate([y.astype(BF16)] * n, axis=0) * bd_b
    return jnp.dot(x.astype(BF16), yb, preferred_element_type=F32)


def _tri_inv_cat(Lcs, masks, bd_b, n):
    eye, base_mask, level_masks = masks
    mm = functools.partial(_cat_dot1, bd_b=bd_b, n=n)
    mm3 = functools.partial(_cat_dot3, bd_b=bd_b, n=n)
    N = [jnp.where(base_mask, -Lc, 0.0) for Lc in Lcs]
    P = [eye + x for x in N]
    N2 = [mm(x, x) for x in N]
    P = [p + mm(p, x) for p, x in zip(P, N2)]
    N4 = [mm(x, x) for x in N2]
    P = [p + mm(p, x) for p, x in zip(P, N4)]
    for m in level_masks:
        PO = [mm(p, jnp.where(m, Lc, 0.0)) for p, Lc in zip(P, Lcs)]
        P = [p - mm(po, p) for p, po in zip(P, PO)]
    for _ in range(NEWTON_STEPS):
        R = [eye - p - mm3(Lc, p) for p, Lc in zip(P, Lcs)]
        P = [p + mm(p, r) for p, r in zip(P, R)]
    return P


def _cat_masks(C, n):
    rr = lax.broadcasted_iota(jnp.int32, (C, n * C), 0)
    jj = lax.broadcasted_iota(jnp.int32, (C, n * C), 1) % C
    same = lambda s: (rr // s) == (jj // s)
    eye = (rr == jj).astype(F32)
    levels = []
    s = TRI_BASE
    while s < C:
        levels.append(jnp.logical_and(same(2 * s), jnp.logical_not(same(s))))
        s *= 2
    return eye, same(TRI_BASE), levels


def _gdn_kernel(qkv_ref, z_ref, ab_ref, buf_ref, cw_ref, alog_ref, dtb_ref, ng_ref,
                o_ref, nbuf_ref, S_ref, xbuf, act, S_sc, wq_sc, uv_sc, qk_sc, kd_sc, gl_sc, lc_sc, rhs_sc, *, tt):
    i = pl.program_id(1)
    nt = pl.num_programs(1)
    C = GDN_CHUNK
    H = GDN_HEADS
    n = tt // C

    @pl.when(i == 0)
    def _():
        xbuf[0:SUBLANES, :] = buf_ref[0]
        S_sc[...] = jnp.zeros(S_sc.shape, F32)

    xbuf[SUBLANES:SUBLANES + tt, :] = qkv_ref[0]
    lo = SUBLANES - (CONV_W - 1)
    xs = xbuf[...]
    y = cw_ref[CONV_W - 1:CONV_W, :] * xs[SUBLANES:, :]
    for j in range(CONV_W - 1):
        y = y + cw_ref[j:j + 1, :] * pltpu.roll(xs, CONV_W - 1 - j, axis=0)[SUBLANES:, :]
    act[...] = _silu(y)

    @pl.when(i == nt - 1)
    def _():
        nbuf_ref[0] = xbuf[tt + lo:tt + SUBLANES, :]

    xbuf[0:SUBLANES, :] = xbuf[tt:tt + SUBLANES, :]

    ab = ab_ref[0]
    g = -jnp.exp(alog_ref[...]) * _softplus(ab + dtb_ref[...])
    beta = _sigmoid(ab)

    ri = lax.broadcasted_iota(jnp.int32, (tt, tt), 0)
    ci = lax.broadcasted_iota(jnp.int32, (tt, tt), 1)
    bd = (ri // C) == (ci // C)
    bd_incl = jnp.logical_and(bd, ri >= ci)
    bd_strict = jnp.logical_and(bd, ri > ci)
    bd_b = bd.astype(BF16)
    last_b = jnp.logical_and(bd, ci % C == C - 1).astype(BF16)
    gcum = _dot_exact_lhs(bd_incl.astype(BF16), g)
    glast = _dot_exact_lhs(last_b, gcum)
    gl_sc[...] = jnp.exp(glast)
    gcum_t = gcum.T
    masks = _cat_masks(C, n)
    ng = ng_ref[...]

    for h in range(H):
        q = act[:, h * GDN_DK:(h + 1) * GDN_DK]
        k = act[:, GDN_KW + h * GDN_DK:GDN_KW + (h + 1) * GDN_DK]
        v = act[:, 2 * GDN_KW + h * GDN_DV:2 * GDN_KW + (h + 1) * GDN_DV]
        q = q * lax.rsqrt(jnp.sum(q * q, axis=-1, keepdims=True) + EPS) * (GDN_DK ** -0.5)
        k = k * lax.rsqrt(jnp.sum(k * k, axis=-1, keepdims=True) + EPS)
        gc = gcum[:, h:h + 1]
        gr = gcum_t[h:h + 1, :]
        e = jnp.exp(jnp.where(bd_incl, gc - gr, -jnp.inf))
        bt = beta[:, H + h:H + h + 1]
        kb = k.astype(BF16)
        qb = q.astype(BF16)
        Lf = jnp.where(bd_strict, bt * _dot_nt(kb, kb) * e, 0.0)
        qkd = (_dot_nt(qb, kb) * e).astype(BF16)
        for c in range(n):
            qk_sc[h, c] = qkd[c * C:(c + 1) * C, c * C:(c + 1) * C]
        Lc = Lf[0:C]
        for c in range(1, n):
            Lc = Lc + Lf[c * C:(c + 1) * C]
        lc_sc[h] = Lc
        eg = jnp.exp(gc)
        rhs_sc[h, :, 0:GDN_DV] = bt * v
        rhs_sc[h, :, GDN_DV:] = (bt * eg) * k
        qe = (eg * q).astype(BF16)
        for c in range(n):
            wq_sc[h, c, C:2 * C, :] = qe[c * C:(c + 1) * C]
        kd_sc[h] = (k * jnp.exp(glast[:, h:h + 1] - gc)).astype(BF16)

    for h0 in range(0, H, INV_GROUP):
        hs = range(h0, h0 + INV_GROUP)
        t_cats = _tri_inv_cat([lc_sc[h] for h in hs], masks, bd_b, n)
        for h, t_cat in zip(hs, t_cats):
            t_bd = jnp.where(bd, jnp.concatenate([t_cat] * n, axis=0), 0.0)
            uw = _dot3(t_bd, rhs_sc[h])
            uv_sc[h] = uw[:, :GDN_DV]
            w = uw[:, GDN_DV:].astype(BF16)
            for c in range(n):
                wq_sc[h, c, 0:C, :] = w[c * C:(c + 1) * C]

    def chunk(c, carry):
        r0 = pl.multiple_of(c * C, C)
        rows = pl.ds(r0, C)
        gl = gl_sc[pl.ds(r0, 1), :]
        S = [S_sc[h] for h in range(H)]
        wq = [jnp.dot(wq_sc[h, c], S[h].astype(BF16), preferred_element_type=F32) for h in range(H)]
        Ub = [(uv_sc[h, rows, :] - wq[h][:C]).astype(BF16) for h in range(H)]
        o = [wq[h][C:] + jnp.dot(qk_sc[h, c], Ub[h], preferred_element_type=F32) for h in range(H)]
        for h in range(H):
            S_sc[h] = gl[:, h:h + 1] * S[h] + _dot_tn(kd_sc[h, rows, :], Ub[h])
        for h in range(H):
            zz = z_ref[0, rows, h * GDN_DV:(h + 1) * GDN_DV]
            o_ref[0, rows, h * GDN_DV:(h + 1) * GDN_DV] = (_rms(o[h], ng) * _silu(zz)).astype(BF16)
        return carry

    lax.fori_loop(0, n, chunk, 0, unroll=True)

    @pl.when(i == nt - 1)
    def _():
        S_ref[0] = S_sc[...]


def _gdn_prompt(u, buf8, p, l, tt):
    B, T, _ = u.shape
    Cc = GDN_CONV_C
    H = GDN_HEADS
    return pl.pallas_call(
        functools.partial(_gdn_kernel, tt=tt),
        grid=(B, T // tt),
        in_specs=[pl.BlockSpec((1, tt, Cc), lambda b, i: (b, i, U_QKV // Cc)),
                  pl.BlockSpec((1, tt, GDN_VW), lambda b, i: (b, i, U_Z // GDN_VW)),
                  pl.BlockSpec((1, tt, W_AB), lambda b, i: (b, i, U_AB // W_AB)),
                  pl.BlockSpec((None, 1, SUBLANES, Cc), lambda b, i: (0, b, 0, 0)),
                  _lspec((CONV_W, Cc), l), _lspec((1, LANES), l), _lspec((1, LANES), l), _lspec((1, GDN_DV), l)],
        out_specs=[pl.BlockSpec((1, tt, GDN_VW), lambda b, i: (b, i, 0)),
                   pl.BlockSpec((1, CONV_W - 1, Cc), lambda b, i: (b, 0, 0)),
                   pl.BlockSpec((1, H, GDN_DK, GDN_DV), lambda b, i: (b, 0, 0, 0))],
        out_shape=[jax.ShapeDtypeStruct((B, T, GDN_VW), BF16),
                   jax.ShapeDtypeStruct((B, CONV_W - 1, Cc), F32),
                   jax.ShapeDtypeStruct((B, H, GDN_DK, GDN_DV), F32)],
        scratch_shapes=[pltpu.VMEM((tt + SUBLANES, Cc), F32),
                        pltpu.VMEM((tt, Cc), F32),
                        pltpu.VMEM((H, GDN_DK, GDN_DV), F32),
                        pltpu.VMEM((H, tt // GDN_CHUNK, 2 * GDN_CHUNK, GDN_DK), BF16),
                        pltpu.VMEM((H, tt, GDN_DV), F32),
                        pltpu.VMEM((H, tt // GDN_CHUNK, GDN_CHUNK, GDN_CHUNK), BF16),
                        pltpu.VMEM((H, tt, GDN_DK), BF16),
                        pltpu.VMEM((tt, LANES), F32),
                        pltpu.VMEM((H, GDN_CHUNK, tt), F32),
                        pltpu.VMEM((H, tt, GDN_DV + GDN_DK), F32)],
        name="gdn_chunked",
        compiler_params=_cparams(("parallel", "arbitrary")),
    )(u, u, u, buf8, p["cw"], p["alog"], p["dtb"], p["ng"])


def _gdn_step_kernel(qkv_ref, z_ref, ab_ref, buf_ref, S0_ref, cw_ref, alog_ref, dtb_ref, ng_ref,
                     o_ref, nbuf_ref, S_ref, xbuf):
    H = GDN_HEADS
    xbuf[0:SUBLANES, :] = buf_ref[0]
    xbuf[SUBLANES:SUBLANES + 1, :] = qkv_ref[0]
    lo = SUBLANES - (CONV_W - 1)
    y = cw_ref[0:1, :] * xbuf[lo:lo + 1, :]
    for j in range(1, CONV_W):
        y = y + cw_ref[j:j + 1, :] * xbuf[lo + j:lo + j + 1, :]
    y = _silu(y)
    nbuf_ref[0] = xbuf[lo + 1:SUBLANES + 1, :]

    ab = ab_ref[0]
    g_all = -jnp.exp(alog_ref[...]) * _softplus(ab + dtb_ref[...])
    beta_all = _sigmoid(ab)
    ii = lax.broadcasted_iota(jnp.int32, (GDN_DK, GDN_DK), 0)
    jj = lax.broadcasted_iota(jnp.int32, (GDN_DK, GDN_DK), 1)
    eye = ii == jj
    ng = ng_ref[...]
    for h in range(H):
        q = y[:, h * GDN_DK:(h + 1) * GDN_DK]
        k = y[:, GDN_KW + h * GDN_DK:GDN_KW + (h + 1) * GDN_DK]
        v = y[:, 2 * GDN_KW + h * GDN_DV:2 * GDN_KW + (h + 1) * GDN_DV]
        q = q * lax.rsqrt(jnp.sum(q * q, axis=-1, keepdims=True) + EPS) * (GDN_DK ** -0.5)
        k = k * lax.rsqrt(jnp.sum(k * k, axis=-1, keepdims=True) + EPS)
        eg = jnp.exp(g_all[:, h:h + 1])
        bt = beta_all[:, H + h:H + h + 1]
        S = S0_ref[0, h]
        Sb = S.astype(BF16)
        kb = k.astype(BF16)
        qb = q.astype(BF16)
        u = bt * (v - eg * jnp.dot(kb, Sb, preferred_element_type=F32))
        ub = u.astype(BF16)
        qk = jnp.sum(qb.astype(F32) * kb.astype(F32), axis=-1, keepdims=True)
        o = eg * jnp.dot(qb, Sb, preferred_element_type=F32) + qk.astype(BF16).astype(F32) * ub.astype(F32)
        kdiag = jnp.where(eye, jnp.broadcast_to(kb.astype(F32), (GDN_DK, GDN_DK)), 0.0).astype(BF16)
        urows = jnp.broadcast_to(ub, (GDN_DK, GDN_DV))
        S_ref[0, h] = eg * S + jnp.dot(kdiag, urows, preferred_element_type=F32)
        zz = z_ref[0, :, h * GDN_DV:(h + 1) * GDN_DV]
        o_ref[0, :, h * GDN_DV:(h + 1) * GDN_DV] = (_rms(o, ng) * _silu(zz)).astype(BF16)


def _gdn_step(u, buf8, S0, p, l):
    B = u.shape[0]
    Cc = GDN_CONV_C
    H = GDN_HEADS
    return pl.pallas_call(
        _gdn_step_kernel,
        grid=(B,),
        in_specs=[pl.BlockSpec((1, 1, Cc), lambda b: (b, 0, U_QKV // Cc)),
                  pl.BlockSpec((1, 1, GDN_VW), lambda b: (b, 0, U_Z // GDN_VW)),
                  pl.BlockSpec((1, 1, W_AB), lambda b: (b, 0, U_AB // W_AB)),
                  pl.BlockSpec((None, 1, SUBLANES, Cc), lambda b: (l, b, 0, 0)),
                  pl.BlockSpec((None, 1, H, GDN_DK, GDN_DV), lambda b: (l, b, 0, 0, 0)),
                  _lspec((CONV_W, Cc), l), _lspec((1, LANES), l), _lspec((1, LANES), l), _lspec((1, GDN_DV), l)],
        out_specs=[pl.BlockSpec((1, 1, GDN_VW), lambda b: (b, 0, 0)),
                   pl.BlockSpec((1, CONV_W - 1, Cc), lambda b: (b, 0, 0)),
                   pl.BlockSpec((1, H, GDN_DK, GDN_DV), lambda b: (b, 0, 0, 0))],
        out_shape=[jax.ShapeDtypeStruct((B, 1, GDN_VW), BF16),
                   jax.ShapeDtypeStruct((B, CONV_W - 1, Cc), F32),
                   jax.ShapeDtypeStruct((B, H, GDN_DK, GDN_DV), F32)],
        scratch_shapes=[pltpu.VMEM((2 * SUBLANES, Cc), F32)],
        name="gdn_step",
        compiler_params=_cparams(("parallel",)),
    )(u, u, u, buf8, S0, p["cw"], p["alog"], p["dtb"], p["ng"])


def _mla_prep_kernel(mq_ref, mkv_ref, cos_ref, sin_ref, gq_ref, gkv_ref, wq_ref, wkv_ref,
                     q_ref, k_ref, v_ref, ckv_ref, kr_ref):
    H = MLA_HEADS
    W = LANES * H
    cos = cos_ref[0]
    sin = sin_ref[0]
    cq = _rms(mq_ref[0], gq_ref[...])
    qa = _dot(cq, wq_ref[...])
    mkv = mkv_ref[0]
    ckv = _rms(mkv[:, :KV_LORA], gkv_ref[...])
    ckv_ref[0] = ckv
    kro = mkv[:, KV_LORA:KV_LORA + LANES] * cos + mkv[:, KV_LORA + LANES:KV_LORA + 2 * LANES] * sin
    kr_ref[0] = kro[:, ROPE_LANE0:ROPE_LANE0 + QK_ROPE]
    kv = _dot(ckv, wkv_ref[...])
    for h in range(H):
        sl = slice(h * LANES, (h + 1) * LANES)
        qh = qa[:, sl] * cos + qa[:, W + h * LANES:W + (h + 1) * LANES] * sin
        q_ref[0, :, sl] = (qh * MLA_SCALE).astype(BF16)
        k_ref[0, :, sl] = (kv[:, sl] + kro).astype(BF16)
    v_ref[0] = kv[:, W:].astype(BF16)


def _mla_prep(u, cos, sin, p, l, bm):
    B, T, _ = u.shape
    W = LANES * MLA_HEADS
    return pl.pallas_call(
        _mla_prep_kernel,
        grid=(B, T // bm),
        in_specs=[pl.BlockSpec((1, bm, Q_LORA), lambda b, i: (b, i, U_MQ // Q_LORA)),
                  pl.BlockSpec((1, bm, W_MKV), lambda b, i: (b, i, U_MKV // W_MKV)),
                  pl.BlockSpec((1, bm, LANES), lambda b, i: (0, i, 0)),
                  pl.BlockSpec((1, bm, LANES), lambda b, i: (0, i, 0)),
                  _lspec((1, Q_LORA), l), _lspec((1, KV_LORA), l),
                  _lspec((Q_LORA, 2 * W), l), _lspec((KV_LORA, 2 * W), l)],
        out_specs=[pl.BlockSpec((1, bm, W), lambda b, i: (b, i, 0)),
                   pl.BlockSpec((1, bm, W), lambda b, i: (b, i, 0)),
                   pl.BlockSpec((1, bm, W), lambda b, i: (b, i, 0)),
                   pl.BlockSpec((1, bm, KV_LORA), lambda b, i: (b, i, 0)),
                   pl.BlockSpec((1, bm, QK_ROPE), lambda b, i: (b, i, 0))],
        out_shape=[jax.ShapeDtypeStruct((B, T, W), BF16),
                   jax.ShapeDtypeStruct((B, T, W), BF16),
                   jax.ShapeDtypeStruct((B, T, W), BF16),
                   jax.ShapeDtypeStruct((B, T, KV_LORA), F32),
                   jax.ShapeDtypeStruct((B, T, QK_ROPE), F32)],
        name="mla_prep",
        compiler_params=_cparams(("parallel", "parallel")),
    )(u, u, cos, sin, p["gq"], p["gkv"], p["wq"], p["wkv"])


NEG_BIG = -1e30


FLASH_HEADS = 4
FLASH_TK = 512


def _flash_kernel(q_ref, k_ref, v_ref, o_ref, *, tq):
    qi = pl.program_id(2)
    hp = FLASH_HEADS
    heads = [slice(h * LANES, (h + 1) * LANES) for h in range(hp)]
    qs = [q_ref[0, :, sl] for sl in heads]
    tk = min(FLASH_TK, tq)
    n_full = (qi * tq) // tk

    def step(j, carry, diagonal):
        ks = pl.ds(pl.multiple_of(j * tk, tk), tk)
        if diagonal:
            row = qi * tq + lax.broadcasted_iota(jnp.int32, (tq, tk), 0)
            col = j * tk + lax.broadcasted_iota(jnp.int32, (tq, tk), 1)
            causal = col <= row
        out = []
        for h, sl in enumerate(heads):
            m, l, acc = carry[h]
            s = lax.dot_general(qs[h], k_ref[0, ks, sl], (((1,), (1,)), ((), ())), preferred_element_type=F32)
            if diagonal:
                s = jnp.where(causal, s, NEG_BIG)
            m_new = jnp.maximum(m, jnp.max(s, axis=-1, keepdims=True))
            alpha = jnp.exp(m - m_new)
            p = jnp.exp(s - m_new)
            l = alpha * l + jnp.sum(p, axis=-1, keepdims=True)
            acc = alpha * acc + jnp.dot(p.astype(BF16), v_ref[0, ks, sl], preferred_element_type=F32)
            out.append((m_new, l, acc))
        return tuple(out)

    init = tuple((jnp.full((tq, 1), NEG_BIG, F32), jnp.zeros((tq, 1), F32), jnp.zeros((tq, LANES), F32))
                 for _ in heads)
    carry = lax.fori_loop(0, n_full, lambda j, c: step(j, c, False), init)
    for t in range(tq // tk):
        carry = step(n_full + t, carry, True)
    for h, sl in enumerate(heads):
        _, l, acc = carry[h]
        o_ref[0, :, sl] = (acc / l).astype(BF16)


def _flash(q, k, v, tq):
    B, T, W = q.shape
    wb = FLASH_HEADS * LANES
    H = W // wb
    return pl.pallas_call(
        functools.partial(_flash_kernel, tq=tq),
        grid=(B, H, T // tq),
        in_specs=[pl.BlockSpec((1, tq, wb), lambda b, h, i: (b, i, h)),
                  pl.BlockSpec((1, T, wb), lambda b, h, i: (b, 0, h)),
                  pl.BlockSpec((1, T, wb), lambda b, h, i: (b, 0, h))],
        out_specs=pl.BlockSpec((1, tq, wb), lambda b, h, i: (b, i, h)),
        out_shape=jax.ShapeDtypeStruct((B, T, W), BF16),
        name="mla_flash",
        compiler_params=_cparams(("parallel", "parallel", "arbitrary")),
    )(q, k, v)


DEC_GROUP = 32
DEC_SLOTS = 3


def _sattn_kernel(pt_ref, q_ref, cn_ref, krn_ref, wuk_ref, wuv_ref, ckv_hbm, krt_hbm, o_ref,
                  ck_buf, kr_buf, sem, *, layer, n_pages):
    H = MLA_HEADS
    G = DEC_GROUP
    n_groups = n_pages // G
    b = pl.program_id(0)
    nb = pl.num_programs(0)

    total = nb * n_groups
    ahead = DEC_SLOTS - 1

    def group_copies(x):
        seq, g, slot = x // n_groups, x % n_groups, x % DEC_SLOTS
        out = []
        for k in range(G):
            page = pt_ref[seq, g * G + k]
            out.append(pltpu.make_async_copy(ckv_hbm.at[layer, page],
                                             ck_buf.at[slot, pl.ds(k * PAGE_SIZE, PAGE_SIZE), :], sem.at[0, slot]))
            out.append(pltpu.make_async_copy(krt_hbm.at[layer, page],
                                             kr_buf.at[slot, :, pl.ds(k * PAGE_SIZE, PAGE_SIZE)], sem.at[1, slot]))
        return out

    @pl.when(b == 0)
    def _():
        for x in range(ahead):
            for c in group_copies(x):
                c.start()

    q = q_ref[0]
    q_rope = q[:, ROPE_LANE0:ROPE_LANE0 + QK_ROPE]
    rr = lax.broadcasted_iota(jnp.int32, (H, H * LANES), 0)
    cc = lax.broadcasted_iota(jnp.int32, (H, H * LANES), 1)
    q_bd = jnp.where(cc // LANES == rr, jnp.concatenate([q] * H, axis=1), jnp.zeros((), BF16))
    qlat = jnp.dot(q_bd, wuk_ref[...], preferred_element_type=F32).astype(BF16)

    def body(g, carry):
        m, l, acc = carry
        x = b * n_groups + g
        slot = x % DEC_SLOTS

        @pl.when(x + ahead < total)
        def _():
            for c in group_copies(x + ahead):
                c.start()

        for c in group_copies(x):
            c.wait()
        ck = ck_buf[slot].astype(BF16)
        kr_t = kr_buf[slot].astype(BF16)
        s = _dot_nt(qlat, ck) + jnp.dot(q_rope, kr_t, preferred_element_type=F32)
        m_new = jnp.maximum(m, jnp.max(s, axis=-1, keepdims=True))
        alpha = jnp.exp(m - m_new)
        p = jnp.exp(s - m_new)
        l = alpha * l + jnp.sum(p, axis=-1, keepdims=True)
        acc = alpha * acc + jnp.dot(p.astype(BF16), ck, preferred_element_type=F32)
        return m_new, l, acc

    init = (jnp.full((H, 1), NEG_BIG, F32), jnp.zeros((H, 1), F32), jnp.zeros((H, KV_LORA), F32))
    m, l, acc = lax.fori_loop(0, n_groups, body, init)

    cn = cn_ref[0].astype(BF16).astype(F32)
    krn = krn_ref[0].astype(BF16).astype(F32)
    s_n = (jnp.sum(qlat.astype(F32) * cn, axis=-1, keepdims=True)
           + jnp.sum(q_rope.astype(F32) * krn, axis=-1, keepdims=True))
    m_f = jnp.maximum(m, s_n)
    a_f = jnp.exp(m - m_f)
    p_n = jnp.exp(s_n - m_f)
    l_f = a_f * l + p_n
    o_lat = (a_f * acc + p_n.astype(BF16).astype(F32) * cn) / l_f
    o_all = jnp.dot(o_lat.astype(BF16), wuv_ref[...], preferred_element_type=F32)
    o_ref[0] = jnp.sum(jnp.where(cc // LANES == rr, o_all, 0.0), axis=0, keepdims=True).astype(BF16)


def _sample_attention(page_table, q, ckv_new, kr_new, wuk_s, wuv_c, cache_ckv, cache_krope_t, layer):
    Bd = q.shape[0]
    H = MLA_HEADS
    n_pages = page_table.shape[1]
    assert n_pages % DEC_GROUP == 0 and Bd * (n_pages // DEC_GROUP) >= DEC_SLOTS
    q3 = q.reshape(Bd, H, LANES)
    grid_spec = pltpu.PrefetchScalarGridSpec(
        num_scalar_prefetch=1,
        grid=(Bd,),
        in_specs=[pl.BlockSpec((1, H, LANES), lambda b, pt: (b, 0, 0)),
                  pl.BlockSpec((1, 1, KV_LORA), lambda b, pt: (b, 0, 0)),
                  pl.BlockSpec((1, 1, QK_ROPE), lambda b, pt: (b, 0, 0)),
                  _lspec((H * LANES, KV_LORA), layer), _lspec((KV_LORA, H * V_HEAD), layer),
                  pl.BlockSpec(memory_space=pl.ANY), pl.BlockSpec(memory_space=pl.ANY)],
        out_specs=pl.BlockSpec((1, 1, H * V_HEAD), lambda b, pt: (b, 0, 0)),
        scratch_shapes=[pltpu.VMEM((DEC_SLOTS, DEC_GROUP * PAGE_SIZE, KV_LORA), F32),
                        pltpu.VMEM((DEC_SLOTS, QK_ROPE, DEC_GROUP * PAGE_SIZE), F32),
                        pltpu.SemaphoreType.DMA((2, DEC_SLOTS))],
    )
    return pl.pallas_call(
        functools.partial(_sattn_kernel, layer=layer, n_pages=n_pages),
        grid_spec=grid_spec,
        out_shape=jax.ShapeDtypeStruct((Bd, 1, H * V_HEAD), BF16),
        name="mla_decode",
        compiler_params=_cparams(("arbitrary",)),
    )(page_table, q3, ckv_new, kr_new, wuk_s, wuv_c, cache_ckv, cache_krope_t)


def _merge_kernel(x_ref, org_ref, ogdn_ref, omla_ref, ga_ref, gb_ref, gc_ref, gt_ref,
                  wrg_ref, wgdn_ref, wmla_ref, wo_ref, o_ref):
    d = functools.partial(jnp.dot, preferred_element_type=F32)
    m = (_sigmoid(ga_ref[0]) * d(org_ref[0], wrg_ref[...])
         + _sigmoid(gb_ref[0]) * d(ogdn_ref[0], wgdn_ref[...])
         + _sigmoid(gc_ref[0]) * d(omla_ref[0], wmla_ref[...]))
    o_ref[0] = x_ref[0] + gt_ref[...] * d(m.astype(BF16), wo_ref[...])


def _merge(x, o_rg, o_gdn, o_mla, u, mod, p, l, bm):
    B, T, D = x.shape
    tok = lambda w, col: pl.BlockSpec((1, bm, w), lambda b, i: (b, i, col))
    return pl.pallas_call(
        _merge_kernel,
        grid=(B, T // bm),
        in_specs=[tok(D, 0), tok(D_RNN, 0), tok(GDN_VW, 0), tok(D, 0),
                  tok(D, U_GA // D), tok(D, U_GB // D), tok(D, U_GC // D),
                  mod.spec(MOD_GT1, bm),
                  _lspec((D_RNN, D), l), _lspec((GDN_VW, D), l), _lspec((D, D), l), _lspec((D, D), l)],
        out_specs=tok(D, 0),
        out_shape=jax.ShapeDtypeStruct((B, T, D), F32),
        name="branch_merge",
        compiler_params=_cparams(("parallel", "parallel")),
    )(x, o_rg, o_gdn, o_mla, u, u, u, mod.array, p["wrg"], p["wgdn"], p["wmla"], p["wo"])


def _ffn_kernel(x_ref, sc_ref, sh_ref, gt_ref, g_ref, gf_ref, wg_ref, wu_ref, wd_ref, o_ref, hb_ref, acc_ref, *, final):
    j = pl.program_id(2)

    @pl.when(j == 0)
    def _():
        h = _rms(x_ref[0], g_ref[...]) * (1.0 + sc_ref[...]) + sh_ref[...]
        hb_ref[...] = h.astype(BF16)
        acc_ref[...] = jnp.zeros(acc_ref.shape, F32)

    hb = hb_ref[...]
    gate = jnp.dot(hb, wg_ref[...], preferred_element_type=F32)
    up = jnp.dot(hb, wu_ref[...], preferred_element_type=F32)
    acc_ref[...] += jnp.dot((_silu(gate) * up).astype(BF16), wd_ref[...], preferred_element_type=F32)

    @pl.when(j == pl.num_programs(2) - 1)
    def _():
        y = x_ref[0] + gt_ref[...] * acc_ref[...]
        o_ref[0] = _rms(y, gf_ref[...]) if final else y


def _ffn(x, mod, g, g_final, w_in_b, w_out_b, l, bm, bf, final):
    B, T, D = x.shape
    nf = D_FF // bf
    return pl.pallas_call(
        functools.partial(_ffn_kernel, final=final),
        grid=(B, T // bm, nf),
        in_specs=[pl.BlockSpec((1, bm, D), lambda b, i, j: (b, i, 0)),
                  mod.spec(MOD_SC2, bm), mod.spec(MOD_SH2, bm), mod.spec(MOD_GT2, bm),
                  _lspec((1, D), l),
                  pl.BlockSpec((1, D), lambda b, i, j: (0, 0)),
                  pl.BlockSpec((None, D, bf), lambda b, i, j: (l, 0, j)),
                  pl.BlockSpec((None, D, bf), lambda b, i, j: (l, 0, nf + j)),
                  pl.BlockSpec((None, bf, D), lambda b, i, j: (l, j, 0))],
        out_specs=pl.BlockSpec((1, bm, D), lambda b, i, j: (b, i, 0)),
        out_shape=jax.ShapeDtypeStruct((B, T, D), F32),
        scratch_shapes=[pltpu.VMEM((bm, D), BF16), pltpu.VMEM((bm, D), F32)],
        name="swiglu",
        compiler_params=_cparams(("parallel", "parallel", "arbitrary")),
    )(x, mod.array, mod.array, mod.array, g, g_final, w_in_b, w_in_b, w_out_b)


def _rot_half(w):
    half = QK_ROPE // 2
    return jnp.concatenate([-w[..., half:], w[..., :half]], axis=-1)


def _pack_w_in(w):
    L, D, _ = w.shape
    wt = jnp.swapaxes(w, 1, 2)
    offs = np.cumsum((0,) + IN_SIZES)
    rx, ry, qkv, z, a, b, mq, mkv, gate = [wt[:, offs[i]:offs[i + 1]] for i in range(len(IN_SIZES))]
    zeros = lambda n: jnp.zeros((L, n, D), w.dtype)
    kr = mkv[:, KV_LORA:]
    half = QK_ROPE // 2
    kr_rot = jnp.concatenate([-kr[:, half:], kr[:, :half]], axis=1)
    tail = LANES - ROPE_LANE0 - QK_ROPE
    rows = [rx, ry, zeros(U_QKV - 2 * D_RNN), qkv, z, gate,
            a, b, zeros(W_AB - 2 * GDN_HEADS), mq,
            mkv[:, :KV_LORA], zeros(ROPE_LANE0), kr, zeros(tail), zeros(ROPE_LANE0), kr_rot, zeros(tail)]
    out = jnp.concatenate(rows, axis=1).astype(BF16)
    assert out.shape[1] == N_U
    return out


def _pack_w_uq(w):
    H = MLA_HEADS
    L = w.shape[0]
    w = w.reshape(L, Q_LORA, H, QK_NOPE + QK_ROPE)
    nope, rope = w[..., :QK_NOPE], w[..., QK_NOPE:]
    tail = jnp.zeros((L, Q_LORA, H, LANES - QK_NOPE - QK_ROPE), w.dtype)
    a = jnp.concatenate([nope, rope, tail], axis=-1).reshape(L, Q_LORA, H * LANES)
    b = jnp.concatenate([jnp.zeros_like(nope), _rot_half(rope), tail], axis=-1).reshape(L, Q_LORA, H * LANES)
    return jnp.concatenate([a, b], axis=-1).astype(BF16)


def _pack_w_ukv(w):
    H = MLA_HEADS
    L = w.shape[0]
    w3 = w.reshape(L, KV_LORA, H, QK_NOPE + V_HEAD)
    w_uk, w_uv = w3[..., :QK_NOPE], w3[..., QK_NOPE:]
    kpad = jnp.concatenate([w_uk, jnp.zeros((L, KV_LORA, H, LANES - QK_NOPE), w.dtype)], axis=-1)
    wuv_c = w_uv.reshape(L, KV_LORA, H * V_HEAD)
    wkv_p = jnp.concatenate([kpad.reshape(L, KV_LORA, H * LANES), wuv_c], axis=-1)
    wuk_s = jnp.transpose(kpad, (0, 2, 3, 1)).reshape(L, H * LANES, KV_LORA)
    return wkv_p.astype(BF16), wuk_s.astype(BF16), wuv_c.astype(BF16)


def _rope_tables(T, pos0):
    inv = ROPE_BASE ** (-jnp.arange(0, QK_ROPE, 2, dtype=F32) / QK_ROPE)
    ang = (jnp.arange(T, dtype=F32) + pos0)[:, None] * inv[None, :]
    cos, sin = jnp.cos(ang), jnp.sin(ang)
    tail = jnp.zeros((T, LANES - ROPE_LANE0 - QK_ROPE), F32)
    cos_t = jnp.concatenate([jnp.ones((T, ROPE_LANE0), F32), cos, cos, tail], axis=1)
    sin_t = jnp.concatenate([jnp.zeros((T, ROPE_LANE0), F32), sin, sin, tail], axis=1)
    return cos_t[None], sin_t[None]


def kernel(x_prompt, x_sample, cache_ckv, cache_krope, state_rg_conv, state_rg_h, state_gdn_conv, state_gdn_S,
           page_table, c_prompt, c_sample, w_ada, b_ada, g_norm1, g_norm2, w_in, rg_conv_w, rg_conv_b, rg_wa,
           rg_ba, rg_wx, rg_bx, rg_lambda, gdn_conv_w, gdn_A_log, gdn_dt_bias, gdn_norm_g, mla_q_norm_g, w_uq,
           mla_kv_norm_g, w_ukv, w_rg_proj, w_gdn_proj, w_mla_proj, w_o, w_ffn_in, w_ffn_out, g_final):
    L = w_in.shape[0]
    Bp, T, D = x_prompt.shape
    Bd = x_sample.shape[0]
    past_len = page_table.shape[1] * PAGE_SIZE

    mod = _modulation(jnp.concatenate([c_sample, c_prompt], axis=0), w_ada, b_ada)
    cos_p, sin_p = _rope_tables(T, 0.0)
    cos_s, sin_s = _rope_tables(1, float(past_len))
    cos_s = jnp.broadcast_to(cos_s, (1, Bd, LANES))
    sin_s = jnp.broadcast_to(sin_s, (1, Bd, LANES))

    row = lambda v: v.reshape(L, 1, v.shape[-1])
    lane_row = lambda v: jnp.pad(v, ((0, 0), (0, LANES - v.shape[-1]))).reshape(L, 1, LANES)
    pad8 = lambda s: jnp.pad(s, ((0, 0), (0, 0), (SUBLANES - (CONV_W - 1), 0), (0, 0)))
    g1, g2, gf = row(g_norm1), row(g_norm2), g_final.reshape(1, D)
    w_in_p = _pack_w_in(w_in)
    wkv_p, wuk_s, wuv_c = _pack_w_ukv(w_ukv)
    rg_p = dict(cw=rg_conv_w, cb=row(rg_conv_b), wa=rg_wa.astype(BF16), ba=row(rg_ba),
                wx=rg_wx.astype(BF16), bx=row(rg_bx), lam=row(rg_lambda))
    gdn_p = dict(cw=gdn_conv_w, alog=lane_row(gdn_A_log), dtb=lane_row(gdn_dt_bias), ng=row(gdn_norm_g))
    mla_p = dict(gq=row(mla_q_norm_g), gkv=row(mla_kv_norm_g), wq=_pack_w_uq(w_uq), wkv=wkv_p)
    mrg_p = dict(wrg=w_rg_proj.astype(BF16), wgdn=w_gdn_proj.astype(BF16), wmla=w_mla_proj.astype(BF16),
                 wo=w_o.astype(BF16))
    wfi, wfo = w_ffn_in.astype(BF16), w_ffn_out.astype(BF16)
    rg_buf_s, gdn_buf_s = pad8(state_rg_conv), pad8(state_gdn_conv)
    rg_h_s = state_rg_h.reshape(L, Bd, 1, D_RNN)
    rg_buf_0 = jnp.zeros((1, Bp, SUBLANES, D_RNN), F32)
    rg_h_0 = jnp.zeros((1, Bp, 1, D_RNN), F32)
    gdn_buf_0 = jnp.zeros((1, Bp, SUBLANES, GDN_CONV_C), F32)
    cache_krope_t = jnp.swapaxes(cache_krope, 2, 3)

    xp = x_prompt
    xs = x_sample.reshape(1, Bd, D)
    outs_p, outs_s = [], []
    for l in range(L):
        final = l == L - 1

        m_p = _Mod(mod, l, per_token=False, row0=Bd)
        u = _in_proj(xp, m_p, g1, w_in_p, l, bm=min(2048, T), bn=1024)
        o_rg, rg_buf, rg_h = _rglru(u, rg_buf_0, rg_h_0, 0, rg_p, l, tt=128)
        o_gdn, gdn_buf, gdn_S = _gdn_prompt(u, gdn_buf_0, gdn_p, l, tt=256)
        q, k, v, ckv, kr = _mla_prep(u, cos_p, sin_p, mla_p, l, bm=512)
        o_mla = _flash(q, k, v, tq=min(512, T))
        x1 = _merge(xp, o_rg, o_gdn, o_mla, u, m_p, mrg_p, l, bm=512)
        xp = _ffn(x1, m_p, g2, gf, wfi, wfo, l, bm=512, bf=1408, final=final)
        outs_p.append((ckv, kr, rg_buf, rg_h.reshape(Bp, D_RNN), gdn_buf, gdn_S))

        m_s = _Mod(mod, l, per_token=True, row0=0)
        us = _in_proj(xs, m_s, g1, w_in_p, l, bm=Bd, bn=1024)
        us_seq = us.reshape(Bd, 1, N_U)
        o_rg, rg_buf, rg_h = _rglru(us_seq, rg_buf_s, rg_h_s, l, rg_p, l, tt=1)
        o_gdn, gdn_buf, gdn_S = _gdn_step(us_seq, gdn_buf_s, state_gdn_S, gdn_p, l)
        q, k, v, ckv, kr = _mla_prep(us, cos_s, sin_s, mla_p, l, bm=Bd)
        o_mla = _sample_attention(page_table, q.reshape(Bd, MLA_HEADS * LANES), ckv.reshape(Bd, 1, KV_LORA),
                                  kr.reshape(Bd, 1, QK_ROPE), wuk_s, wuv_c, cache_ckv, cache_krope_t, l)
        x1 = _merge(xs, o_rg.reshape(1, Bd, D_RNN), o_gdn.reshape(1, Bd, GDN_VW), o_mla.reshape(1, Bd, D),
                    us, m_s, mrg_p, l, bm=Bd)
        xs = _ffn(x1, m_s, g2, gf, wfi, wfo, l, bm=Bd, bf=1408, final=final)
        outs_s.append((ckv.reshape(Bd, 1, KV_LORA), kr.reshape(Bd, 1, QK_ROPE), rg_buf, rg_h.reshape(Bd, D_RNN),
                       gdn_buf, gdn_S))

    stack = lambda outs, i: jnp.stack([o[i] for o in outs])
    return (xp, xs.reshape(Bd, 1, D),
            stack(outs_p, 0), stack(outs_p, 1), stack(outs_p, 2), stack(outs_p, 3), stack(outs_p, 4), stack(outs_p, 5),
            stack(outs_s, 0), stack(outs_s, 1), stack(outs_s, 2), stack(outs_s, 3), stack(outs_s, 4), stack(outs_s, 5))
```

```python
import functools
import math

import numpy as np
import jax
import jax.numpy as jnp
from jax import lax
from jax.experimental import pallas as pl
from jax.experimental.pallas import tpu as pltpu

F32 = jnp.float32
BF16 = jnp.bfloat16

D_MODEL = 1024
CONV_W = 4
D_RNN = 1280
RG_BLOCK = 128
RG_BLOCKS = D_RNN // RG_BLOCK
RG_C = 8.0
GDN_HEADS = 8
GDN_DK = 128
GDN_DV = 128
GDN_KW = GDN_HEADS * GDN_DK
GDN_VW = GDN_HEADS * GDN_DV
GDN_CONV_C = 2 * GDN_KW + GDN_VW
GDN_CHUNK = 64
MLA_HEADS = 8
Q_LORA = 384
KV_LORA = 256
QK_NOPE = 64
QK_ROPE = 32
V_HEAD = 128
MLA_SCALE = (QK_NOPE + QK_ROPE) ** -0.5
ROPE_BASE = 10000.0
D_FF = 2816
N_BRANCH = 3
IN_SIZES = (D_RNN, D_RNN, GDN_CONV_C, GDN_VW, GDN_HEADS, GDN_HEADS, Q_LORA, KV_LORA + QK_ROPE, N_BRANCH * D_MODEL)
EPS = 1e-6
PAGE_SIZE = 128

LANES = 128
SUBLANES = 8
VMEM_LIMIT = 56 * 1024 * 1024

U_RX = 0
U_RY = D_RNN
U_QKV = 3072
U_Z = 6144
U_GA = 7168
U_GB = 8192
U_GC = 9216
U_AB = 10240
U_MQ = 10368
U_MKV = 10752
N_U = 11264
W_AB = LANES
W_MKV = 512
ROPE_LANE0 = QK_NOPE


def _cparams(sem):
    return pltpu.CompilerParams(dimension_semantics=sem, vmem_limit_bytes=VMEM_LIMIT)


def _sigmoid(x):
    return 1.0 / (1.0 + jnp.exp(-x))


def _silu(x):
    return x * _sigmoid(x)


def _softplus(x):
    return jnp.maximum(x, 0.0) + jnp.log1p(jnp.exp(-jnp.abs(x)))


def _gelu_tanh(x):
    c = math.sqrt(2.0 / math.pi)
    return 0.5 * x * (1.0 + jnp.tanh(c * (x + 0.044715 * (x * x * x))))


def _rms(x, g):
    return x * lax.rsqrt(jnp.mean(x * x, axis=-1, keepdims=True) + EPS) * g


def _dot(a, b):
    return jnp.dot(a.astype(BF16), b.astype(BF16), preferred_element_type=F32)


def _dot_nt(a, b):
    return lax.dot_general(a.astype(BF16), b.astype(BF16), (((1,), (1,)), ((), ())), preferred_element_type=F32)


def _dot_tn(a, b):
    return lax.dot_general(a.astype(BF16), b.astype(BF16), (((0,), (0,)), ((), ())), preferred_element_type=F32)


def _split2(a):
    hi = a.astype(BF16)
    lo = (a - hi.astype(F32)).astype(BF16)
    return hi, lo


def _dot3(a, b):
    ah, al = _split2(a)
    bh, bl = _split2(b)
    d = functools.partial(jnp.dot, preferred_element_type=F32)
    return d(ah, bh) + (d(ah, bl) + d(al, bh))


def _dot_exact_lhs(a_bf16, b):
    b0 = b.astype(BF16)
    r1 = b - b0.astype(F32)
    b1 = r1.astype(BF16)
    b2 = (r1 - b1.astype(F32)).astype(BF16)
    d = functools.partial(jnp.dot, preferred_element_type=F32)
    return d(a_bf16, b0) + (d(a_bf16, b1) + d(a_bf16, b2))


def _mod_kernel(c_ref, w_ref, b_ref, o_ref):
    c = c_ref[...]
    o_ref[0] = _dot(_silu(c), w_ref[0]) + b_ref[0]


def _modulation(c_all, w_ada, b_ada):
    L, D, N = w_ada.shape
    R = c_all.shape[0]
    bn = 3072
    return pl.pallas_call(
        _mod_kernel,
        grid=(L, N // bn),
        in_specs=[pl.BlockSpec((R, D), lambda l, j: (0, 0)),
                  pl.BlockSpec((1, D, bn), lambda l, j: (l, 0, j)),
                  pl.BlockSpec((1, 1, bn), lambda l, j: (l, 0, j))],
        out_specs=pl.BlockSpec((1, R, bn), lambda l, j: (l, 0, j)),
        out_shape=jax.ShapeDtypeStruct((L, R, N), F32),
        name="adaln_mod",
        compiler_params=_cparams(("parallel", "parallel")),
    )(c_all, w_ada, b_ada.reshape(L, 1, N))


MOD_SH1, MOD_SC1, MOD_GT1, MOD_SH2, MOD_SC2, MOD_GT2 = range(6)


class _Mod:
    def __init__(self, mod, layer, per_token, row0):
        self.layer, self.per_token, self.row0 = layer, per_token, row0
        L, R, N = mod.shape
        self.array = mod if per_token else mod.reshape(L, R, 1, N)

    def spec(self, part, bm):
        l, r0 = self.layer, self.row0
        if self.per_token:
            assert r0 == 0
            return pl.BlockSpec((None, bm, D_MODEL), lambda b, i, *_: (l, i, part))
        return pl.BlockSpec((None, None, 1, D_MODEL), lambda b, i, *_: (l, r0 + b, 0, part))


def _lspec(shape, l):
    return pl.BlockSpec((None,) + tuple(shape), lambda *_: (l,) + (0,) * len(shape))


def _in_kernel(x_ref, sc_ref, sh_ref, g_ref, w_ref, o_ref, hb_ref):
    @pl.when(pl.program_id(2) == 0)
    def _():
        h = _rms(x_ref[0], g_ref[...]) * (1.0 + sc_ref[...]) + sh_ref[...]
        hb_ref[...] = h.astype(BF16)

    o_ref[0] = lax.dot_general(hb_ref[...], w_ref[...], (((1,), (1,)), ((), ())), preferred_element_type=F32)


def _in_proj(x, mod, g, w_packed, l, bm, bn):
    B, T, D = x.shape
    N = w_packed.shape[1]
    return pl.pallas_call(
        _in_kernel,
        grid=(B, T // bm, N // bn),
        in_specs=[pl.BlockSpec((1, bm, D), lambda b, i, j: (b, i, 0)),
                  mod.spec(MOD_SC1, bm), mod.spec(MOD_SH1, bm),
                  _lspec((1, D), l),
                  pl.BlockSpec((None, bn, D), lambda b, i, j: (l, j, 0))],
        out_specs=pl.BlockSpec((1, bm, bn), lambda b, i, j: (b, i, j)),
        out_shape=jax.ShapeDtypeStruct((B, T, N), F32),
        scratch_shapes=[pltpu.VMEM((bm, D), BF16)],
        name="in_proj",
        compiler_params=_cparams(("parallel", "parallel", "arbitrary")),
    )(x, mod.array, mod.array, g, w_packed)


SCAN_PAD = 128


def _rg_kernel(ux_ref, uy_ref, buf_ref, h0_ref, cw_ref, cb_ref, wa_ref, ba_ref, wx_ref, bx_ref, lam_ref,
               o_ref, nbuf_ref, hl_ref, xbuf, abuf, bbuf, hc, *, tt):
    i = pl.program_id(1)
    nt = pl.num_programs(1)

    @pl.when(i == 0)
    def _():
        xbuf[0:SUBLANES, :] = buf_ref[0]
        hc[...] = h0_ref[0]
        abuf[0:SCAN_PAD, :] = jnp.ones((SCAN_PAD, D_RNN), F32)
        bbuf[0:SCAN_PAD, :] = jnp.zeros((SCAN_PAD, D_RNN), F32)

    xbuf[SUBLANES:SUBLANES + tt, :] = ux_ref[0]
    lo = SUBLANES - (CONV_W - 1)
    if tt >= SUBLANES:
        xs = xbuf[...]
        xc = cb_ref[...] + cw_ref[CONV_W - 1:CONV_W, :] * xs[SUBLANES:, :]
        for j in range(CONV_W - 1):
            xc = xc + cw_ref[j:j + 1, :] * pltpu.roll(xs, CONV_W - 1 - j, axis=0)[SUBLANES:, :]
    else:
        xc = cb_ref[...] + cw_ref[0:1, :] * xbuf[lo:lo + tt, :]
        for j in range(1, CONV_W):
            xc = xc + cw_ref[j:j + 1, :] * xbuf[lo + j:lo + j + tt, :]

    @pl.when(i == nt - 1)
    def _():
        nbuf_ref[0] = xbuf[tt + lo:tt + SUBLANES, :]

    if tt >= SUBLANES:
        xbuf[0:SUBLANES, :] = xbuf[tt:tt + SUBLANES, :]

    xb = xc.astype(BF16)
    ra = jnp.concatenate([jnp.dot(xb[:, n * RG_BLOCK:(n + 1) * RG_BLOCK], wa_ref[n], preferred_element_type=F32)
                          for n in range(RG_BLOCKS)], axis=1)
    ri = jnp.concatenate([jnp.dot(xb[:, n * RG_BLOCK:(n + 1) * RG_BLOCK], wx_ref[n], preferred_element_type=F32)
                          for n in range(RG_BLOCKS)], axis=1)
    r = _sigmoid(ra + ba_ref[...])
    ig = _sigmoid(ri + bx_ref[...])
    log_a = (-RG_C) * r * _softplus(-lam_ref[...])
    a = jnp.exp(log_a)
    b = jnp.sqrt(-jnp.tanh(log_a) * (a * a + 1.0)) * (ig * xc)

    abuf[SCAN_PAD:SCAN_PAD + tt, :] = a
    bbuf[SCAN_PAD:SCAN_PAD + tt, :] = b
    d = 1
    while d < tt:
        a_s = abuf[SCAN_PAD - d:SCAN_PAD - d + tt, :]
        b_s = bbuf[SCAN_PAD - d:SCAN_PAD - d + tt, :]
        a0 = abuf[SCAN_PAD:SCAN_PAD + tt, :]
        b0 = bbuf[SCAN_PAD:SCAN_PAD + tt, :]
        abuf[SCAN_PAD:SCAN_PAD + tt, :] = a0 * a_s
        bbuf[SCAN_PAD:SCAN_PAD + tt, :] = a0 * b_s + b0
        d *= 2
    h = bbuf[SCAN_PAD:SCAN_PAD + tt, :] + abuf[SCAN_PAD:SCAN_PAD + tt, :] * hc[...]
    hc[...] = h[tt - 1:tt, :]
    o_ref[0] = (h * _gelu_tanh(uy_ref[0])).astype(BF16)

    @pl.when(i == nt - 1)
    def _():
        hl_ref[0] = h[tt - 1:tt, :]


def _rglru(u, buf8, h0, ls, p, l, tt):
    B, T, _ = u.shape
    C = D_RNN
    wblk = (RG_BLOCKS, RG_BLOCK, RG_BLOCK)
    return pl.pallas_call(
        functools.partial(_rg_kernel, tt=tt),
        grid=(B, T // tt),
        in_specs=[pl.BlockSpec((1, tt, C), lambda b, i: (b, i, U_RX // C)),
                  pl.BlockSpec((1, tt, C), lambda b, i: (b, i, U_RY // C)),
                  pl.BlockSpec((None, 1, SUBLANES, C), lambda b, i: (ls, b, 0, 0)),
                  pl.BlockSpec((None, 1, 1, C), lambda b, i: (ls, b, 0, 0)),
                  _lspec((CONV_W, C), l), _lspec((1, C), l),
                  _lspec(wblk, l), _lspec((1, C), l),
                  _lspec(wblk, l), _lspec((1, C), l), _lspec((1, C), l)],
        out_specs=[pl.BlockSpec((1, tt, C), lambda b, i: (b, i, 0)),
                   pl.BlockSpec((1, CONV_W - 1, C), lambda b, i: (b, 0, 0)),
                   pl.BlockSpec((1, 1, C), lambda b, i: (b, 0, 0))],
        out_shape=[jax.ShapeDtypeStruct((B, T, C), BF16),
                   jax.ShapeDtypeStruct((B, CONV_W - 1, C), F32),
                   jax.ShapeDtypeStruct((B, 1, C), F32)],
        scratch_shapes=[pltpu.VMEM((tt + SUBLANES, C), F32),
                        pltpu.VMEM((SCAN_PAD + tt, C), F32),
                        pltpu.VMEM((SCAN_PAD + tt, C), F32),
                        pltpu.VMEM((1, C), F32)],
        name="rglru",
        compiler_params=_cparams(("parallel", "arbitrary")),
    )(u, u, buf8, h0, p["cw"], p["cb"], p["wa"], p["ba"], p["wx"], p["bx"], p["lam"])


TRI_BASE = 8
INV_GROUP = 8
NEWTON_STEPS = 2


def _cat_dot3(x, y, bd_b, n):
    C = x.shape[0]
    xh, xl = _split2(x)
    yh, yl = _split2(y)
    d = functools.partial(jnp.dot, preferred_element_type=F32)
    r = d(jnp.concatenate([xh, xl], axis=0), jnp.concatenate([yh] * n, axis=0) * bd_b)
    return r[:C] + (r[C:] + d(xh, jnp.concatenate([yl] * n, axis=0) * bd_b))


def _cat_dot1(x, y, bd_b, n):
    yb = jnp.concatenate([y.astype(BF16)] * n, axis=0) * bd_b
    return jnp.dot(x.astype(BF16), yb, preferred_element_type=F32)


def _tri_inv_cat(Lcs, masks, bd_b, n):
    eye, base_mask, level_masks = masks
    mm = functools.partial(_cat_dot1, bd_b=bd_b, n=n)
    mm3 = functools.partial(_cat_dot3, bd_b=bd_b, n=n)
    N = [jnp.where(base_mask, -Lc, 0.0) for Lc in Lcs]
    P = [eye + x for x in N]
    N2 = [mm(x, x) for x in N]
    P = [p + mm(p, x) for p, x in zip(P, N2)]
    N4 = [mm(x, x) for x in N2]
    P = [p + mm(p, x) for p, x in zip(P, N4)]
    for m in level_masks:
        PO = [mm(p, jnp.where(m, Lc, 0.0)) for p, Lc in zip(P, Lcs)]
        P = [p - mm(po, p) for p, po in zip(P, PO)]
    for _ in range(NEWTON_STEPS):
        R = [eye - p - mm3(Lc, p) for p, Lc in zip(P, Lcs)]
        P = [p + mm(p, r) for p, r in zip(P, R)]
    return P


def _cat_masks(C, n):
    rr = lax.broadcasted_iota(jnp.int32, (C, n * C), 0)
    jj = lax.broadcasted_iota(jnp.int32, (C, n * C), 1) % C
    same = lambda s: (rr // s) == (jj // s)
    eye = (rr == jj).astype(F32)
    levels = []
    s = TRI_BASE
    while s < C:
        levels.append(jnp.logical_and(same(2 * s), jnp.logical_not(same(s))))
        s *= 2
    return eye, same(TRI_BASE), levels


def _gdn_kernel(qkv_ref, z_ref, ab_ref, buf_ref, cw_ref, alog_ref, dtb_ref, ng_ref,
                o_ref, nbuf_ref, S_ref, xbuf, act, S_sc, wq_sc, uv_sc, qk_sc, kd_sc, gl_sc, lc_sc, rhs_sc, *, tt):
    i = pl.program_id(1)
    nt = pl.num_programs(1)
    C = GDN_CHUNK
    H = GDN_HEADS
    n = tt // C

    @pl.when(i == 0)
    def _():
        xbuf[0:SUBLANES, :] = buf_ref[0]
        S_sc[...] = jnp.zeros(S_sc.shape, F32)

    xbuf[SUBLANES:SUBLANES + tt, :] = qkv_ref[0]
    lo = SUBLANES - (CONV_W - 1)
    xs = xbuf[...]
    y = cw_ref[CONV_W - 1:CONV_W, :] * xs[SUBLANES:, :]
    for j in range(CONV_W - 1):
        y = y + cw_ref[j:j + 1, :] * pltpu.roll(xs, CONV_W - 1 - j, axis=0)[SUBLANES:, :]
    act[...] = _silu(y)

    @pl.when(i == nt - 1)
    def _():
        nbuf_ref[0] = xbuf[tt + lo:tt + SUBLANES, :]

    xbuf[0:SUBLANES, :] = xbuf[tt:tt + SUBLANES, :]

    ab = ab_ref[0]
    g = -jnp.exp(alog_ref[...]) * _softplus(ab + dtb_ref[...])
    beta = _sigmoid(ab)

    ri = lax.broadcasted_iota(jnp.int32, (tt, tt), 0)
    ci = lax.broadcasted_iota(jnp.int32, (tt, tt), 1)
    bd = (ri // C) == (ci // C)
    bd_incl = jnp.logical_and(bd, ri >= ci)
    bd_strict = jnp.logical_and(bd, ri > ci)
    bd_b = bd.astype(BF16)
    last_b = jnp.logical_and(bd, ci % C == C - 1).astype(BF16)
    gcum = _dot_exact_lhs(bd_incl.astype(BF16), g)
    glast = _dot_exact_lhs(last_b, gcum)
    gl_sc[...] = jnp.exp(glast)
    gcum_t = gcum.T
    masks = _cat_masks(C, n)
    ng = ng_ref[...]

    for h in range(H):
        q = act[:, h * GDN_DK:(h + 1) * GDN_DK]
        k = act[:, GDN_KW + h * GDN_DK:GDN_KW + (h + 1) * GDN_DK]
        v = act[:, 2 * GDN_KW + h * GDN_DV:2 * GDN_KW + (h + 1) * GDN_DV]
        q = q * lax.rsqrt(jnp.sum(q * q, axis=-1, keepdims=True) + EPS) * (GDN_DK ** -0.5)
        k = k * lax.rsqrt(jnp.sum(k * k, axis=-1, keepdims=True) + EPS)
        gc = gcum[:, h:h + 1]
        gr = gcum_t[h:h + 1, :]
        e = jnp.exp(jnp.where(bd_incl, gc - gr, -jnp.inf))
        bt = beta[:, H + h:H + h + 1]
        kb = k.astype(BF16)
        qb = q.astype(BF16)
        Lf = jnp.where(bd_strict, bt * _dot_nt(kb, kb) * e, 0.0)
        qkd = (_dot_nt(qb, kb) * e).astype(BF16)
        for c in range(n):
            qk_sc[h, c] = qkd[c * C:(c + 1) * C, c * C:(c + 1) * C]
        Lc = Lf[0:C]
        for c in range(1, n):
            Lc = Lc + Lf[c * C:(c + 1) * C]
        lc_sc[h] = Lc
        eg = jnp.exp(gc)
        rhs_sc[h, :, 0:GDN_DV] = bt * v
        rhs_sc[h, :, GDN_DV:] = (bt * eg) * k
        qe = (eg * q).astype(BF16)
        for c in range(n):
            wq_sc[h, c, C:2 * C, :] = qe[c * C:(c + 1) * C]
        kd_sc[h] = (k * jnp.exp(glast[:, h:h + 1] - gc)).astype(BF16)

    for h0 in range(0, H, INV_GROUP):
        hs = range(h0, h0 + INV_GROUP)
        t_cats = _tri_inv_cat([lc_sc[h] for h in hs], masks, bd_b, n)
        for h, t_cat in zip(hs, t_cats):
            t_bd = jnp.where(bd, jnp.concatenate([t_cat] * n, axis=0), 0.0)
            uw = _dot3(t_bd, rhs_sc[h])
            uv_sc[h] = uw[:, :GDN_DV]
            w = uw[:, GDN_DV:].astype(BF16)
            for c in range(n):
                wq_sc[h, c, 0:C, :] = w[c * C:(c + 1) * C]

    def chunk(c, carry):
        r0 = pl.multiple_of(c * C, C)
        rows = pl.ds(r0, C)
        gl = gl_sc[pl.ds(r0, 1), :]
        S = [S_sc[h] for h in range(H)]
        wq = [jnp.dot(wq_sc[h, c], S[h].astype(BF16), preferred_element_type=F32) for h in range(H)]
        Ub = [(uv_sc[h, rows, :] - wq[h][:C]).astype(BF16) for h in range(H)]
        o = [wq[h][C:] + jnp.dot(qk_sc[h, c], Ub[h], preferred_element_type=F32) for h in range(H)]
        for h in range(H):
            S_sc[h] = gl[:, h:h + 1] * S[h] + _dot_tn(kd_sc[h, rows, :], Ub[h])
        for h in range(H):
            zz = z_ref[0, rows, h * GDN_DV:(h + 1) * GDN_DV]
            o_ref[0, rows, h * GDN_DV:(h + 1) * GDN_DV] = (_rms(o[h], ng) * _silu(zz)).astype(BF16)
        return carry

    lax.fori_loop(0, n, chunk, 0, unroll=True)

    @pl.when(i == nt - 1)
    def _():
        S_ref[0] = S_sc[...]


def _gdn_prompt(u, buf8, p, l, tt):
    B, T, _ = u.shape
    Cc = GDN_CONV_C
    H = GDN_HEADS
    return pl.pallas_call(
        functools.partial(_gdn_kernel, tt=tt),
        grid=(B, T // tt),
        in_specs=[pl.BlockSpec((1, tt, Cc), lambda b, i: (b, i, U_QKV // Cc)),
                  pl.BlockSpec((1, tt, GDN_VW), lambda b, i: (b, i, U_Z // GDN_VW)),
                  pl.BlockSpec((1, tt, W_AB), lambda b, i: (b, i, U_AB // W_AB)),
                  pl.BlockSpec((None, 1, SUBLANES, Cc), lambda b, i: (0, b, 0, 0)),
                  _lspec((CONV_W, Cc), l), _lspec((1, LANES), l), _lspec((1, LANES), l), _lspec((1, GDN_DV), l)],
        out_specs=[pl.BlockSpec((1, tt, GDN_VW), lambda b, i: (b, i, 0)),
                   pl.BlockSpec((1, CONV_W - 1, Cc), lambda b, i: (b, 0, 0)),
                   pl.BlockSpec((1, H, GDN_DK, GDN_DV), lambda b, i: (b, 0, 0, 0))],
        out_shape=[jax.ShapeDtypeStruct((B, T, GDN_VW), BF16),
                   jax.ShapeDtypeStruct((B, CONV_W - 1, Cc), F32),
                   jax.ShapeDtypeStruct((B, H, GDN_DK, GDN_DV), F32)],
        scratch_shapes=[pltpu.VMEM((tt + SUBLANES, Cc), F32),
                        pltpu.VMEM((tt, Cc), F32),
                        pltpu.VMEM((H, GDN_DK, GDN_DV), F32),
                        pltpu.VMEM((H, tt // GDN_CHUNK, 2 * GDN_CHUNK, GDN_DK), BF16),
                        pltpu.VMEM((H, tt, GDN_DV), F32),
                        pltpu.VMEM((H, tt // GDN_CHUNK, GDN_CHUNK, GDN_CHUNK), BF16),
                        pltpu.VMEM((H, tt, GDN_DK), BF16),
                        pltpu.VMEM((tt, LANES), F32),
                        pltpu.VMEM((H, GDN_CHUNK, tt), F32),
                        pltpu.VMEM((H, tt, GDN_DV + GDN_DK), F32)],
        name="gdn_chunked",
        compiler_params=_cparams(("parallel", "arbitrary")),
    )(u, u, u, buf8, p["cw"], p["alog"], p["dtb"], p["ng"])


def _gdn_step_kernel(qkv_ref, z_ref, ab_ref, buf_ref, S0_ref, cw_ref, alog_ref, dtb_ref, ng_ref,
                     o_ref, nbuf_ref, S_ref, xbuf):
    H = GDN_HEADS
    xbuf[0:SUBLANES, :] = buf_ref[0]
    xbuf[SUBLANES:SUBLANES + 1, :] = qkv_ref[0]
    lo = SUBLANES - (CONV_W - 1)
    y = cw_ref[0:1, :] * xbuf[lo:lo + 1, :]
    for j in range(1, CONV_W):
        y = y + cw_ref[j:j + 1, :] * xbuf[lo + j:lo + j + 1, :]
    y = _silu(y)
    nbuf_ref[0] = xbuf[lo + 1:SUBLANES + 1, :]

    ab = ab_ref[0]
    g_all = -jnp.exp(alog_ref[...]) * _softplus(ab + dtb_ref[...])
    beta_all = _sigmoid(ab)
    ii = lax.broadcasted_iota(jnp.int32, (GDN_DK, GDN_DK), 0)
    jj = lax.broadcasted_iota(jnp.int32, (GDN_DK, GDN_DK), 1)
    eye = ii == jj
    ng = ng_ref[...]
    for h in range(H):
        q = y[:, h * GDN_DK:(h + 1) * GDN_DK]
        k = y[:, GDN_KW + h * GDN_DK:GDN_KW + (h + 1) * GDN_DK]
        v = y[:, 2 * GDN_KW + h * GDN_DV:2 * GDN_KW + (h + 1) * GDN_DV]
        q = q * lax.rsqrt(jnp.sum(q * q, axis=-1, keepdims=True) + EPS) * (GDN_DK ** -0.5)
        k = k * lax.rsqrt(jnp.sum(k * k, axis=-1, keepdims=True) + EPS)
        eg = jnp.exp(g_all[:, h:h + 1])
        bt = beta_all[:, H + h:H + h + 1]
        S = S0_ref[0, h]
        Sb = S.astype(BF16)
        kb = k.astype(BF16)
        qb = q.astype(BF16)
        u = bt * (v - eg * jnp.dot(kb, Sb, preferred_element_type=F32))
        ub = u.astype(BF16)
        qk = jnp.sum(qb.astype(F32) * kb.astype(F32), axis=-1, keepdims=True)
        o = eg * jnp.dot(qb, Sb, preferred_element_type=F32) + qk.astype(BF16).astype(F32) * ub.astype(F32)
        kdiag = jnp.where(eye, jnp.broadcast_to(kb.astype(F32), (GDN_DK, GDN_DK)), 0.0).astype(BF16)
        urows = jnp.broadcast_to(ub, (GDN_DK, GDN_DV))
        S_ref[0, h] = eg * S + jnp.dot(kdiag, urows, preferred_element_type=F32)
        zz = z_ref[0, :, h * GDN_DV:(h + 1) * GDN_DV]
        o_ref[0, :, h * GDN_DV:(h + 1) * GDN_DV] = (_rms(o, ng) * _silu(zz)).astype(BF16)


def _gdn_step(u, buf8, S0, p, l):
    B = u.shape[0]
    Cc = GDN_CONV_C
    H = GDN_HEADS
    return pl.pallas_call(
        _gdn_step_kernel,
        grid=(B,),
        in_specs=[pl.BlockSpec((1, 1, Cc), lambda b: (b, 0, U_QKV // Cc)),
                  pl.BlockSpec((1, 1, GDN_VW), lambda b: (b, 0, U_Z // GDN_VW)),
                  pl.BlockSpec((1, 1, W_AB), lambda b: (b, 0, U_AB // W_AB)),
                  pl.BlockSpec((None, 1, SUBLANES, Cc), lambda b: (l, b, 0, 0)),
                  pl.BlockSpec((None, 1, H, GDN_DK, GDN_DV), lambda b: (l, b, 0, 0, 0)),
                  _lspec((CONV_W, Cc), l), _lspec((1, LANES), l), _lspec((1, LANES), l), _lspec((1, GDN_DV), l)],
        out_specs=[pl.BlockSpec((1, 1, GDN_VW), lambda b: (b, 0, 0)),
                   pl.BlockSpec((1, CONV_W - 1, Cc), lambda b: (b, 0, 0)),
                   pl.BlockSpec((1, H, GDN_DK, GDN_DV), lambda b: (b, 0, 0, 0))],
        out_shape=[jax.ShapeDtypeStruct((B, 1, GDN_VW), BF16),
                   jax.ShapeDtypeStruct((B, CONV_W - 1, Cc), F32),
                   jax.ShapeDtypeStruct((B, H, GDN_DK, GDN_DV), F32)],
        scratch_shapes=[pltpu.VMEM((2 * SUBLANES, Cc), F32)],
        name="gdn_step",
        compiler_params=_cparams(("parallel",)),
    )(u, u, u, buf8, S0, p["cw"], p["alog"], p["dtb"], p["ng"])


def _mla_prep_kernel(mq_ref, mkv_ref, cos_ref, sin_ref, gq_ref, gkv_ref, wq_ref, wkv_ref,
                     q_ref, k_ref, v_ref, ckv_ref, kr_ref):
    H = MLA_HEADS
    W = LANES * H
    cos = cos_ref[0]
    sin = sin_ref[0]
    cq = _rms(mq_ref[0], gq_ref[...])
    qa = _dot(cq, wq_ref[...])
    mkv = mkv_ref[0]
    ckv = _rms(mkv[:, :KV_LORA], gkv_ref[...])
    ckv_ref[0] = ckv
    kro = mkv[:, KV_LORA:KV_LORA + LANES] * cos + mkv[:, KV_LORA + LANES:KV_LORA + 2 * LANES] * sin
    kr_ref[0] = kro[:, ROPE_LANE0:ROPE_LANE0 + QK_ROPE]
    kv = _dot(ckv, wkv_ref[...])
    for h in range(H):
        sl = slice(h * LANES, (h + 1) * LANES)
        qh = qa[:, sl] * cos + qa[:, W + h * LANES:W + (h + 1) * LANES] * sin
        q_ref[0, :, sl] = (qh * MLA_SCALE).astype(BF16)
        k_ref[0, :, sl] = (kv[:, sl] + kro).astype(BF16)
    v_ref[0] = kv[:, W:].astype(BF16)


def _mla_prep(u, cos, sin, p, l, bm):
    B, T, _ = u.shape
    W = LANES * MLA_HEADS
    return pl.pallas_call(
        _mla_prep_kernel,
        grid=(B, T // bm),
        in_specs=[pl.BlockSpec((1, bm, Q_LORA), lambda b, i: (b, i, U_MQ // Q_LORA)),
                  pl.BlockSpec((1, bm, W_MKV), lambda b, i: (b, i, U_MKV // W_MKV)),
                  pl.BlockSpec((1, bm, LANES), lambda b, i: (0, i, 0)),
                  pl.BlockSpec((1, bm, LANES), lambda b, i: (0, i, 0)),
                  _lspec((1, Q_LORA), l), _lspec((1, KV_LORA), l),
                  _lspec((Q_LORA, 2 * W), l), _lspec((KV_LORA, 2 * W), l)],
        out_specs=[pl.BlockSpec((1, bm, W), lambda b, i: (b, i, 0)),
                   pl.BlockSpec((1, bm, W), lambda b, i: (b, i, 0)),
                   pl.BlockSpec((1, bm, W), lambda b, i: (b, i, 0)),
                   pl.BlockSpec((1, bm, KV_LORA), lambda b, i: (b, i, 0)),
                   pl.BlockSpec((1, bm, QK_ROPE), lambda b, i: (b, i, 0))],
        out_shape=[jax.ShapeDtypeStruct((B, T, W), BF16),
                   jax.ShapeDtypeStruct((B, T, W), BF16),
                   jax.ShapeDtypeStruct((B, T, W), BF16),
                   jax.ShapeDtypeStruct((B, T, KV_LORA), F32),
                   jax.ShapeDtypeStruct((B, T, QK_ROPE), F32)],
        name="mla_prep",
        compiler_params=_cparams(("parallel", "parallel")),
    )(u, u, cos, sin, p["gq"], p["gkv"], p["wq"], p["wkv"])


NEG_BIG = -1e30


FLASH_HEADS = 4
FLASH_TK = 512


def _flash_kernel(q_ref, k_ref, v_ref, o_ref, *, tq):
    qi = pl.program_id(2)
    hp = FLASH_HEADS
    heads = [slice(h * LANES, (h + 1) * LANES) for h in range(hp)]
    qs = [q_ref[0, :, sl] for sl in heads]
    tk = min(FLASH_TK, tq)
    n_full = (qi * tq) // tk

    def step(j, carry, diagonal):
        ks = pl.ds(pl.multiple_of(j * tk, tk), tk)
        if diagonal:
            row = qi * tq + lax.broadcasted_iota(jnp.int32, (tq, tk), 0)
            col = j * tk + lax.broadcasted_iota(jnp.int32, (tq, tk), 1)
            causal = col <= row
        out = []
        for h, sl in enumerate(heads):
            m, l, acc = carry[h]
            s = lax.dot_general(qs[h], k_ref[0, ks, sl], (((1,), (1,)), ((), ())), preferred_element_type=F32)
            if diagonal:
                s = jnp.where(causal, s, NEG_BIG)
            m_new = jnp.maximum(m, jnp.max(s, axis=-1, keepdims=True))
            alpha = jnp.exp(m - m_new)
            p = jnp.exp(s - m_new)
            l = alpha * l + jnp.sum(p, axis=-1, keepdims=True)
            acc = alpha * acc + jnp.dot(p.astype(BF16), v_ref[0, ks, sl], preferred_element_type=F32)
            out.append((m_new, l, acc))
        return tuple(out)

    init = tuple((jnp.full((tq, 1), NEG_BIG, F32), jnp.zeros((tq, 1), F32), jnp.zeros((tq, LANES), F32))
                 for _ in heads)
    carry = lax.fori_loop(0, n_full, lambda j, c: step(j, c, False), init)
    for t in range(tq // tk):
        carry = step(n_full + t, carry, True)
    for h, sl in enumerate(heads):
        _, l, acc = carry[h]
        o_ref[0, :, sl] = (acc / l).astype(BF16)


def _flash(q, k, v, tq):
    B, T, W = q.shape
    wb = FLASH_HEADS * LANES
    H = W // wb
    return pl.pallas_call(
        functools.partial(_flash_kernel, tq=tq),
        grid=(B, H, T // tq),
        in_specs=[pl.BlockSpec((1, tq, wb), lambda b, h, i: (b, i, h)),
                  pl.BlockSpec((1, T, wb), lambda b, h, i: (b, 0, h)),
                  pl.BlockSpec((1, T, wb), lambda b, h, i: (b, 0, h))],
        out_specs=pl.BlockSpec((1, tq, wb), lambda b, h, i: (b, i, h)),
        out_shape=jax.ShapeDtypeStruct((B, T, W), BF16),
        name="mla_flash",
        compiler_params=_cparams(("parallel", "parallel", "arbitrary")),
    )(q, k, v)


DEC_GROUP = 32
DEC_SLOTS = 3


def _sattn_kernel(pt_ref, q_ref, cn_ref, krn_ref, wuk_ref, wuv_ref, ckv_hbm, krt_hbm, o_ref,
                  ck_buf, kr_buf, sem, *, layer, n_pages):
    H = MLA_HEADS
    G = DEC_GROUP
    n_groups = n_pages // G
    b = pl.program_id(0)
    nb = pl.num_programs(0)

    total = nb * n_groups
    ahead = DEC_SLOTS - 1

    def group_copies(x):
        seq, g, slot = x // n_groups, x % n_groups, x % DEC_SLOTS
        out = []
        for k in range(G):
            page = pt_ref[seq, g * G + k]
            out.append(pltpu.make_async_copy(ckv_hbm.at[layer, page],
                                             ck_buf.at[slot, pl.ds(k * PAGE_SIZE, PAGE_SIZE), :], sem.at[0, slot]))
            out.append(pltpu.make_async_copy(krt_hbm.at[layer, page],
                                             kr_buf.at[slot, :, pl.ds(k * PAGE_SIZE, PAGE_SIZE)], sem.at[1, slot]))
        return out

    @pl.when(b == 0)
    def _():
        for x in range(ahead):
            for c in group_copies(x):
                c.start()

    q = q_ref[0]
    q_rope = q[:, ROPE_LANE0:ROPE_LANE0 + QK_ROPE]
    rr = lax.broadcasted_iota(jnp.int32, (H, H * LANES), 0)
    cc = lax.broadcasted_iota(jnp.int32, (H, H * LANES), 1)
    q_bd = jnp.where(cc // LANES == rr, jnp.concatenate([q] * H, axis=1), jnp.zeros((), BF16))
    qlat = jnp.dot(q_bd, wuk_ref[...], preferred_element_type=F32).astype(BF16)

    def body(g, carry):
        m, l, acc = carry
        x = b * n_groups + g
        slot = x % DEC_SLOTS

        @pl.when(x + ahead < total)
        def _():
            for c in group_copies(x + ahead):
                c.start()

        for c in group_copies(x):
            c.wait()
        ck = ck_buf[slot].astype(BF16)
        kr_t = kr_buf[slot].astype(BF16)
        s = _dot_nt(qlat, ck) + jnp.dot(q_rope, kr_t, preferred_element_type=F32)
        m_new = jnp.maximum(m, jnp.max(s, axis=-1, keepdims=True))
        alpha = jnp.exp(m - m_new)
        p = jnp.exp(s - m_new)
        l = alpha * l + jnp.sum(p, axis=-1, keepdims=True)
        acc = alpha * acc + jnp.dot(p.astype(BF16), ck, preferred_element_type=F32)
        return m_new, l, acc

    init = (jnp.full((H, 1), NEG_BIG, F32), jnp.zeros((H, 1), F32), jnp.zeros((H, KV_LORA), F32))
    m, l, acc = lax.fori_loop(0, n_groups, body, init)

    cn = cn_ref[0].astype(BF16).astype(F32)
    krn = krn_ref[0].astype(BF16).astype(F32)
    s_n = (jnp.sum(qlat.astype(F32) * cn, axis=-1, keepdims=True)
           + jnp.sum(q_rope.astype(F32) * krn, axis=-1, keepdims=True))
    m_f = jnp.maximum(m, s_n)
    a_f = jnp.exp(m - m_f)
    p_n = jnp.exp(s_n - m_f)
    l_f = a_f * l + p_n
    o_lat = (a_f * acc + p_n.astype(BF16).astype(F32) * cn) / l_f
    o_all = jnp.dot(o_lat.astype(BF16), wuv_ref[...], preferred_element_type=F32)
    o_ref[0] = jnp.sum(jnp.where(cc // LANES == rr, o_all, 0.0), axis=0, keepdims=True).astype(BF16)


def _sample_attention(page_table, q, ckv_new, kr_new, wuk_s, wuv_c, cache_ckv, cache_krope_t, layer):
    Bd = q.shape[0]
    H = MLA_HEADS
    n_pages = page_table.shape[1]
    assert n_pages % DEC_GROUP == 0 and Bd * (n_pages // DEC_GROUP) >= DEC_SLOTS
    q3 = q.reshape(Bd, H, LANES)
    grid_spec = pltpu.PrefetchScalarGridSpec(
        num_scalar_prefetch=1,
        grid=(Bd,),
        in_specs=[pl.BlockSpec((1, H, LANES), lambda b, pt: (b, 0, 0)),
                  pl.BlockSpec((1, 1, KV_LORA), lambda b, pt: (b, 0, 0)),
                  pl.BlockSpec((1, 1, QK_ROPE), lambda b, pt: (b, 0, 0)),
                  _lspec((H * LANES, KV_LORA), layer), _lspec((KV_LORA, H * V_HEAD), layer),
                  pl.BlockSpec(memory_space=pl.ANY), pl.BlockSpec(memory_space=pl.ANY)],
        out_specs=pl.BlockSpec((1, 1, H * V_HEAD), lambda b, pt: (b, 0, 0)),
        scratch_shapes=[pltpu.VMEM((DEC_SLOTS, DEC_GROUP * PAGE_SIZE, KV_LORA), F32),
                        pltpu.VMEM((DEC_SLOTS, QK_ROPE, DEC_GROUP * PAGE_SIZE), F32),
                        pltpu.SemaphoreType.DMA((2, DEC_SLOTS))],
    )
    return pl.pallas_call(
        functools.partial(_sattn_kernel, layer=layer, n_pages=n_pages),
        grid_spec=grid_spec,
        out_shape=jax.ShapeDtypeStruct((Bd, 1, H * V_HEAD), BF16),
        name="mla_decode",
        compiler_params=_cparams(("arbitrary",)),
    )(page_table, q3, ckv_new, kr_new, wuk_s, wuv_c, cache_ckv, cache_krope_t)


def _merge_kernel(x_ref, org_ref, ogdn_ref, omla_ref, ga_ref, gb_ref, gc_ref, gt_ref,
                  wrg_ref, wgdn_ref, wmla_ref, wo_ref, o_ref):
    d = functools.partial(jnp.dot, preferred_element_type=F32)
    m = (_sigmoid(ga_ref[0]) * d(org_ref[0], wrg_ref[...])
         + _sigmoid(gb_ref[0]) * d(ogdn_ref[0], wgdn_ref[...])
         + _sigmoid(gc_ref[0]) * d(omla_ref[0], wmla_ref[...]))
    o_ref[0] = x_ref[0] + gt_ref[...] * d(m.astype(BF16), wo_ref[...])


def _merge(x, o_rg, o_gdn, o_mla, u, mod, p, l, bm):
    B, T, D = x.shape
    tok = lambda w, col: pl.BlockSpec((1, bm, w), lambda b, i: (b, i, col))
    return pl.pallas_call(
        _merge_kernel,
        grid=(B, T // bm),
        in_specs=[tok(D, 0), tok(D_RNN, 0), tok(GDN_VW, 0), tok(D, 0),
                  tok(D, U_GA // D), tok(D, U_GB // D), tok(D, U_GC // D),
                  mod.spec(MOD_GT1, bm),
                  _lspec((D_RNN, D), l), _lspec((GDN_VW, D), l), _lspec((D, D), l), _lspec((D, D), l)],
        out_specs=tok(D, 0),
        out_shape=jax.ShapeDtypeStruct((B, T, D), F32),
        name="branch_merge",
        compiler_params=_cparams(("parallel", "parallel")),
    )(x, o_rg, o_gdn, o_mla, u, u, u, mod.array, p["wrg"], p["wgdn"], p["wmla"], p["wo"])


def _ffn_kernel(x_ref, sc_ref, sh_ref, gt_ref, g_ref, gf_ref, wg_ref, wu_ref, wd_ref, o_ref, hb_ref, acc_ref, *, final):
    j = pl.program_id(2)

    @pl.when(j == 0)
    def _():
        h = _rms(x_ref[0], g_ref[...]) * (1.0 + sc_ref[...]) + sh_ref[...]
        hb_ref[...] = h.astype(BF16)
        acc_ref[...] = jnp.zeros(acc_ref.shape, F32)

    hb = hb_ref[...]
    gate = jnp.dot(hb, wg_ref[...], preferred_element_type=F32)
    up = jnp.dot(hb, wu_ref[...], preferred_element_type=F32)
    acc_ref[...] += jnp.dot((_silu(gate) * up).astype(BF16), wd_ref[...], preferred_element_type=F32)

    @pl.when(j == pl.num_programs(2) - 1)
    def _():
        y = x_ref[0] + gt_ref[...] * acc_ref[...]
        o_ref[0] = _rms(y, gf_ref[...]) if final else y


def _ffn(x, mod, g, g_final, w_in_b, w_out_b, l, bm, bf, final):
    B, T, D = x.shape
    nf = D_FF // bf
    return pl.pallas_call(
        functools.partial(_ffn_kernel, final=final),
        grid=(B, T // bm, nf),
        in_specs=[pl.BlockSpec((1, bm, D), lambda b, i, j: (b, i, 0)),
                  mod.spec(MOD_SC2, bm), mod.spec(MOD_SH2, bm), mod.spec(MOD_GT2, bm),
                  _lspec((1, D), l),
                  pl.BlockSpec((1, D), lambda b, i, j: (0, 0)),
                  pl.BlockSpec((None, D, bf), lambda b, i, j: (l, 0, j)),
                  pl.BlockSpec((None, D, bf), lambda b, i, j: (l, 0, nf + j)),
                  pl.BlockSpec((None, bf, D), lambda b, i, j: (l, j, 0))],
        out_specs=pl.BlockSpec((1, bm, D), lambda b, i, j: (b, i, 0)),
        out_shape=jax.ShapeDtypeStruct((B, T, D), F32),
        scratch_shapes=[pltpu.VMEM((bm, D), BF16), pltpu.VMEM((bm, D), F32)],
        name="swiglu",
        compiler_params=_cparams(("parallel", "parallel", "arbitrary")),
    )(x, mod.array, mod.array, mod.array, g, g_final, w_in_b, w_in_b, w_out_b)


def _rot_half(w):
    half = QK_ROPE // 2
    return jnp.concatenate([-w[..., half:], w[..., :half]], axis=-1)


def _pack_w_in(w):
    L, D, _ = w.shape
    wt = jnp.swapaxes(w, 1, 2)
    offs = np.cumsum((0,) + IN_SIZES)
    rx, ry, qkv, z, a, b, mq, mkv, gate = [wt[:, offs[i]:offs[i + 1]] for i in range(len(IN_SIZES))]
    zeros = lambda n: jnp.zeros((L, n, D), w.dtype)
    kr = mkv[:, KV_LORA:]
    half = QK_ROPE // 2
    kr_rot = jnp.concatenate([-kr[:, half:], kr[:, :half]], axis=1)
    tail = LANES - ROPE_LANE0 - QK_ROPE
    rows = [rx, ry, zeros(U_QKV - 2 * D_RNN), qkv, z, gate,
            a, b, zeros(W_AB - 2 * GDN_HEADS), mq,
            mkv[:, :KV_LORA], zeros(ROPE_LANE0), kr, zeros(tail), zeros(ROPE_LANE0), kr_rot, zeros(tail)]
    out = jnp.concatenate(rows, axis=1).astype(BF16)
    assert out.shape[1] == N_U
    return out


def _pack_w_uq(w):
    H = MLA_HEADS
    L = w.shape[0]
    w = w.reshape(L, Q_LORA, H, QK_NOPE + QK_ROPE)
    nope, rope = w[..., :QK_NOPE], w[..., QK_NOPE:]
    tail = jnp.zeros((L, Q_LORA, H, LANES - QK_NOPE - QK_ROPE), w.dtype)
    a = jnp.concatenate([nope, rope, tail], axis=-1).reshape(L, Q_LORA, H * LANES)
    b = jnp.concatenate([jnp.zeros_like(nope), _rot_half(rope), tail], axis=-1).reshape(L, Q_LORA, H * LANES)
    return jnp.concatenate([a, b], axis=-1).astype(BF16)


def _pack_w_ukv(w):
    H = MLA_HEADS
    L = w.shape[0]
    w3 = w.reshape(L, KV_LORA, H, QK_NOPE + V_HEAD)
    w_uk, w_uv = w3[..., :QK_NOPE], w3[..., QK_NOPE:]
    kpad = jnp.concatenate([w_uk, jnp.zeros((L, KV_LORA, H, LANES - QK_NOPE), w.dtype)], axis=-1)
    wuv_c = w_uv.reshape(L, KV_LORA, H * V_HEAD)
    wkv_p = jnp.concatenate([kpad.reshape(L, KV_LORA, H * LANES), wuv_c], axis=-1)
    wuk_s = jnp.transpose(kpad, (0, 2, 3, 1)).reshape(L, H * LANES, KV_LORA)
    return wkv_p.astype(BF16), wuk_s.astype(BF16), wuv_c.astype(BF16)


def _rope_tables(T, pos0):
    inv = ROPE_BASE ** (-jnp.arange(0, QK_ROPE, 2, dtype=F32) / QK_ROPE)
    ang = (jnp.arange(T, dtype=F32) + pos0)[:, None] * inv[None, :]
    cos, sin = jnp.cos(ang), jnp.sin(ang)
    tail = jnp.zeros((T, LANES - ROPE_LANE0 - QK_ROPE), F32)
    cos_t = jnp.concatenate([jnp.ones((T, ROPE_LANE0), F32), cos, cos, tail], axis=1)
    sin_t = jnp.concatenate([jnp.zeros((T, ROPE_LANE0), F32), sin, sin, tail], axis=1)
    return cos_t[None], sin_t[None]


def kernel(x_prompt, x_sample, cache_ckv, cache_krope, state_rg_conv, state_rg_h, state_gdn_conv, state_gdn_S,
           page_table, c_prompt, c_sample, w_ada, b_ada, g_norm1, g_norm2, w_in, rg_conv_w, rg_conv_b, rg_wa,
           rg_ba, rg_wx, rg_bx, rg_lambda, gdn_conv_w, gdn_A_log, gdn_dt_bias, gdn_norm_g, mla_q_norm_g, w_uq,
           mla_kv_norm_g, w_ukv, w_rg_proj, w_gdn_proj, w_mla_proj, w_o, w_ffn_in, w_ffn_out, g_final):
    L = w_in.shape[0]
    Bp, T, D = x_prompt.shape
    Bd = x_sample.shape[0]
    past_len = page_table.shape[1] * PAGE_SIZE

    mod = _modulation(jnp.concatenate([c_sample, c_prompt], axis=0), w_ada, b_ada)
    cos_p, sin_p = _rope_tables(T, 0.0)
    cos_s, sin_s = _rope_tables(1, float(past_len))
    cos_s = jnp.broadcast_to(cos_s, (1, Bd, LANES))
    sin_s = jnp.broadcast_to(sin_s, (1, Bd, LANES))

    row = lambda v: v.reshape(L, 1, v.shape[-1])
    lane_row = lambda v: jnp.pad(v, ((0, 0), (0, LANES - v.shape[-1]))).reshape(L, 1, LANES)
    pad8 = lambda s: jnp.pad(s, ((0, 0), (0, 0), (SUBLANES - (CONV_W - 1), 0), (0, 0)))
    g1, g2, gf = row(g_norm1), row(g_norm2), g_final.reshape(1, D)
    w_in_p = _pack_w_in(w_in)
    wkv_p, wuk_s, wuv_c = _pack_w_ukv(w_ukv)
    rg_p = dict(cw=rg_conv_w, cb=row(rg_conv_b), wa=rg_wa.astype(BF16), ba=row(rg_ba),
                wx=rg_wx.astype(BF16), bx=row(rg_bx), lam=row(rg_lambda))
    gdn_p = dict(cw=gdn_conv_w, alog=lane_row(gdn_A_log), dtb=lane_row(gdn_dt_bias), ng=row(gdn_norm_g))
    mla_p = dict(gq=row(mla_q_norm_g), gkv=row(mla_kv_norm_g), wq=_pack_w_uq(w_uq), wkv=wkv_p)
    mrg_p = dict(wrg=w_rg_proj.astype(BF16), wgdn=w_gdn_proj.astype(BF16), wmla=w_mla_proj.astype(BF16),
                 wo=w_o.astype(BF16))
    wfi, wfo = w_ffn_in.astype(BF16), w_ffn_out.astype(BF16)
    rg_buf_s, gdn_buf_s = pad8(state_rg_conv), pad8(state_gdn_conv)
    rg_h_s = state_rg_h.reshape(L, Bd, 1, D_RNN)
    rg_buf_0 = jnp.zeros((1, Bp, SUBLANES, D_RNN), F32)
    rg_h_0 = jnp.zeros((1, Bp, 1, D_RNN), F32)
    gdn_buf_0 = jnp.zeros((1, Bp, SUBLANES, GDN_CONV_C), F32)
    cache_krope_t = jnp.swapaxes(cache_krope, 2, 3)

    xp = x_prompt
    xs = x_sample.reshape(1, Bd, D)
    outs_p, outs_s = [], []
    for l in range(L):
        final = l == L - 1

        m_p = _Mod(mod, l, per_token=False, row0=Bd)
        u = _in_proj(xp, m_p, g1, w_in_p, l, bm=min(2048, T), bn=1024)
        o_rg, rg_buf, rg_h = _rglru(u, rg_buf_0, rg_h_0, 0, rg_p, l, tt=256)
        o_gdn, gdn_buf, gdn_S = _gdn_prompt(u, gdn_buf_0, gdn_p, l, tt=256)
        q, k, v, ckv, kr = _mla_prep(u, cos_p, sin_p, mla_p, l, bm=512)
        o_mla = _flash(q, k, v, tq=min(512, T))
        x1 = _merge(xp, o_rg, o_gdn, o_mla, u, m_p, mrg_p, l, bm=512)
        xp = _ffn(x1, m_p, g2, gf, wfi, wfo, l, bm=512, bf=1408, final=final)
        outs_p.append((ckv, kr, rg_buf, rg_h.reshape(Bp, D_RNN), gdn_buf, gdn_S))

        m_s = _Mod(mod, l, per_token=True, row0=0)
        us = _in_proj(xs, m_s, g1, w_in_p, l, bm=Bd, bn=1024)
        us_seq = us.reshape(Bd, 1, N_U)
        o_rg, rg_buf, rg_h = _rglru(us_seq, rg_buf_s, rg_h_s, l, rg_p, l, tt=1)
        o_gdn, gdn_buf, gdn_S = _gdn_step(us_seq, gdn_buf_s, state_gdn_S, gdn_p, l)
        q, k, v, ckv, kr = _mla_prep(us, cos_s, sin_s, mla_p, l, bm=Bd)
        o_mla = _sample_attention(page_table, q.reshape(Bd, MLA_HEADS * LANES), ckv.reshape(Bd, 1, KV_LORA),
                                  kr.reshape(Bd, 1, QK_ROPE), wuk_s, wuv_c, cache_ckv, cache_krope_t, l)
        x1 = _merge(xs, o_rg.reshape(1, Bd, D_RNN), o_gdn.reshape(1, Bd, GDN_VW), o_mla.reshape(1, Bd, D),
                    us, m_s, mrg_p, l, bm=Bd)
        xs = _ffn(x1, m_s, g2, gf, wfi, wfo, l, bm=Bd, bf=1408, final=final)
        outs_s.append((ckv.reshape(Bd, 1, KV_LORA), kr.reshape(Bd, 1, QK_ROPE), rg_buf, rg_h.reshape(Bd, D_RNN),
                       gdn_buf, gdn_S))

    stack = lambda outs, i: jnp.stack([o[i] for o in outs])
    return (xp, xs.reshape(Bd, 1, D),
            stack(outs_p, 0), stack(outs_p, 1), stack(outs_p, 2), stack(outs_p, 3), stack(outs_p, 4), stack(outs_p, 5),
            stack(outs_s, 0), stack(outs_s, 1), stack(outs_s, 2), stack(outs_s, 3), stack(outs_s, 4), stack(outs_s, 5))
```
